```python
import numpy as np
import jax
import jax.numpy as jnp
from jax import lax

D_MODEL = 2048
BATCH = 2
SEQ = 4096
DEPTH = 4
DEC_BATCH = 8
DEC_SEQ = 8
PAST_LEN = 16384
PAGE_SIZE = 128

HEAD_DIM = 128
N_A_LAYERS = DEPTH // 2
N_B_LAYERS = DEPTH - N_A_LAYERS
DIL_WINDOWS = (128, 512, 2048)
DIL_RATES = (1, 4, 16)
N_DIL_GROUPS = len(DIL_WINDOWS)
A_HEADS = 8
A_Q_BLOCK = 64
B_HEADS = 16
B_KV_HEADS = 4
B_HEADS_PER_KV = B_HEADS // B_KV_HEADS
N_NSA_BRANCHES = 3
CMP_LEN = 32
CMP_STRIDE = 16
CMP_HIDDEN = 2 * HEAD_DIM
SLC_LEN = 64
N_SELECT = 16
B_WINDOW = 512
B_Q_BLOCK = 64
FORCE_BONUS = 1.0e4
N_EXPERTS = 32
TOP_K = 4
D_FF = D_MODEL
SWIGLU_ALPHA = 1.702
SWIGLU_LIMIT = 7.0
MOE_BLOCK = 128
ROPE_THETA = 10000.0
NORM_EPS = 1e-6
N_MODS = 6

kernel_name = 'yoco_dilated_nsa_moe_decode_step'


def rms_norm(x, g):
    xf = x.astype(jnp.float32)
    y = xf * lax.rsqrt(jnp.mean(xf * xf, axis=-1, keepdims=True) + NORM_EPS)
    return (y * g.astype(jnp.float32)).astype(x.dtype)


def modulate(x, g, shift, scale):
    return rms_norm(x, g) * (1.0 + scale[:, None, :]) + shift[:, None, :]


def rope(x, pos):
    half = HEAD_DIM // 2
    inv_freq = 1.0 / (ROPE_THETA ** (jnp.arange(half, dtype=jnp.float32) * 2.0 / HEAD_DIM))
    ang = pos.astype(jnp.float32)[:, None] * inv_freq[None, :]
    ang = ang.reshape((1, pos.shape[0]) + (1,) * (x.ndim - 3) + (half,))
    cos, sin = jnp.cos(ang), jnp.sin(ang)
    xf = x.astype(jnp.float32)
    x1, x2 = xf[..., :half], xf[..., half:]
    return jnp.concatenate([x1 * cos - x2 * sin, x1 * sin + x2 * cos], axis=-1).astype(x.dtype)


def masked_softmax(s, mask):
    s = jnp.where(mask, s, -1e30)
    e = jnp.where(mask, jnp.exp(s - jnp.max(s, axis=-1, keepdims=True)), 0.0)
    return e / jnp.maximum(jnp.sum(e, axis=-1, keepdims=True), 1e-30)


def project_a(h, w_qkv, qn, kn, pos):
    b_, t_, _ = h.shape
    qkv = (h @ w_qkv).reshape(b_, t_, 3, N_DIL_GROUPS, A_HEADS, HEAD_DIM)
    q = rope(rms_norm(qkv[:, :, 0], qn), pos)
    k = rope(rms_norm(qkv[:, :, 1], kn), pos)
    return q, k, qkv[:, :, 2]


def dilated_group_attention(q, k_all, v_all, q_idx, window, rate):
    offs = jnp.arange(window // rate + 1) * rate
    idx = q_idx[:, None] - offs[None, :]
    valid = idx >= 0
    idx = jnp.maximum(idx, 0)
    kg = jnp.take(k_all, idx, axis=1)
    vg = jnp.take(v_all, idx, axis=1)
    s = jnp.einsum('bqhd,bqjhd->bqhj', q, kg).astype(jnp.float32) * (HEAD_DIM ** -0.5)
    s = jnp.where(valid[None, :, None, :], s, -jnp.inf)
    m = jnp.max(s, axis=-1, keepdims=True)
    e = jnp.exp(s - m)
    den = jnp.sum(e, axis=-1, keepdims=True)
    o = jnp.einsum('bqhj,bqjhd->bqhd', e / den, vg.astype(jnp.float32))
    lse = (m + jnp.log(den))[..., 0]
    return o, lse


def dilated_mixture(q, ks, vs, q_idxs):
    outs, lses = [], []
    for g in range(N_DIL_GROUPS):
        o, lse = dilated_group_attention(q[:, :, g], ks[g], vs[g], q_idxs[g], DIL_WINDOWS[g], DIL_RATES[g])
        outs.append(o)
        lses.append(lse)
    w = jax.nn.softmax(jnp.stack(lses), axis=0)
    o = jnp.sum(w[..., None] * jnp.stack(outs), axis=0)
    return o.reshape(o.shape[0], o.shape[1], A_HEADS * HEAD_DIM)


def mixer_a_prompt(h, w_qkv, w_o, qn, kn):
    b_, t_, _ = h.shape
    q, k, v = project_a(h, w_qkv, qn, kn, jnp.arange(t_))
    ks = [k[:, :, g] for g in range(N_DIL_GROUPS)]
    vs = [v[:, :, g] for g in range(N_DIL_GROUPS)]

    def block(i):
        s0 = i * A_Q_BLOCK
        qi = s0 + jnp.arange(A_Q_BLOCK)
        return dilated_mixture(lax.dynamic_slice_in_dim(q, s0, A_Q_BLOCK, 1), ks, vs, [qi] * N_DIL_GROUPS)

    o = lax.map(block, jnp.arange(t_ // A_Q_BLOCK))
    o = jnp.moveaxis(o, 0, 1).reshape(b_, t_, A_HEADS * HEAD_DIM)
    states = []
    for g in range(N_DIL_GROUPS):
        keep = min(DIL_WINDOWS[g], t_)
        states.append(jnp.stack([ks[g][:, t_ - keep:], vs[g][:, t_ - keep:]], axis=2))
    return o.astype(h.dtype) @ w_o, states


def mixer_a_sample(h, bufs, w_qkv, w_o, qn, kn, pos):
    t_ = h.shape[1]
    q, k, v = project_a(h, w_qkv, qn, kn, pos)
    ks, vs, q_idxs, states = [], [], [], []
    for g in range(N_DIL_GROUPS):
        lb = bufs[g].shape[1]
        ka = jnp.concatenate([bufs[g][:, :, 0], k[:, :, g]], axis=1)
        va = jnp.concatenate([bufs[g][:, :, 1], v[:, :, g]], axis=1)
        ks.append(ka)
        vs.append(va)
        q_idxs.append(lb + jnp.arange(t_))
        keep = min(DIL_WINDOWS[g], lb + t_)
        states.append(jnp.stack([ka[:, lb + t_ - keep:], va[:, lb + t_ - keep:]], axis=2))
    o = dilated_mixture(q, ks, vs, q_idxs)
    return o.astype(h.dtype) @ w_o, states


def shared_kv(x, c, kv_w, pos):
    w_mod_kv, b_mod_kv, g_kv, w_kv, kn_kv = kv_w
    b_, t_, _ = x.shape
    m = (jax.nn.silu(c) @ w_mod_kv + b_mod_kv).reshape(b_, 2, -1)
    h = modulate(x, g_kv, m[:, 0], m[:, 1])
    kv = (h @ w_kv).reshape(b_, t_, N_NSA_BRANCHES, 2, B_KV_HEADS, HEAD_DIM)
    keys = rope(rms_norm(kv[:, :, :, 0], kn_kv[:, None, :]), pos)
    return jnp.stack([keys, kv[:, :, :, 1]], axis=3)


def compress_rows(rows, pe, w1, b1, w2, b2):
    b_, l_ = rows.shape[:2]
    n_chunks = l_ // CMP_STRIDE
    r = CMP_LEN // CMP_STRIDE
    ch = rows[:, :n_chunks * CMP_STRIDE].reshape(b_, n_chunks, CMP_STRIDE, B_KV_HEADS, HEAD_DIM)
    n_cmp = n_chunks - r + 1
    blocks = jnp.concatenate([ch[:, i:i + n_cmp] for i in range(r)], axis=2)
    blocks = blocks + pe[None, None, :, None, :].astype(blocks.dtype)
    hid = jax.nn.gelu(jnp.einsum('bnlgd,ldf->bngf', blocks, w1) + b1)
    return hid @ w2 + b2


def nsa_context(rows4, cmp_w):
    pe, w1, b1, w2, b2 = cmp_w
    kcmp = compress_rows(rows4[:, :, 0], pe[0], w1[0], b1[0], w2[0], b2[0])
    vcmp = compress_rows(rows4[:, :, 1], pe[1], w1[1], b1[1], w2[1], b2[1])
    b_, l_ = rows4.shape[:2]
    n_slc = -(-l_ // SLC_LEN)
    slc = jnp.pad(rows4[:, :, 2:], ((0, 0), (0, n_slc * SLC_LEN - l_), (0, 0), (0, 0), (0, 0)))
    slc = slc.reshape(b_, n_slc, SLC_LEN, 2, B_KV_HEADS, HEAD_DIM).transpose(3, 0, 4, 1, 2, 5)
    return kcmp, vcmp, slc[0], slc[1]


def selection_map(n_cmp, n_slc):
    r_s = SLC_LEN // CMP_STRIDE
    r_c = CMP_LEN // CMP_STRIDE
    mult = np.zeros(r_s + r_c - 1, np.float32)
    for m in range(r_s):
        for n in range(r_c):
            mult[m + n] += 1.0
    off = r_s * np.arange(n_slc)[None, :] - np.arange(n_cmp)[:, None]
    ok = (off >= 0) & (off < mult.shape[0])
    return np.where(ok, mult[np.clip(off, 0, mult.shape[0] - 1)], 0.0).astype(np.float32)


def nsa_queries(h, w_q, qn, w_g, b_g, pos):
    b_, t_, _ = h.shape
    q = rope(rms_norm((h @ w_q).reshape(b_, t_, B_HEADS, HEAD_DIM), qn), pos)
    gates = jax.nn.sigmoid((h @ w_g + b_g).astype(jnp.float32))
    return q, gates


def nsa_attend(q, pos, gates, kcmp, vcmp, kslc, vslc, kwin, vwin, kpos_win):
    b_, tq = q.shape[:2]
    scale = HEAD_DIM ** -0.5
    qg = q.reshape(b_, tq, B_KV_HEADS, B_HEADS_PER_KV, HEAD_DIM)
    n_cmp = kcmp.shape[1]
    cmp_end = jnp.arange(n_cmp) * CMP_STRIDE + (CMP_LEN - 1)
    cmp_mask = (cmp_end[None, :] <= pos[:, None])[None, :, None, None, :]
    s = jnp.einsum('btgqd,bngd->btgqn', qg, kcmp).astype(jnp.float32) * scale
    p_cmp = masked_softmax(s, cmp_mask)
    o_cmp = jnp.einsum('btgqn,bngd->btgqd', p_cmp, vcmp.astype(jnp.float32))
    n_slc = kslc.shape[2]
    imp = jnp.einsum('btgqn,ns->btgs', p_cmp, jnp.asarray(selection_map(n_cmp, n_slc)))
    blk = jnp.arange(n_slc)[None, :]
    cur = (pos // SLC_LEN)[:, None]
    forced = (blk == 0) | (blk == cur) | (blk == cur - 1)
    valid = blk * SLC_LEN <= pos[:, None]
    bonus = jnp.where(forced, FORCE_BONUS, 0.0)
    score = jnp.where(valid[None, :, None, :], imp + bonus[None, :, None, :], -FORCE_BONUS)
    n_sel = min(N_SELECT, n_slc)
    _, sel = lax.top_k(score, n_sel)
    bi = jnp.arange(b_)[:, None, None, None]
    gi = jnp.arange(B_KV_HEADS)[None, None, :, None]
    ks = kslc[bi, gi, sel].reshape(b_, tq, B_KV_HEADS, n_sel * SLC_LEN, HEAD_DIM)
    vs = vslc[bi, gi, sel].reshape(b_, tq, B_KV_HEADS, n_sel * SLC_LEN, HEAD_DIM)
    kpos = (sel[..., None] * SLC_LEN + jnp.arange(SLC_LEN)).reshape(b_, tq, B_KV_HEADS, 1, n_sel * SLC_LEN)
    slc_mask = kpos <= pos[None, :, None, None, None]
    s = jnp.einsum('btgqd,btgmd->btgqm', qg, ks).astype(jnp.float32) * scale
    o_slc = jnp.einsum('btgqm,btgmd->btgqd', masked_softmax(s, slc_mask), vs.astype(jnp.float32))
    rel = pos[:, None] - kpos_win[None, :]
    win_mask = ((rel >= 0) & (rel < B_WINDOW) & (kpos_win[None, :] >= 0))[None, :, None, None, :]
    s = jnp.einsum('btgqd,blgd->btgql', qg, kwin).astype(jnp.float32) * scale
    o_win = jnp.einsum('btgql,blgd->btgqd', masked_softmax(s, win_mask), vwin.astype(jnp.float32))
    g = gates.reshape(b_, tq, B_KV_HEADS, B_HEADS_PER_KV, N_NSA_BRANCHES, 1)
    o = g[..., 0, :] * o_cmp + g[..., 1, :] * o_slc + g[..., 2, :] * o_win
    return o.reshape(b_, tq, B_HEADS * HEAD_DIM)


def build_ctx_prompt(x, c, kv_w, cmp_w):
    b_, t_, _ = x.shape
    kvb = shared_kv(x, c, kv_w, jnp.arange(t_))
    rows = kvb[:, :, :2].reshape(b_, t_, 4, B_KV_HEADS, HEAD_DIM)
    win = kvb[:, :, 2]
    kcmp, vcmp, kslc, vslc = nsa_context(rows, cmp_w)
    win_pad = jnp.pad(win, ((0, 0), (B_WINDOW, 0), (0, 0), (0, 0), (0, 0)))
    keep = min(B_WINDOW, t_)
    ctx = (kcmp, vcmp, kslc, vslc, win_pad[:, :, 0], win_pad[:, :, 1])
    return ctx, (rows, win[:, t_ - keep:])


def build_ctx_sample(x, c, kv_w, cmp_w, past_rows, win_buf, past_len):
    b_, t_, _ = x.shape
    kvb = shared_kv(x, c, kv_w, past_len + jnp.arange(t_))
    rows = kvb[:, :, :2].reshape(b_, t_, 4, B_KV_HEADS, HEAD_DIM)
    kcmp, vcmp, kslc, vslc = nsa_context(jnp.concatenate([past_rows, rows], axis=1), cmp_w)
    lb = win_buf.shape[1]
    win_all = jnp.concatenate([win_buf, kvb[:, :, 2]], axis=1)
    kpos = past_len - lb + jnp.arange(lb + t_)
    keep = min(B_WINDOW, lb + t_)
    ctx = (kcmp, vcmp, kslc, vslc, win_all[:, :, 0], win_all[:, :, 1], kpos)
    return ctx, (rows, win_all[:, lb + t_ - keep:])


def mixer_b_prompt(h, ctx, w_q, qn, w_g, b_g, w_o):
    kcmp, vcmp, kslc, vslc, kwin_pad, vwin_pad = ctx
    b_, t_, _ = h.shape
    q, gates = nsa_queries(h, w_q, qn, w_g, b_g, jnp.arange(t_))
    span = B_WINDOW + B_Q_BLOCK

    def block(i):
        s0 = i * B_Q_BLOCK
        pos = s0 + jnp.arange(B_Q_BLOCK)
        return nsa_attend(lax.dynamic_slice_in_dim(q, s0, B_Q_BLOCK, 1), pos,
                          lax.dynamic_slice_in_dim(gates, s0, B_Q_BLOCK, 1),
                          kcmp, vcmp, kslc, vslc,
                          lax.dynamic_slice_in_dim(kwin_pad, s0, span, 1),
                          lax.dynamic_slice_in_dim(vwin_pad, s0, span, 1),
                          s0 - B_WINDOW + jnp.arange(span))

    o = lax.map(block, jnp.arange(t_ // B_Q_BLOCK))
    o = jnp.moveaxis(o, 0, 1).reshape(b_, t_, B_HEADS * HEAD_DIM)
    return o.astype(h.dtype) @ w_o


def mixer_b_sample(h, ctx, w_q, qn, w_g, b_g, w_o, pos):
    kcmp, vcmp, kslc, vslc, kwin, vwin, kpos = ctx
    q, gates = nsa_queries(h, w_q, qn, w_g, b_g, pos)
    o = nsa_attend(q, pos, gates, kcmp, vcmp, kslc, vslc, kwin, vwin, kpos)
    return o.astype(h.dtype) @ w_o


def expert_ffn(x, w_gu, b_gu, w_dn, b_dn):
    gu = x @ w_gu + b_gu
    gate = jnp.minimum(gu[..., 0::2], SWIGLU_LIMIT)
    up = jnp.clip(gu[..., 1::2], -SWIGLU_LIMIT, SWIGLU_LIMIT)
    glu = gate * jax.nn.sigmoid(SWIGLU_ALPHA * gate)
    return ((up + 1.0) * glu) @ w_dn + b_dn


def moe_ffn(h, router_w, router_b, w_gu, b_gu, w_dn, b_dn):
    b_, t_, d_ = h.shape
    x = h.reshape(b_ * t_, d_)
    n_tok = x.shape[0]
    n_asg = n_tok * TOP_K
    logits = x.astype(jnp.float32) @ router_w.astype(jnp.float32) + router_b.astype(jnp.float32)
    top_logit, top_e = lax.top_k(logits, TOP_K)
    top_w = jax.nn.softmax(top_logit, axis=-1)
    flat_e = top_e.reshape(-1)
    order = jnp.argsort(flat_e)
    e_sorted = flat_e[order]
    tok_sorted = (jnp.arange(n_asg) // TOP_K)[order].astype(jnp.int32)
    w_sorted = top_w.reshape(-1)[order]
    counts = jnp.bincount(flat_e, length=N_EXPERTS)
    padded = (counts + MOE_BLOCK - 1) // MOE_BLOCK * MOE_BLOCK
    start = jnp.cumsum(counts) - counts
    pend = jnp.cumsum(padded)
    pstart = pend - padded
    dest = pstart[e_sorted] + jnp.arange(n_asg) - start[e_sorted]
    n_blocks = -(-n_asg // MOE_BLOCK) + N_EXPERTS
    n_slots = n_blocks * MOE_BLOCK
    slot_tok = jnp.full((n_slots,), n_tok, jnp.int32).at[dest].set(tok_sorted)
    slot_w = jnp.zeros((n_slots,), jnp.float32).at[dest].set(w_sorted)
    block_e = jnp.minimum(jnp.searchsorted(pend, jnp.arange(n_blocks) * MOE_BLOCK, side='right'), N_EXPERTS - 1)
    x_pad = jnp.concatenate([x, jnp.zeros((1, d_), x.dtype)], axis=0)
    xb = x_pad[slot_tok].reshape(n_blocks, MOE_BLOCK, d_)

    def run_block(args):
        xe, e = args
        return expert_ffn(xe, w_gu[e], b_gu[e], w_dn[e], b_dn[e])

    yb = lax.map(run_block, (xb, block_e)).reshape(n_slots, d_)
    y = jax.ops.segment_sum(yb.astype(jnp.float32) * slot_w[:, None], slot_tok, num_segments=n_tok + 1)[:n_tok]
    return y.reshape(b_, t_, d_).astype(h.dtype)


def decoder_layers(x, c, attn_a, build_ctx, attn_b, mods, ffn):
    w_mod, b_mod, g_norm = mods
    router_w, router_b, w_gu, b_gu, w_down, b_down = ffn
    a_states, ctx, kv_state = [], None, None
    for l in range(DEPTH):
        m = (jax.nn.silu(c) @ w_mod[l] + b_mod[l]).reshape(c.shape[0], N_MODS, -1)
        h = modulate(x, g_norm[l, 0], m[:, 0], m[:, 1])
        if l < N_A_LAYERS:
            y, st = attn_a(l, h)
            a_states.append(st)
        else:
            y = attn_b(l - N_A_LAYERS, h, ctx)
        x = x + m[:, 2][:, None, :] * y
        h = modulate(x, g_norm[l, 1], m[:, 3], m[:, 4])
        x = x + m[:, 5][:, None, :] * moe_ffn(h, router_w[l], router_b[l], w_gu[l], b_gu[l], w_down[l], b_down[l])
        if l == N_A_LAYERS - 1:
            ctx, kv_state = build_ctx(x, c)
    dil = [jnp.stack([st[g] for st in a_states]) for g in range(N_DIL_GROUPS)]
    return x, dil, kv_state


def setup_inputs(seed: int = 0) -> dict:
    key = jax.random.key(seed)
    ks = iter(jax.random.split(key, 64))

    def nrm(shape, scale):
        return jax.random.normal(next(ks), shape, jnp.float32) * scale

    def gain(shape):
        return 1.0 + nrm(shape, 0.01)

    D = D_MODEL
    n_pages = PAST_LEN // PAGE_SIZE
    n_used = DEC_BATCH * n_pages
    n_phys = n_used + (n_used + 3) // 4
    inp = {}
    inp['x_prompt'] = nrm((BATCH, SEQ, D), 1.0)
    inp['x_sample'] = nrm((DEC_BATCH, DEC_SEQ, D), 1.0)
    for g in range(N_DIL_GROUPS):
        inp[f'state_dil{g}'] = nrm((N_A_LAYERS, DEC_BATCH, min(DIL_WINDOWS[g], PAST_LEN), 2, A_HEADS, HEAD_DIM), 1.0)
    inp['cache_nsa'] = nrm((n_phys, PAGE_SIZE, 4, B_KV_HEADS, HEAD_DIM), 1.0)
    inp['state_win'] = nrm((DEC_BATCH, min(B_WINDOW, PAST_LEN), 2, B_KV_HEADS, HEAD_DIM), 1.0)
    perm = jax.random.permutation(next(ks), n_phys)[:n_used]
    inp['page_table'] = perm.reshape(DEC_BATCH, n_pages).astype(jnp.int32)
    inp['c_prompt'] = nrm((BATCH, D), 1.0)
    inp['c_sample'] = nrm((DEC_BATCH, D), 1.0)
    inp['w_mod'] = nrm((DEPTH, D, N_MODS * D), 0.5 * D ** -0.5)
    inp['b_mod'] = nrm((DEPTH, N_MODS * D), 0.01)
    inp['g_norm'] = gain((DEPTH, 2, D))
    a_width = N_DIL_GROUPS * A_HEADS * HEAD_DIM
    inp['w_qkv_a'] = nrm((N_A_LAYERS, D, 3 * a_width), D ** -0.5)
    inp['w_o_a'] = nrm((N_A_LAYERS, A_HEADS * HEAD_DIM, D), (A_HEADS * HEAD_DIM) ** -0.5)
    inp['qn_a'] = gain((N_A_LAYERS, HEAD_DIM))
    inp['kn_a'] = gain((N_A_LAYERS, HEAD_DIM))
    inp['w_q_b'] = nrm((N_B_LAYERS, D, B_HEADS * HEAD_DIM), D ** -0.5)
    inp['qn_b'] = gain((N_B_LAYERS, HEAD_DIM))
    inp['w_gate_b'] = nrm((N_B_LAYERS, D, B_HEADS * N_NSA_BRANCHES), D ** -0.5)
    inp['b_gate_b'] = nrm((N_B_LAYERS, B_HEADS * N_NSA_BRANCHES), 0.01)
    inp['w_o_b'] = nrm((N_B_LAYERS, B_HEADS * HEAD_DIM, D), (B_HEADS * HEAD_DIM) ** -0.5)
    inp['w_mod_kv'] = nrm((D, 2 * D), 0.5 * D ** -0.5)
    inp['b_mod_kv'] = nrm((2 * D,), 0.01)
    inp['g_kv'] = gain((D,))
    inp['w_kv'] = nrm((D, N_NSA_BRANCHES * 2 * B_KV_HEADS * HEAD_DIM), D ** -0.5)
    inp['kn_kv'] = gain((N_NSA_BRANCHES, HEAD_DIM))
    inp['cmp_pe'] = nrm((2, CMP_LEN, HEAD_DIM), 0.1)
    inp['cmp_w1'] = nrm((2, CMP_LEN, HEAD_DIM, CMP_HIDDEN), (CMP_LEN * HEAD_DIM) ** -0.5)
    inp['cmp_b1'] = nrm((2, CMP_HIDDEN), 0.01)
    inp['cmp_w2'] = nrm((2, CMP_HIDDEN, HEAD_DIM), CMP_HIDDEN ** -0.5)
    inp['cmp_b2'] = nrm((2, HEAD_DIM), 0.01)
    inp['router_w'] = nrm((DEPTH, D, N_EXPERTS), D ** -0.5)
    inp['router_b'] = nrm((DEPTH, N_EXPERTS), 0.01)
    inp['w_gu'] = nrm((DEPTH, N_EXPERTS, D, 2 * D_FF), D ** -0.5)
    inp['b_gu'] = nrm((DEPTH, N_EXPERTS, 2 * D_FF), 0.01)
    inp['w_down'] = nrm((DEPTH, N_EXPERTS, D_FF, D), D_FF ** -0.5)
    inp['b_down'] = nrm((DEPTH, N_EXPERTS, D), 0.01)
    return inp


def reference(x_prompt, x_sample, state_dil0, state_dil1, state_dil2, cache_nsa, state_win, page_table,
              c_prompt, c_sample, w_mod, b_mod, g_norm, w_qkv_a, w_o_a, qn_a, kn_a,
              w_q_b, qn_b, w_gate_b, b_gate_b, w_o_b, w_mod_kv, b_mod_kv, g_kv, w_kv, kn_kv,
              cmp_pe, cmp_w1, cmp_b1, cmp_w2, cmp_b2, router_w, router_b, w_gu, b_gu, w_down, b_down):
    mods = (w_mod, b_mod, g_norm)
    ffn = (router_w, router_b, w_gu, b_gu, w_down, b_down)
    kv_w = (w_mod_kv, b_mod_kv, g_kv, w_kv, kn_kv)
    cmp_w = (cmp_pe, cmp_w1, cmp_b1, cmp_w2, cmp_b2)

    def attn_a_prompt(l, h):
        return mixer_a_prompt(h, w_qkv_a[l], w_o_a[l], qn_a[l], kn_a[l])

    def ctx_prompt(x, c):
        return build_ctx_prompt(x, c, kv_w, cmp_w)

    def attn_b_prompt(l, h, ctx):
        return mixer_b_prompt(h, ctx, w_q_b[l], qn_b[l], w_gate_b[l], b_gate_b[l], w_o_b[l])

    y_prompt, p_dil, (p_rows, p_win) = decoder_layers(x_prompt, c_prompt, attn_a_prompt, ctx_prompt,
                                                      attn_b_prompt, mods, ffn)

    dec_b, dec_t = x_sample.shape[:2]
    past_len = page_table.shape[1] * PAGE_SIZE
    pos_s = past_len + jnp.arange(dec_t)
    past_rows = cache_nsa[page_table].reshape(dec_b, past_len, 4, B_KV_HEADS, HEAD_DIM)
    dil_bufs = (state_dil0, state_dil1, state_dil2)

    def attn_a_sample(l, h):
        return mixer_a_sample(h, [buf[l] for buf in dil_bufs], w_qkv_a[l], w_o_a[l], qn_a[l], kn_a[l], pos_s)

    def ctx_sample(x, c):
        return build_ctx_sample(x, c, kv_w, cmp_w, past_rows, state_win, past_len)

    def attn_b_sample(l, h, ctx):
        return mixer_b_sample(h, ctx, w_q_b[l], qn_b[l], w_gate_b[l], b_gate_b[l], w_o_b[l], pos_s)

    y_sample, s_dil, (s_rows, s_win) = decoder_layers(x_sample, c_sample, attn_a_sample, ctx_sample,
                                                      attn_b_sample, mods, ffn)

    return (y_prompt, y_sample, p_dil[0], p_dil[1], p_dil[2], p_rows, p_win,
            s_dil[0], s_dil[1], s_dil[2], s_rows, s_win)
```

```python
import functools

import numpy as np
import jax
import jax.numpy as jnp
from jax import lax
from jax.experimental import pallas as pl
from jax.experimental.pallas import tpu as pltpu

F32 = jnp.float32
BF16 = jnp.bfloat16
I32 = jnp.int32

HEAD_DIM = 128
LANES = 128
DIL_WINDOWS = (128, 512, 2048)
DIL_RATES = (1, 4, 16)
A_HEADS = 8
B_HEADS = 16
B_KV_HEADS = 4
B_HEADS_PER_KV = B_HEADS // B_KV_HEADS
CMP_LEN = 32
CMP_STRIDE = 16
SLC_LEN = 64
N_SELECT = 16
B_WINDOW = 512
FORCE_BONUS = 1.0e4
N_EXPERTS = 32
TOP_K = 4
SWIGLU_ALPHA = 1.702
SWIGLU_LIMIT = 7.0
MOE_BLOCK = 128
MOE_TB = 1280
PAGE_SIZE = 128
ROPE_THETA = 10000.0
NORM_EPS = 1e-6
N_MODS = 6
NEG = -1e30
VMEM_LIMIT = 56 * 1024 * 1024

_NT = (((1,), (1,)), ((), ()))


def _params(sem, **kw):
    return pltpu.CompilerParams(dimension_semantics=sem, vmem_limit_bytes=VMEM_LIMIT, **kw)


def _dot(a, b):
    return jnp.dot(a.astype(BF16), b.astype(BF16), preferred_element_type=F32)


def _dot_nt(a, b):
    return lax.dot_general(a.astype(BF16), b.astype(BF16), _NT, preferred_element_type=F32)


def _pick_tile(n, cap, mult=LANES):
    if n <= cap:
        return n
    t = (cap // mult) * mult
    while t > mult and n % t:
        t -= mult
    assert n % t == 0, (n, cap)
    return t


def _rope_tables(pos):
    half = HEAD_DIM // 2
    inv_freq = 1.0 / (ROPE_THETA ** (jnp.arange(half, dtype=F32) * 2.0 / HEAD_DIM))
    ang = pos.astype(F32)[:, None] * inv_freq[None, :]
    c, s = jnp.cos(ang), jnp.sin(ang)
    return jnp.concatenate([c, c], axis=-1), jnp.concatenate([-s, s], axis=-1)


def _modulate(x, g, shift, scale):
    y = x * lax.rsqrt(jnp.mean(x * x, axis=-1, keepdims=True) + NORM_EPS) * g
    return y * (1.0 + scale) + shift


def _head_norm_rope(blk, gain, cos2, sin2):
    y = blk * lax.rsqrt(jnp.mean(blk * blk, axis=-1, keepdims=True) + NORM_EPS) * gain
    return y * cos2 + pltpu.roll(y, HEAD_DIM // 2, 1) * sin2


def _mod_kernel(c_ref, w_ref, b_ref, o_ref):
    c = c_ref[...]
    o_ref[...] = _dot(c * jax.nn.sigmoid(c), w_ref[...]) + b_ref[...]


def mod_vectors(c, w, b):
    n_layers, d, n = w.shape
    nc = c.shape[0]
    tn = _pick_tile(n, max(LANES, (8 << 20) // (4 * d)))
    return pl.pallas_call(
        _mod_kernel,
        out_shape=jax.ShapeDtypeStruct((n_layers, nc, n), F32),
        grid=(n_layers, n // tn),
        in_specs=[pl.BlockSpec((nc, d), lambda l, j: (0, 0)),
                  pl.BlockSpec((None, d, tn), lambda l, j: (l, 0, j)),
                  pl.BlockSpec((None, 1, tn), lambda l, j: (l, 0, j))],
        out_specs=pl.BlockSpec((None, nc, tn), lambda l, j: (l, 0, j)),
        compiler_params=_params(("arbitrary", "arbitrary")),
    )(c, w, b.reshape(n_layers, 1, n))


class Stream:
    def __init__(self, b, t, pos, tm_cap):
        self.b, self.t, self.r = b, t, b * t
        self.per_row_mods = t < LANES
        self.tm = self.r if self.per_row_mods else _pick_tile(t, tm_cap, 8)
        self.tiles_per_seq = 1 if self.per_row_mods else t // self.tm
        cos2, sin2 = _rope_tables(pos)
        if self.per_row_mods:
            cos2, sin2 = jnp.tile(cos2, (b, 1)), jnp.tile(sin2, (b, 1))
        self.cos2, self.sin2 = cos2, sin2

    def mods(self, m):
        if self.per_row_mods:
            return jnp.repeat(m, self.t, axis=0)[None]
        return m[:, None, :]

    def mod_spec(self, d, k, n_grid_axes=2):
        rows = self.r if self.per_row_mods else 1
        tps = self.tiles_per_seq
        return pl.BlockSpec((None, rows, d), lambda i, *_: (i // tps, 0, k))

    def rope_spec(self):
        tps = self.tiles_per_seq
        return pl.BlockSpec((self.tm, HEAD_DIM), lambda i, *_: (i % tps, 0))


def _lin_kernel(flag_ref, gidx_ref, x_ref, g_ref, sh_ref, sc_ref, w_ref, gain_ref, cos_ref, sin_ref, b_ref,
                o_ref, h_scr, *, act):
    del gidx_ref
    j = pl.program_id(1)

    @pl.when(j == 0)
    def _():
        h_scr[...] = _modulate(x_ref[...], g_ref[...], sh_ref[...], sc_ref[...]).astype(BF16)

    acc = jnp.dot(h_scr[...], w_ref[...], preferred_element_type=F32) + b_ref[...]
    tn = acc.shape[1]

    @pl.when(flag_ref[j] == 0)
    def _():
        o_ref[...] = (jax.nn.sigmoid(acc) if act == "sigmoid" else acc).astype(o_ref.dtype)

    @pl.when(flag_ref[j] != 0)
    def _():
        for hh in range(tn // HEAD_DIM):
            sl = slice(hh * HEAD_DIM, (hh + 1) * HEAD_DIM)
            o_ref[:, sl] = _head_norm_rope(acc[:, sl], gain_ref[...], cos_ref[...], sin_ref[...]).astype(o_ref.dtype)


def mod_linear(st, x, g, mods, ksh, ksc, w_bf, *, tn, flags=None, gidx=None, gains=None, bias=None, act=None):
    r, d = x.shape
    n = w_bf.shape[1]
    nj = n // tn
    if flags is None:
        flags = np.zeros((nj,), np.int32)
        gidx = np.zeros((nj,), np.int32)
        gains = jnp.ones((1, HEAD_DIM), F32)
    if bias is None:
        bias = jnp.zeros((n,), F32)
    gains = gains.reshape(-1, 1, HEAD_DIM)
    tm = st.tm
    grid_spec = pltpu.PrefetchScalarGridSpec(
        num_scalar_prefetch=2,
        grid=(r // tm, nj),
        in_specs=[pl.BlockSpec((tm, d), lambda i, j, *_: (i, 0)),
                  pl.BlockSpec((1, d), lambda i, j, *_: (0, 0)),
                  st.mod_spec(d, ksh), st.mod_spec(d, ksc),
                  pl.BlockSpec((d, tn), lambda i, j, *_: (0, j)),
                  pl.BlockSpec((None, 1, HEAD_DIM), lambda i, j, fl, gi: (gi[j], 0, 0)),
                  st.rope_spec(), st.rope_spec(),
                  pl.BlockSpec((1, tn), lambda i, j, *_: (0, j))],
        out_specs=pl.BlockSpec((tm, tn), lambda i, j, *_: (i, j)),
        scratch_shapes=[pltpu.VMEM((tm, d), BF16)])
    return pl.pallas_call(
        functools.partial(_lin_kernel, act=act),
        out_shape=jax.ShapeDtypeStruct((r, n), F32),
        grid_spec=grid_spec,
        compiler_params=_params(("arbitrary", "arbitrary")),
    )(jnp.asarray(flags, I32), jnp.asarray(gidx, I32), x, g.reshape(1, d), mods, mods, w_bf, gains,
      st.cos2, st.sin2, bias.reshape(1, n))


def _out_kernel(o_ref, w_ref, x_ref, gate_ref, y_ref):
    y_ref[...] = x_ref[...] + gate_ref[...] * _dot(o_ref[...], w_ref[...])


def out_proj_residual(st, o, w_bf, x, mods, kgate):
    r, kd = o.shape
    d = x.shape[1]
    tn = _pick_tile(d, 512)
    tm = st.tm
    nj = d // tn
    rows = st.r if st.per_row_mods else 1
    tps = st.tiles_per_seq
    return pl.pallas_call(
        _out_kernel,
        out_shape=jax.ShapeDtypeStruct((r, d), F32),
        grid=(r // tm, nj),
        in_specs=[pl.BlockSpec((tm, kd), lambda i, j: (i, 0)),
                  pl.BlockSpec((kd, tn), lambda i, j: (0, j)),
                  pl.BlockSpec((tm, tn), lambda i, j: (i, j)),
                  pl.BlockSpec((None, rows, tn), lambda i, j: (i // tps, 0, kgate * nj + j))],
        out_specs=pl.BlockSpec((tm, tn), lambda i, j: (i, j)),
        compiler_params=_params(("arbitrary", "arbitrary")),
    )(o, w_bf, x, mods)


def _dil_prompt_kernel(*refs, wr, rates, sup):
    ng = len(rates)
    o_ref = refs[5 * ng]
    kf, vf, m_scr, l_scr, acc_scr = refs[5 * ng + 1:]
    i = pl.program_id(1)
    scale = HEAD_DIM ** -0.5
    qi = lax.broadcasted_iota(I32, (wr, 2 * wr), 0)
    kk = lax.broadcasted_iota(I32, (wr, 2 * wr), 1)
    band = (kk >= qi) & (kk <= qi + wr)
    for g, rate in enumerate(rates):
        q_ref, kp_ref, kc_ref, vp_ref, vc_ref = refs[5 * g:5 * g + 5]
        prev = wr * rate
        kf[0:prev, :] = kp_ref[...]
        kf[prev:prev + sup, :] = kc_ref[...]
        vf[0:prev, :] = vp_ref[...]
        vf[prev:prev + sup, :] = vc_ref[...]

        def body(idx, carry, q_ref=q_ref, rate=rate, g=g):
            c = idx % rate
            qb = idx // rate
            start = c + rate * qb * wr
            if rate == 1:
                q = q_ref[pl.ds(start, wr), :]
                k = kf[pl.ds(start, 2 * wr), :]
                v = vf[pl.ds(start, 2 * wr), :]
            else:
                q = q_ref[pl.ds(start, wr, stride=rate), :]
                k = kf[pl.ds(start, 2 * wr, stride=rate), :]
                v = vf[pl.ds(start, 2 * wr, stride=rate), :]
            s = _dot_nt(q, k) * scale
            ok = band & ((kk >= wr) | (qb > 0) | (i > 0))
            s = jnp.where(ok, s, NEG)
            m = jnp.max(s, axis=-1, keepdims=True)
            p = jnp.exp(s - m)
            l = jnp.sum(p, axis=-1, keepdims=True)
            acc = _dot(p, v)
            rows = pl.ds(start, wr) if rate == 1 else pl.ds(start, wr, stride=rate)
            if g == 0:
                m_scr[rows, :] = jnp.broadcast_to(m, (wr, HEAD_DIM))
                l_scr[rows, :] = jnp.broadcast_to(l, (wr, HEAD_DIM))
                acc_scr[rows, :] = acc
            else:
                m_old = m_scr[rows, :]
                m_new = jnp.maximum(m_old, m)
                a = jnp.exp(m_old - m_new)
                b = jnp.exp(m - m_new)
                l_scr[rows, :] = a * l_scr[rows, :] + b * l
                acc_scr[rows, :] = a * acc_scr[rows, :] + b * acc
                m_scr[rows, :] = m_new
            return carry

        lax.fori_loop(0, sup // wr, body, 0)
    o_ref[...] = acc_scr[...] / l_scr[...]


def dilated_prompt(st, qkv):
    ng = len(DIL_RATES)
    wr = DIL_WINDOWS[0] // DIL_RATES[0]
    assert all(w // r == wr for w, r in zip(DIL_WINDOWS, DIL_RATES))
    sup = wr * max(DIL_RATES)
    t = st.t
    assert t % sup == 0
    nsup = t // sup
    kind_cols = ng * A_HEADS
    in_specs, args = [], []
    for g, rate in enumerate(DIL_RATES):
        prev = wr * rate
        ratio = sup // prev
        qcol = lambda h, g=g: g * A_HEADS + h
        cur = lambda kind, g=g: pl.BlockSpec(
            (sup, HEAD_DIM), lambda b, i, h: (b * nsup + i, kind * kind_cols + g * A_HEADS + h))
        prv = lambda kind, g=g, ratio=ratio, prev=prev: pl.BlockSpec(
            (prev, HEAD_DIM),
            lambda b, i, h: (jnp.maximum((b * nsup + i) * ratio - 1, 0), kind * kind_cols + g * A_HEADS + h))
        in_specs += [cur(0), prv(1), cur(1), prv(2), cur(2)]
        args += [qkv] * 5
    max_prev = wr * max(DIL_RATES)
    return pl.pallas_call(
        functools.partial(_dil_prompt_kernel, wr=wr, rates=DIL_RATES, sup=sup),
        out_shape=jax.ShapeDtypeStruct((st.r, A_HEADS * HEAD_DIM), F32),
        grid=(st.b, nsup, A_HEADS),
        in_specs=in_specs,
        out_specs=pl.BlockSpec((sup, HEAD_DIM), lambda b, i, h: (b * nsup + i, h)),
        scratch_shapes=[pltpu.VMEM((max_prev + sup, HEAD_DIM), F32), pltpu.VMEM((max_prev + sup, HEAD_DIM), F32),
                        pltpu.VMEM((sup, HEAD_DIM), F32), pltpu.VMEM((sup, HEAD_DIM), F32),
                        pltpu.VMEM((sup, HEAD_DIM), F32)],
        compiler_params=_params(("arbitrary", "arbitrary", "arbitrary")),
    )(*args)


def _dil_sample_kernel(*refs, wr, rates, windows):
    ng = len(rates)
    o_ref = refs[5 * ng]
    scale = HEAD_DIM ** -0.5
    m_run = l_run = acc_run = None
    for g, (rate, win) in enumerate(zip(rates, windows)):
        q_ref, kn_ref, vn_ref, ks_ref, vs_ref = refs[5 * g:5 * g + 5]
        q = q_ref[...]
        tq = q.shape[0]
        lb = ks_ref.shape[0]
        s1 = _dot_nt(q, ks_ref[...]) * scale
        d1 = lb + lax.broadcasted_iota(I32, (tq, lb), 0) - lax.broadcasted_iota(I32, (tq, lb), 1)
        ok1 = (d1 % rate == 0) & (d1 <= win)
        s1 = jnp.where(ok1, s1, NEG)
        s2 = _dot_nt(q, kn_ref[...]) * scale
        d2 = lax.broadcasted_iota(I32, (tq, tq), 0) - lax.broadcasted_iota(I32, (tq, tq), 1)
        ok2 = (d2 >= 0) & (d2 % rate == 0) & (d2 <= win)
        s2 = jnp.where(ok2, s2, NEG)
        m = jnp.maximum(jnp.max(s1, axis=-1, keepdims=True), jnp.max(s2, axis=-1, keepdims=True))
        p1 = jnp.exp(s1 - m)
        p2 = jnp.exp(s2 - m)
        l = jnp.sum(p1, axis=-1, keepdims=True) + jnp.sum(p2, axis=-1, keepdims=True)
        acc = _dot(p1, vs_ref[...]) + _dot(p2, vn_ref[...])
        if g == 0:
            m_run, l_run, acc_run = m, l, acc
        else:
            m_new = jnp.maximum(m_run, m)
            a = jnp.exp(m_run - m_new)
            b = jnp.exp(m - m_new)
            l_run = a * l_run + b * l
            acc_run = a * acc_run + b * acc
            m_run = m_new
    o_ref[...] = acc_run / l_run


def dilated_sample(st, qkv, bufs, layer):
    ng = len(DIL_RATES)
    wr = DIL_WINDOWS[0] // DIL_RATES[0]
    t = st.t
    kind_cols = ng * A_HEADS
    in_specs, args = [], []
    for g in range(ng):
        buf = bufs[g]
        lb = buf.shape[2]
        assert lb == DIL_WINDOWS[g]
        buf2 = buf.reshape(buf.shape[0], buf.shape[1], lb, 2 * A_HEADS * HEAD_DIM)
        new = lambda kind, g=g: pl.BlockSpec((t, HEAD_DIM), lambda b, h: (b, kind * kind_cols + g * A_HEADS + h))
        old = lambda kv: pl.BlockSpec((None, None, lb, HEAD_DIM), lambda b, h: (layer, b, 0, kv * A_HEADS + h))
        in_specs += [new(0), new(1), new(2), old(0), old(1)]
        args += [qkv, qkv, qkv, buf2, buf2]
    return pl.pallas_call(
        functools.partial(_dil_sample_kernel, wr=wr, rates=DIL_RATES, windows=DIL_WINDOWS),
        out_shape=jax.ShapeDtypeStruct((st.r, A_HEADS * HEAD_DIM), F32),
        grid=(st.b, A_HEADS),
        in_specs=in_specs,
        out_specs=pl.BlockSpec((t, HEAD_DIM), lambda b, h: (b, h)),
        compiler_params=_params(("arbitrary", "arbitrary")),
    )(*args)


def qkv_a(st, x, g, mods, w_bf, qn, kn):
    n = w_bf.shape[1]
    per_kind = n // 3
    tn = _pick_tile(per_kind, 512)
    kind = np.arange(n // tn) // (per_kind // tn)
    return mod_linear(st, x, g, mods, 0, 1, w_bf, tn=tn, flags=(kind < 2).astype(np.int32),
                      gidx=np.minimum(kind, 1).astype(np.int32), gains=jnp.stack([qn, kn]))


META_E, META_W, META_RANK = 0, TOP_K, 2 * TOP_K


def _route_kernel(x_ref, g_ref, sh_ref, sc_ref, rw_ref, rb_ref, cin_ref, h_ref, meta_ref, cnt_ref, carry):
    i = pl.program_id(0)

    @pl.when(i == 0)
    def _():
        carry[...] = cin_ref[...]

    h = _modulate(x_ref[...], g_ref[...], sh_ref[...], sc_ref[...])
    h_ref[...] = h
    tm = h.shape[0]
    logits = _dot(h, rw_ref[...]) + rb_ref[...]
    lane = lax.broadcasted_iota(I32, (tm, LANES), 1)
    work = logits
    sel = jnp.zeros((tm, LANES), F32)
    idxs, vals = [], []
    for _ in range(TOP_K):
        v = jnp.max(work, axis=-1, keepdims=True)
        idx = jnp.min(jnp.where(work == v, lane, LANES), axis=-1, keepdims=True)
        hit = lane == idx
        sel = jnp.where(hit, 1.0, sel)
        work = jnp.where(hit, -jnp.inf, work)
        idxs.append(idx)
        vals.append(v)
    es = [jnp.exp(v - vals[0]) for v in vals]
    den = es[0] + es[1] + es[2] + es[3]
    rr = lax.broadcasted_iota(I32, (tm, tm), 0)
    cc = lax.broadcasted_iota(I32, (tm, tm), 1)
    tri = jnp.where(cc < rr, 1.0, 0.0)
    rank = carry[...] + _dot(tri, sel)
    meta = jnp.zeros((tm, LANES), F32)
    for k in range(TOP_K):
        rk = jnp.sum(jnp.where(lane == idxs[k], rank, 0.0), axis=-1, keepdims=True)
        meta = jnp.where(lane == META_E + k, idxs[k].astype(F32), meta)
        meta = jnp.where(lane == META_W + k, es[k] / den, meta)
        meta = jnp.where(lane == META_RANK + k, rk, meta)
    meta_ref[...] = meta
    carry[...] = carry[...] + jnp.sum(sel, axis=0, keepdims=True)
    cnt_ref[...] = carry[...]


def moe_route(st, x, g, mods, rw_bf, rb, counts_in):
    r, d = x.shape
    tm = st.tm
    return pl.pallas_call(
        _route_kernel,
        out_shape=(jax.ShapeDtypeStruct((r, d), F32), jax.ShapeDtypeStruct((r, LANES), F32),
                   jax.ShapeDtypeStruct((1, LANES), F32)),
        grid=(r // tm,),
        in_specs=[pl.BlockSpec((tm, d), lambda i: (i, 0)),
                  pl.BlockSpec((1, d), lambda i: (0, 0)),
                  st.mod_spec(d, 3), st.mod_spec(d, 4),
                  pl.BlockSpec((d, LANES), lambda i: (0, 0)),
                  pl.BlockSpec((1, LANES), lambda i: (0, 0)),
                  pl.BlockSpec((1, LANES), lambda i: (0, 0))],
        out_specs=(pl.BlockSpec((tm, d), lambda i: (i, 0)),
                   pl.BlockSpec((tm, LANES), lambda i: (i, 0)),
                   pl.BlockSpec((1, LANES), lambda i: (0, 0))),
        scratch_shapes=[pltpu.VMEM((1, LANES), F32)],
        compiler_params=_params(("arbitrary",)),
        name="moe_route",
    )(x, g.reshape(1, d), mods, mods, rw_bf, rb, counts_in)


def _dispatch_kernel(dest_ref, h_ref, xb_in, xb_out, sem, *, tm):
    del xb_in
    i = pl.program_id(0)

    def row(t, carry):
        for k in range(TOP_K):
            dst = dest_ref[(i * tm + t) * TOP_K + k]
            pltpu.make_async_copy(h_ref.at[pl.ds(t, 1)], xb_out.at[pl.ds(dst, 1)], sem).start()
        return carry

    lax.fori_loop(0, tm, row, 0)

    def drain(t, carry):
        for k in range(TOP_K):
            pltpu.make_async_copy(h_ref.at[pl.ds(0, 1)], xb_out.at[pl.ds(0, 1)], sem).wait()
        return carry

    lax.fori_loop(0, tm, drain, 0)


def moe_dispatch(st, h, dest, xb):
    r, d = h.shape
    tm = st.tm
    grid_spec = pltpu.PrefetchScalarGridSpec(
        num_scalar_prefetch=1,
        grid=(r // tm,),
        in_specs=[pl.BlockSpec((tm, d), lambda i, *_: (i, 0)),
                  pl.BlockSpec(memory_space=pl.ANY)],
        out_specs=pl.BlockSpec(memory_space=pl.ANY),
        scratch_shapes=[pltpu.SemaphoreType.DMA(())])
    return pl.pallas_call(
        functools.partial(_dispatch_kernel, tm=tm),
        out_shape=jax.ShapeDtypeStruct(xb.shape, xb.dtype),
        grid_spec=grid_spec,
        input_output_aliases={2: 0},
        compiler_params=_params(("arbitrary",), has_side_effects=True),
        name="moe_dispatch",
    )(dest, h, xb)


def _ffn_kernel(e_ref, row_ref, nsub_ref, xb_hbm, wgu_ref, bgu_ref, wdn_ref, bdn_ref, perm_ref, yb_in, yb_hbm,
                xs, xsb, acc, wg_bf, wd_bf, sem_in, sem_out, *, tb, rb):
    del e_ref, yb_in
    w = pl.program_id(0)
    f = pl.program_id(1)
    nf = pl.num_programs(1)
    nsub = nsub_ref[w]
    row0 = pl.multiple_of(row_ref[w], MOE_BLOCK)

    @pl.when((f == 0) & (nsub > 0))
    def _():
        cp = pltpu.make_async_copy(xb_hbm.at[pl.ds(row0, tb)], xs, sem_in)
        cp.start()
        cp.wait()
        xsb[...] = xs[...].astype(BF16)
        acc[...] = jnp.broadcast_to(bdn_ref[...], acc.shape)

    @pl.when(nsub > 0)
    def _():
        wg_bf[...] = wgu_ref[...].astype(BF16)
        wd_bf[...] = wdn_ref[...].astype(BF16)
        lane = lax.broadcasted_iota(I32, (rb, wg_bf.shape[1]), 1)
        odd = (lane % 2) == 1
        for s in range(tb // rb):
            @pl.when(s * rb < nsub * MOE_BLOCK)
            def _(s=s):
                rows = slice(s * rb, (s + 1) * rb)
                gu = jnp.dot(xsb[rows, :], wg_bf[...], preferred_element_type=F32) + bgu_ref[...]
                gate = jnp.minimum(gu, SWIGLU_LIMIT)
                glu = gate * jax.nn.sigmoid(SWIGLU_ALPHA * gate)
                up = jnp.clip(gu, -SWIGLU_LIMIT, SWIGLU_LIMIT) + 1.0
                prod = jnp.where(odd, up * pltpu.roll(glu, 1, 1), 0.0).astype(BF16)
                act = jnp.dot(prod, perm_ref[...], preferred_element_type=F32).astype(BF16)
                acc[rows, :] += jnp.dot(act, wd_bf[...], preferred_element_type=F32)

    @pl.when((f == nf - 1) & (nsub > 0))
    def _():
        def copy(s):
            return pltpu.make_async_copy(acc.at[pl.ds(s * MOE_BLOCK, MOE_BLOCK)],
                                         yb_hbm.at[pl.ds(row0 + s * MOE_BLOCK, MOE_BLOCK)], sem_out)
        for s in range(tb // MOE_BLOCK):
            @pl.when(s < nsub)
            def _(s=s):
                copy(s).start()
        for s in range(tb // MOE_BLOCK):
            @pl.when(s < nsub)
            def _(s=s):
                copy(s).wait()


def moe_ffn(xb, yb, item_e, item_row, item_nsub, w_gu, b_gu, w_dn, b_dn, *, tb, tf):
    n_exp, d, f2 = w_gu.shape
    dff = f2 // 2
    nf = dff // tf
    rb = _pick_tile(tb, 256, MOE_BLOCK)
    n_items = item_e.shape[0]
    perm = np.zeros((2 * tf, tf), np.float32)
    perm[2 * np.arange(tf) + 1, np.arange(tf)] = 1.0

    def fsel(w, f, ns):
        return jnp.where(ns[w] > 0, f, nf - 1)

    grid_spec = pltpu.PrefetchScalarGridSpec(
        num_scalar_prefetch=3,
        grid=(n_items, nf),
        in_specs=[pl.BlockSpec(memory_space=pl.ANY),
                  pl.BlockSpec((None, d, 2 * tf), lambda w, f, e, r_, ns: (e[w], 0, fsel(w, f, ns))),
                  pl.BlockSpec((None, 1, 2 * tf), lambda w, f, e, r_, ns: (e[w], 0, fsel(w, f, ns))),
                  pl.BlockSpec((None, tf, d), lambda w, f, e, r_, ns: (e[w], fsel(w, f, ns), 0)),
                  pl.BlockSpec((None, 1, d), lambda w, f, e, *_: (e[w], 0, 0)),
                  pl.BlockSpec((2 * tf, tf), lambda w, f, *_: (0, 0)),
                  pl.BlockSpec(memory_space=pl.ANY)],
        out_specs=pl.BlockSpec(memory_space=pl.ANY),
        scratch_shapes=[pltpu.VMEM((tb, d), F32), pltpu.VMEM((tb, d), BF16), pltpu.VMEM((tb, d), F32),
                        pltpu.VMEM((d, 2 * tf), BF16), pltpu.VMEM((tf, d), BF16),
                        pltpu.SemaphoreType.DMA(()), pltpu.SemaphoreType.DMA(())])
    return pl.pallas_call(
        functools.partial(_ffn_kernel, tb=tb, rb=rb),
        out_shape=jax.ShapeDtypeStruct(yb.shape, yb.dtype),
        grid_spec=grid_spec,
        input_output_aliases={9: 0},
        compiler_params=_params(("arbitrary", "arbitrary"), has_side_effects=True),
        name="moe_ffn",
    )(item_e, item_row, item_nsub, xb, w_gu, b_gu.reshape(n_exp, 1, f2), w_dn, b_dn.reshape(n_exp, 1, d),
      jnp.asarray(perm, BF16), yb)


def _combine_kernel(dest_ref, yb_hbm, meta_ref, x_ref, gate_ref, o_ref, ybuf, sem, *, tm):
    i = pl.program_id(0)

    def row(t, carry):
        for k in range(TOP_K):
            src = dest_ref[(i * tm + t) * TOP_K + k]
            pltpu.make_async_copy(yb_hbm.at[pl.ds(src, 1)], ybuf.at[k, pl.ds(t, 1)], sem).start()
        return carry

    lax.fori_loop(0, tm, row, 0)

    def drain(t, carry):
        for k in range(TOP_K):
            pltpu.make_async_copy(yb_hbm.at[pl.ds(0, 1)], ybuf.at[0, pl.ds(0, 1)], sem).wait()
        return carry

    lax.fori_loop(0, tm, drain, 0)
    meta = meta_ref[...]
    y = jnp.zeros(x_ref.shape, F32)
    for k in range(TOP_K):
        y = y + meta[:, META_W + k:META_W + k + 1] * ybuf[k]
    o_ref[...] = x_ref[...] + gate_ref[...] * y


def moe_combine(st, yb, dest, meta, x, mods):
    r, d = x.shape
    tm = min(st.tm, 256)
    tps = st.tiles_per_seq * (st.tm // tm)
    rows = st.r if st.per_row_mods else 1
    grid_spec = pltpu.PrefetchScalarGridSpec(
        num_scalar_prefetch=1,
        grid=(r // tm,),
        in_specs=[pl.BlockSpec(memory_space=pl.ANY),
                  pl.BlockSpec((tm, LANES), lambda i, *_: (i, 0)),
                  pl.BlockSpec((tm, d), lambda i, *_: (i, 0)),
                  pl.BlockSpec((None, rows, d), lambda i, *_: (i // tps, 0, 5))],
        out_specs=pl.BlockSpec((tm, d), lambda i, *_: (i, 0)),
        scratch_shapes=[pltpu.VMEM((TOP_K, tm, d), F32), pltpu.SemaphoreType.DMA(())])
    return pl.pallas_call(
        functools.partial(_combine_kernel, tm=tm),
        out_shape=jax.ShapeDtypeStruct((r, d), F32),
        grid_spec=grid_spec,
        compiler_params=_params(("arbitrary",)),
        name="moe_combine",
    )(dest, yb, meta, x, mods)


def moe_layer(streams, xs, g, mods_list, router_w, router_b, w_gu, b_gu, w_dn, b_dn):
    d = xs[0].shape[1]
    n_exp = router_w.shape[1]
    rw = jnp.zeros((d, LANES), BF16).at[:, :n_exp].set(router_w.astype(BF16))
    rb = jnp.full((1, LANES), NEG, F32).at[0, :n_exp].set(router_b)
    counts = jnp.zeros((1, LANES), F32)
    hs, metas = [], []
    for st, x, mods in zip(streams, xs, mods_list):
        h, meta, counts = moe_route(st, x, g, mods, rw, rb, counts)
        hs.append(h)
        metas.append(meta)
    n_asg = sum(st.r for st in streams) * TOP_K
    tb = MOE_TB
    cnt = counts[0, :n_exp].astype(I32)
    padded = (cnt + MOE_BLOCK - 1) // MOE_BLOCK * MOE_BLOCK
    pend = jnp.cumsum(padded)
    pstart = pend - padded
    n_items = (n_asg + n_exp * (MOE_BLOCK - 1)) // tb + n_exp + 1
    per_e = (padded + tb - 1) // tb
    iend = jnp.cumsum(per_e)
    total = iend[-1]
    wi = jnp.arange(n_items, dtype=I32)
    live = wi < total
    e_of = jnp.minimum(jnp.searchsorted(iend, wi, side='right'), n_exp - 1).astype(I32)
    last_e = jnp.minimum(jnp.searchsorted(iend, total - 1, side='right'), n_exp - 1).astype(I32)
    k_of = wi - (iend - per_e)[e_of]
    item_e = jnp.where(live, e_of, last_e).astype(I32)
    item_row = jnp.where(live, pstart[e_of] + k_of * tb, 0).astype(I32)
    item_nsub = jnp.where(live, jnp.minimum(tb, padded[e_of] - k_of * tb) // MOE_BLOCK, 0).astype(I32)
    n_rows = n_asg + n_exp * (MOE_BLOCK - 1) + tb
    n_rows = -(-n_rows // MOE_BLOCK) * MOE_BLOCK
    xb = jnp.zeros((n_rows, d), F32)
    dests = []
    for st, h, meta in zip(streams, hs, metas):
        e = meta[:, META_E:META_E + TOP_K].astype(I32)
        rank = meta[:, META_RANK:META_RANK + TOP_K].astype(I32)
        dest = (pstart[e] + rank).reshape(-1).astype(I32)
        dests.append(dest)
        xb = moe_dispatch(st, h, dest, xb)
    tf = _pick_tile(w_dn.shape[1], 256)
    yb = moe_ffn(xb, jnp.zeros((n_rows, d), F32), item_e, item_row, item_nsub, w_gu, b_gu, w_dn, b_dn, tb=tb, tf=tf)
    return [moe_combine(st, yb, dest, meta, x, mods)
            for st, dest, meta, x, mods in zip(streams, dests, metas, xs, mods_list)]


def kv_proj(st, x, g, mods, w_bf, kn_kv):
    n = w_bf.shape[1]
    tn = B_KV_HEADS * HEAD_DIM
    j = np.arange(n // tn)
    return mod_linear(st, x, g, mods, 0, 1, w_bf, tn=tn, flags=(j % 2 == 0).astype(np.int32),
                      gidx=(j // 2).astype(np.int32), gains=kn_kv)


def q_proj_b(st, x, g, mods, wq_bf, qn, w_gate, b_gate):
    d = x.shape[1]
    n = wq_bf.shape[1]
    nj = n // 512
    q = mod_linear(st, x, g, mods, 0, 1, wq_bf, tn=512, flags=np.ones((nj,), np.int32),
                   gidx=np.zeros((nj,), np.int32), gains=qn)
    per_g = B_HEADS_PER_KV * 3
    wg = jnp.zeros((d, B_KV_HEADS, LANES), F32).at[:, :, :per_g].set(w_gate.reshape(d, B_KV_HEADS, per_g))
    bg = jnp.zeros((B_KV_HEADS, LANES), F32).at[:, :per_g].set(b_gate.reshape(B_KV_HEADS, per_g))
    gates = mod_linear(st, x, g, mods, 0, 1, wg.reshape(d, -1).astype(BF16), tn=B_KV_HEADS * LANES,
                       bias=bg.reshape(-1), act="sigmoid")
    return q, gates


def _cmp1_kernel(*refs, n_pages, cpp):
    page_refs = refs[:n_pages]
    pe_ref, w1_ref, p_ref, q_ref, rows_scr = refs[n_pages:]
    ch = n_pages * cpp
    rpr = cpp * CMP_STRIDE
    half_len = CMP_LEN // 2
    for j, pr in enumerate(page_refs):
        for c in range(2 * B_KV_HEADS):
            rows_scr[c, j * rpr:(j + 1) * rpr, :] = pr[:, c * HEAD_DIM:(c + 1) * HEAD_DIM]
    for kind in range(2):
        for half, out in ((0, p_ref), (1, q_ref)):
            acc = jnp.zeros((B_KV_HEADS * ch, w1_ref.shape[-1]), F32)
            for l in range(half_len):
                pieces = [rows_scr[kind * B_KV_HEADS + g, pl.ds(l, ch, stride=CMP_STRIDE), :]
                          for g in range(B_KV_HEADS)]
                xl = jnp.concatenate(pieces, axis=0) + pe_ref[kind, pl.ds(half * half_len + l, 1), :]
                acc = acc + _dot(xl, w1_ref[kind, half * half_len + l])
            for g in range(B_KV_HEADS):
                out[kind, g] = acc[g * ch:(g + 1) * ch]


def compress_partials(rows_arr, spec_fn, n_refs, rows_per_ref, grid, pe, w1_bf, n_chunks, prefetch=None):
    assert CMP_LEN == 2 * CMP_STRIDE
    cpp = rows_per_ref // CMP_STRIDE
    ch = n_refs * cpp
    hid = w1_bf.shape[-1]
    nb = grid[0]
    out_shape = jax.ShapeDtypeStruct((nb, 2, B_KV_HEADS, n_chunks, hid), F32)
    out_spec = pl.BlockSpec((None, 2, B_KV_HEADS, ch, hid), lambda b, i, *_: (b, 0, 0, i, 0))
    in_specs = [spec_fn(j) for j in range(n_refs)] + [
        pl.BlockSpec(pe.shape, lambda b, i, *_: (0, 0, 0)),
        pl.BlockSpec(w1_bf.shape, lambda b, i, *_: (0, 0, 0, 0))]
    kern = functools.partial(_cmp1_kernel, n_pages=n_refs, cpp=cpp)
    args = [rows_arr] * n_refs + [pe, w1_bf]
    scratch = [pltpu.VMEM((2 * B_KV_HEADS, ch * CMP_STRIDE, HEAD_DIM), F32)]
    if prefetch is None:
        return pl.pallas_call(kern, out_shape=(out_shape, out_shape), grid=grid, in_specs=in_specs,
                              out_specs=(out_spec, out_spec), scratch_shapes=scratch,
                              compiler_params=_params(("arbitrary", "arbitrary")), name="cmp_partials")(*args)
    gs = pltpu.PrefetchScalarGridSpec(num_scalar_prefetch=1, grid=grid, in_specs=in_specs,
                                      out_specs=(out_spec, out_spec), scratch_shapes=scratch)
    return pl.pallas_call(lambda pt, *r: kern(*r), out_shape=(out_shape, out_shape), grid_spec=gs,
                          compiler_params=_params(("arbitrary", "arbitrary")), name="cmp_partials_paged")(prefetch, *args)


def _cmp2_kernel(p_ref, q_ref, b1_ref, w2_ref, b2_ref, o_ref):
    q = q_ref[...]
    qs = jnp.concatenate([q[1:], jnp.zeros((1, q.shape[1]), F32)], axis=0)
    hid = jax.nn.gelu(p_ref[...] + qs + b1_ref[...])
    o_ref[...] = _dot(hid, w2_ref[...]) + b2_ref[...]


def compress_finish(p, q, b1, w2_bf, b2):
    nb, _, ng, nc, hid = p.shape
    blk = pl.BlockSpec((None, None, None, nc, hid), lambda b, k, g: (b, k, g, 0, 0))
    return pl.pallas_call(
        _cmp2_kernel,
        out_shape=jax.ShapeDtypeStruct((nb, 2, ng, nc, HEAD_DIM), F32),
        grid=(nb, 2, ng),
        in_specs=[blk, blk,
                  pl.BlockSpec((None, 1, hid), lambda b, k, g: (k, 0, 0)),
                  pl.BlockSpec((None, hid, HEAD_DIM), lambda b, k, g: (k, 0, 0)),
                  pl.BlockSpec((None, 1, HEAD_DIM), lambda b, k, g: (k, 0, 0))],
        out_specs=pl.BlockSpec((None, None, None, nc, HEAD_DIM), lambda b, k, g: (b, k, g, 0, 0)),
        compiler_params=_params(("arbitrary", "arbitrary", "arbitrary")),
        name="cmp_finish",
    )(p, q, b1.reshape(2, 1, hid), w2_bf, b2.reshape(2, 1, HEAD_DIM))


def _selection_map(n_cmp_pad, n_cmp, n_slc, n_slc_pad):
    r_s = SLC_LEN // CMP_STRIDE
    r_c = CMP_LEN // CMP_STRIDE
    mult = np.zeros(r_s + r_c - 1, np.float32)
    for m in range(r_s):
        for n in range(r_c):
            mult[m + n] += 1.0
    off = r_s * np.arange(n_slc)[None, :] - np.arange(n_cmp)[:, None]
    ok = (off >= 0) & (off < mult.shape[0])
    out = np.zeros((n_cmp_pad, n_slc_pad), np.float32)
    out[:n_cmp, :n_slc] = np.where(ok, mult[np.clip(off, 0, mult.shape[0] - 1)], 0.0)
    return out


def _stack_heads(q):
    return jnp.concatenate([q[:, h * HEAD_DIM:(h + 1) * HEAD_DIM] for h in range(q.shape[1] // HEAD_DIM)], axis=0)


def _rep_rows(x, n):
    return jnp.concatenate([x] * n, axis=0)


def _cmp_and_select(qs, pos, kc, vc, selmap, *, n_cmp, n_slc, n_sel):
    tq = pos.shape[0]
    hpk = qs.shape[0] // tq
    pos_h = _rep_rows(pos, hpk)
    ncp = kc.shape[0]
    s = _dot_nt(qs, kc) * (HEAD_DIM ** -0.5)
    ci = lax.broadcasted_iota(I32, (hpk * tq, ncp), 1)
    vis = (ci * CMP_STRIDE + (CMP_LEN - 1) <= pos_h) & (ci < n_cmp)
    s = jnp.where(vis, s, NEG)
    e = jnp.where(vis, jnp.exp(s - jnp.max(s, axis=-1, keepdims=True)), 0.0)
    p = e / jnp.maximum(jnp.sum(e, axis=-1, keepdims=True), 1e-30)
    o_cmp = _dot(p, vc)
    pi = _dot(p, selmap)
    imp = pi[0:tq]
    for h in range(1, hpk):
        imp = imp + pi[h * tq:(h + 1) * tq]
    ns = imp.shape[1]
    blk = lax.broadcasted_iota(I32, (tq, ns), 1)
    cur = pos // SLC_LEN
    forced = (blk == 0) | (blk == cur) | (blk == cur - 1)
    valid = blk * SLC_LEN <= pos
    score = jnp.where(valid, imp + jnp.where(forced, FORCE_BONUS, 0.0), -FORCE_BONUS)
    score = jnp.where(blk < n_slc, score, -jnp.inf)

    def pick(_, carry):
        sel, work = carry
        v = jnp.max(work, axis=-1, keepdims=True)
        idx = jnp.min(jnp.where(work == v, blk, ns), axis=-1, keepdims=True)
        hit = blk == idx
        return jnp.where(hit, 1.0, sel), jnp.where(hit, -jnp.inf, work)

    sel, _ = lax.fori_loop(0, n_sel, pick, (jnp.zeros((tq, ns), F32), score))
    return o_cmp, sel


def _flash_tiles(qs, pos_h, k_ref, v_ref, lo, hi, tk, ok_fn):
    rows = qs.shape[0]

    def body(kt, carry):
        m, l, acc = carry
        start = pl.multiple_of(kt * tk, tk)
        k = k_ref[pl.ds(start, tk), :]
        v = v_ref[pl.ds(start, tk), :]
        s = _dot_nt(qs, k) * (HEAD_DIM ** -0.5)
        kpos = kt * tk + lax.broadcasted_iota(I32, (rows, tk), 1)
        ok = ok_fn(kt, kpos)
        s = jnp.where(ok, s, NEG)
        m_new = jnp.maximum(m, jnp.max(s, axis=-1, keepdims=True))
        alpha = jnp.exp(m - m_new)
        p = jnp.where(ok, jnp.exp(s - m_new), 0.0)
        return m_new, alpha * l + jnp.sum(p, axis=-1, keepdims=True), alpha * acc + _dot(p, v)

    init = (jnp.full((rows, 1), NEG, F32), jnp.zeros((rows, 1), F32), jnp.zeros((rows, HEAD_DIM), F32))
    m, l, acc = lax.fori_loop(lo, hi, body, init)
    return acc / jnp.maximum(l, 1e-30)


def _gate_cols(gt, branch, hpk):
    return jnp.concatenate([gt[:, h * 3 + branch:h * 3 + branch + 1] for h in range(hpk)], axis=0)


def _nsa_prompt_kernel(q_ref, gate_ref, kc_ref, vc_ref, ks_ref, vs_ref, kw_ref, vw_ref, selmap_ref, expand_ref,
                       o_ref, *, tq, tk, n_cmp, n_slc, n_sel):
    qi = pl.program_id(2)
    hpk = B_HEADS_PER_KV
    qs = _stack_heads(q_ref[...]).astype(BF16)
    pos = qi * tq + lax.broadcasted_iota(I32, (tq, 1), 0)
    pos_h = _rep_rows(pos, hpk)
    o_cmp, sel = _cmp_and_select(qs, pos, kc_ref[...], vc_ref[...], selmap_ref[...],
                                 n_cmp=n_cmp, n_slc=n_slc, n_sel=n_sel)
    sel_bf = sel.astype(BF16)
    hi = (qi * tq + tq - 1) // tk + 1

    def ok_slc(kt, kpos):
        picked = jnp.dot(sel_bf, expand_ref[kt], preferred_element_type=F32)
        return (kpos <= pos_h) & (_rep_rows(picked, hpk) > 0.5)

    o_slc = _flash_tiles(qs, pos_h, ks_ref, vs_ref, 0, hi, tk, ok_slc)

    def ok_win(kt, kpos):
        rel = pos_h - kpos
        return (rel >= 0) & (rel < B_WINDOW)

    lo = jnp.maximum(qi * tq - (B_WINDOW - 1), 0) // tk
    o_win = _flash_tiles(qs, pos_h, kw_ref, vw_ref, lo, hi, tk, ok_win)
    gt = gate_ref[...]
    o = _gate_cols(gt, 0, hpk) * o_cmp + _gate_cols(gt, 1, hpk) * o_slc + _gate_cols(gt, 2, hpk) * o_win
    for h in range(hpk):
        o_ref[:, h * HEAD_DIM:(h + 1) * HEAD_DIM] = o[h * tq:(h + 1) * tq]


def nsa_prompt(st, q, gates, kvs, cmp_kv):
    t = st.t
    tq = _pick_tile(t, 128, 8)
    tk = _pick_tile(t, 256, 8)
    nq = t // tq
    ncp = cmp_kv.shape[3]
    n_cmp = t // CMP_STRIDE - CMP_LEN // CMP_STRIDE + 1
    n_slc = -(-t // SLC_LEN)
    ns = -(-n_slc // LANES) * LANES
    selmap = jnp.asarray(_selection_map(ncp, n_cmp, n_slc, ns), BF16)
    expand = np.zeros((t // tk, ns, tk), np.float32)
    kk = np.arange(t)
    expand[kk // tk, kk // SLC_LEN, kk % tk] = 1.0
    gw = B_HEADS_PER_KV * HEAD_DIM
    kv_blk = lambda col: pl.BlockSpec((t, HEAD_DIM), lambda b, g, i: (b, col + g))
    cmp_blk = lambda kind: pl.BlockSpec((None, None, None, ncp, HEAD_DIM), lambda b, g, i: (b, kind, g, 0, 0))
    return pl.pallas_call(
        functools.partial(_nsa_prompt_kernel, tq=tq, tk=tk, n_cmp=n_cmp, n_slc=n_slc, n_sel=min(N_SELECT, n_slc)),
        out_shape=jax.ShapeDtypeStruct((st.r, B_HEADS * HEAD_DIM), F32),
        grid=(st.b, B_KV_HEADS, nq),
        in_specs=[pl.BlockSpec((tq, gw), lambda b, g, i: (b * nq + i, g)),
                  pl.BlockSpec((tq, LANES), lambda b, g, i: (b * nq + i, g)),
                  cmp_blk(0), cmp_blk(1),
                  kv_blk(2 * B_KV_HEADS), kv_blk(3 * B_KV_HEADS), kv_blk(4 * B_KV_HEADS), kv_blk(5 * B_KV_HEADS),
                  pl.BlockSpec(selmap.shape, lambda b, g, i: (0, 0)),
                  pl.BlockSpec(expand.shape, lambda b, g, i: (0, 0, 0))],
        out_specs=pl.BlockSpec((tq, gw), lambda b, g, i: (b * nq + i, g)),
        compiler_params=_params(("arbitrary", "arbitrary", "arbitrary")),
        name="nsa_prompt",
    )(q, gates, cmp_kv, cmp_kv, kvs, kvs, kvs, kvs, selmap, jnp.asarray(expand, BF16))


def _nsa_s1_kernel(q_ref, kc_ref, vc_ref, selmap_ref, ocmp_ref, sel_ref, *, tq, past_len, n_cmp, n_slc, n_sel):
    qs = _stack_heads(q_ref[...]).astype(BF16)
    pos = past_len + lax.broadcasted_iota(I32, (tq, 1), 0)
    o_cmp, sel = _cmp_and_select(qs, pos, kc_ref[...], vc_ref[...], selmap_ref[...],
                                 n_cmp=n_cmp, n_slc=n_slc, n_sel=n_sel)
    ocmp_ref[...] = o_cmp
    sel_ref[...] = sel


def _nsa_s2_kernel(pt_ref, q_ref, page_ref, sel_ref, new_ref, o_ref, m_scr, l_scr, acc_scr, *, tq, past_len):
    del pt_ref
    p = pl.program_id(1)
    n_pages = pl.num_programs(1)
    hpk = B_HEADS_PER_KV
    gw = B_KV_HEADS * HEAD_DIM
    scale = HEAD_DIM ** -0.5
    ns = sel_ref.shape[-1]
    rows = hpk * tq

    @pl.when(p == 0)
    def _():
        m_scr[...] = jnp.full(m_scr.shape, NEG, F32)
        l_scr[...] = jnp.zeros(l_scr.shape, F32)
        acc_scr[...] = jnp.zeros(acc_scr.shape, F32)

    q = q_ref[...]

    def update(g, s, ok, v):
        s = jnp.where(ok, s, NEG)
        m_old = m_scr[g]
        m_new = jnp.maximum(m_old, jnp.max(s, axis=-1, keepdims=True))
        alpha = jnp.exp(m_old - m_new)
        pr = jnp.where(ok, jnp.exp(s - m_new), 0.0)
        l_scr[g] = alpha * l_scr[g] + jnp.sum(pr, axis=-1, keepdims=True)
        acc_scr[g] = alpha * acc_scr[g] + _dot(pr, v)
        m_scr[g] = m_new

    per_page = PAGE_SIZE // SLC_LEN
    nn = lax.broadcasted_iota(I32, (ns, PAGE_SIZE), 0)
    jj = lax.broadcasted_iota(I32, (ns, PAGE_SIZE), 1)
    expand = jnp.where(nn == p * per_page + jj // SLC_LEN, 1.0, 0.0).astype(BF16)
    for g in range(B_KV_HEADS):
        qs = _stack_heads(q[:, g * hpk * HEAD_DIM:(g + 1) * hpk * HEAD_DIM]).astype(BF16)
        k = page_ref[:, g * HEAD_DIM:(g + 1) * HEAD_DIM]
        v = page_ref[:, gw + g * HEAD_DIM:gw + (g + 1) * HEAD_DIM]
        picked = jnp.dot(sel_ref[g].astype(BF16), expand, preferred_element_type=F32)
        update(g, _dot_nt(qs, k) * scale, _rep_rows(picked, hpk) > 0.5, v)

    @pl.when(p == n_pages - 1)
    def _():
        nn2 = lax.broadcasted_iota(I32, (ns, tq), 0)
        tt2 = lax.broadcasted_iota(I32, (ns, tq), 1)
        expand2 = jnp.where(nn2 == (past_len + tt2) // SLC_LEN, 1.0, 0.0).astype(BF16)
        t_q = _rep_rows(lax.broadcasted_iota(I32, (tq, tq), 0), hpk)
        t_k = _rep_rows(lax.broadcasted_iota(I32, (tq, tq), 1), hpk)
        for g in range(B_KV_HEADS):
            qs = _stack_heads(q[:, g * hpk * HEAD_DIM:(g + 1) * hpk * HEAD_DIM]).astype(BF16)
            k = new_ref[:, g * HEAD_DIM:(g + 1) * HEAD_DIM]
            v = new_ref[:, gw + g * HEAD_DIM:gw + (g + 1) * HEAD_DIM]
            picked = jnp.dot(sel_ref[g].astype(BF16), expand2, preferred_element_type=F32)
            update(g, _dot_nt(qs, k) * scale, (t_k <= t_q) & (_rep_rows(picked, hpk) > 0.5), v)
            o_ref[g] = acc_scr[g] / jnp.maximum(l_scr[g], 1e-30)


def _nsa_s3_kernel(q_ref, gate_ref, win_ref, new_ref, ocmp_ref, oslc_ref, o_ref, *, tq):
    hpk = B_HEADS_PER_KV
    gw = B_KV_HEADS * HEAD_DIM
    scale = HEAD_DIM ** -0.5
    lb = win_ref.shape[0]
    rows = hpk * tq
    q = q_ref[...]
    gt_all = gate_ref[...]
    t_q1 = _rep_rows(lax.broadcasted_iota(I32, (tq, lb), 0), hpk)
    i_k1 = _rep_rows(lax.broadcasted_iota(I32, (tq, lb), 1), hpk)
    ok1 = (t_q1 + lb - i_k1) < B_WINDOW
    t_q2 = _rep_rows(lax.broadcasted_iota(I32, (tq, tq), 0), hpk)
    t_k2 = _rep_rows(lax.broadcasted_iota(I32, (tq, tq), 1), hpk)
    ok2 = t_k2 <= t_q2
    for g in range(B_KV_HEADS):
        qs = _stack_heads(q[:, g * hpk * HEAD_DIM:(g + 1) * hpk * HEAD_DIM]).astype(BF16)
        s1 = jnp.where(ok1, _dot_nt(qs, win_ref[:, g * HEAD_DIM:(g + 1) * HEAD_DIM]) * scale, NEG)
        s2 = jnp.where(ok2, _dot_nt(qs, new_ref[:, g * HEAD_DIM:(g + 1) * HEAD_DIM]) * scale, NEG)
        m = jnp.maximum(jnp.max(s1, axis=-1, keepdims=True), jnp.max(s2, axis=-1, keepdims=True))
        p1 = jnp.where(ok1, jnp.exp(s1 - m), 0.0)
        p2 = jnp.where(ok2, jnp.exp(s2 - m), 0.0)
        den = jnp.sum(p1, axis=-1, keepdims=True) + jnp.sum(p2, axis=-1, keepdims=True)
        o_win = (_dot(p1, win_ref[:, gw + g * HEAD_DIM:gw + (g + 1) * HEAD_DIM])
                 + _dot(p2, new_ref[:, gw + g * HEAD_DIM:gw + (g + 1) * HEAD_DIM])) / jnp.maximum(den, 1e-30)
        gt = gt_all[:, g * LANES:(g + 1) * LANES]
        o = (_gate_cols(gt, 0, hpk) * ocmp_ref[g] + _gate_cols(gt, 1, hpk) * oslc_ref[g]
             + _gate_cols(gt, 2, hpk) * o_win)
        for h in range(hpk):
            col = (g * hpk + h) * HEAD_DIM
            o_ref[:, col:col + HEAD_DIM] = o[h * tq:(h + 1) * tq]


def nsa_sample(st, q, gates, kvs, cmp_kv, cache, page_table, state_win, past_len):
    nb, tq = st.b, st.t
    hpk = B_HEADS_PER_KV
    rows = hpk * tq
    ncp = cmp_kv.shape[3]
    l_all = past_len + tq
    n_cmp = l_all // CMP_STRIDE - CMP_LEN // CMP_STRIDE + 1
    n_slc = -(-l_all // SLC_LEN)
    ns = -(-n_slc // LANES) * LANES
    assert n_cmp <= ncp
    selmap = jnp.asarray(_selection_map(ncp, n_cmp, n_slc, ns), BF16)
    gw = hpk * HEAD_DIM
    cmp_blk = lambda kind: pl.BlockSpec((None, None, None, ncp, HEAD_DIM), lambda b, g: (b, kind, g, 0, 0))
    o_cmp, sel = pl.pallas_call(
        functools.partial(_nsa_s1_kernel, tq=tq, past_len=past_len, n_cmp=n_cmp, n_slc=n_slc,
                          n_sel=min(N_SELECT, n_slc)),
        out_shape=(jax.ShapeDtypeStruct((nb, B_KV_HEADS, rows, HEAD_DIM), F32),
                   jax.ShapeDtypeStruct((nb, B_KV_HEADS, tq, ns), F32)),
        grid=(nb, B_KV_HEADS),
        in_specs=[pl.BlockSpec((tq, gw), lambda b, g: (b, g)), cmp_blk(0), cmp_blk(1),
                  pl.BlockSpec(selmap.shape, lambda b, g: (0, 0))],
        out_specs=(pl.BlockSpec((None, None, rows, HEAD_DIM), lambda b, g: (b, g, 0, 0)),
                   pl.BlockSpec((None, None, tq, ns), lambda b, g: (b, g, 0, 0))),
        compiler_params=_params(("arbitrary", "arbitrary")),
        name="nsa_sample_cmp",
    )(q, cmp_kv, cmp_kv, selmap)

    n_pages = page_table.shape[1]
    page_w = cache.shape[2] * cache.shape[3] * cache.shape[4]
    cache3 = cache.reshape(cache.shape[0], cache.shape[1], page_w)
    half = page_w // 2
    grp = pl.BlockSpec((None, B_KV_HEADS, rows, HEAD_DIM), lambda b, *_: (b, 0, 0, 0))
    o_slc = pl.pallas_call(
        functools.partial(_nsa_s2_kernel, tq=tq, past_len=past_len),
        out_shape=jax.ShapeDtypeStruct((nb, B_KV_HEADS, rows, HEAD_DIM), F32),
        grid_spec=pltpu.PrefetchScalarGridSpec(
            num_scalar_prefetch=1,
            grid=(nb, n_pages),
            in_specs=[pl.BlockSpec((tq, B_HEADS * HEAD_DIM), lambda b, p, pt: (b, 0)),
                      pl.BlockSpec((None, PAGE_SIZE, half), lambda b, p, pt: (pt[b * n_pages + p], 0, 1)),
                      pl.BlockSpec((None, B_KV_HEADS, tq, ns), lambda b, p, pt: (b, 0, 0, 0)),
                      pl.BlockSpec((tq, half), lambda b, p, pt: (b, 1))],
            out_specs=grp,
            scratch_shapes=[pltpu.VMEM((B_KV_HEADS, rows, 1), F32), pltpu.VMEM((B_KV_HEADS, rows, 1), F32),
                            pltpu.VMEM((B_KV_HEADS, rows, HEAD_DIM), F32)]),
        compiler_params=_params(("arbitrary", "arbitrary")),
        name="nsa_sample_slc",
    )(page_table.reshape(-1), q, cache3, sel, kvs)

    lb = state_win.shape[1]
    win2 = state_win.reshape(nb, lb, 2 * B_KV_HEADS * HEAD_DIM)
    return pl.pallas_call(
        functools.partial(_nsa_s3_kernel, tq=tq),
        out_shape=jax.ShapeDtypeStruct((st.r, B_HEADS * HEAD_DIM), F32),
        grid=(nb,),
        in_specs=[pl.BlockSpec((tq, B_HEADS * HEAD_DIM), lambda b: (b, 0)),
                  pl.BlockSpec((tq, B_KV_HEADS * LANES), lambda b: (b, 0)),
                  pl.BlockSpec((None, lb, half), lambda b: (b, 0, 0)),
                  pl.BlockSpec((tq, half), lambda b: (b, 2)),
                  grp, grp],
        out_specs=pl.BlockSpec((tq, B_HEADS * HEAD_DIM), lambda b: (b, 0)),
        compiler_params=_params(("arbitrary",)),
        name="nsa_sample_win",
    )(q, gates, win2, kvs, o_cmp, o_slc)


def kernel(x_prompt, x_sample, state_dil0, state_dil1, state_dil2, cache_nsa, state_win, page_table, c_prompt, c_sample, w_mod, b_mod, g_norm, w_qkv_a, w_o_a, qn_a, kn_a, w_q_b, qn_b, w_gate_b, b_gate_b, w_o_b, w_mod_kv, b_mod_kv, g_kv, w_kv, kn_kv, cmp_pe, cmp_w1, cmp_b1, cmp_w2, cmp_b2, router_w, router_b, w_gu, b_gu, w_down, b_down):
    bp, tp, d = x_prompt.shape
    bs, ts, _ = x_sample.shape
    past_len = page_table.shape[1] * PAGE_SIZE
    stp = Stream(bp, tp, jnp.arange(tp), 512)
    sts = Stream(bs, ts, past_len + jnp.arange(ts), 512)
    streams = [stp, sts]
    n_pages = page_table.shape[1]
    depth = w_mod.shape[0]
    n_a = w_qkv_a.shape[0]
    ng = len(DIL_RATES)
    dil_bufs = (state_dil0, state_dil1, state_dil2)

    c_all = jnp.concatenate([c_prompt, c_sample], axis=0)
    m_all = mod_vectors(c_all, w_mod, b_mod)
    m_kv = mod_vectors(c_all, w_mod_kv[None], b_mod_kv[None])[0]

    def split(m):
        return [stp.mods(m[:bp]), sts.mods(m[bp:])]

    xs = [x_prompt.reshape(-1, d), x_sample.reshape(-1, d)]
    a_qkv = []
    kvs = cmp_ctx = None
    for l in range(depth):
        mods = split(m_all[l])
        g_attn = g_norm[l, 0]
        if l < n_a:
            wq, wo = w_qkv_a[l].astype(BF16), w_o_a[l].astype(BF16)
            qkvs = [qkv_a(st, x, g_attn, md, wq, qn_a[l], kn_a[l]) for st, x, md in zip(streams, xs, mods)]
            a_qkv.append(qkvs)
            outs = [dilated_prompt(stp, qkvs[0]), dilated_sample(sts, qkvs[1], dil_bufs, l)]
        else:
            lb = l - n_a
            wq, wo = w_q_b[lb].astype(BF16), w_o_b[lb].astype(BF16)
            qg = [q_proj_b(st, x, g_attn, md, wq, qn_b[lb], w_gate_b[lb], b_gate_b[lb])
                  for st, x, md in zip(streams, xs, mods)]
            outs = [nsa_prompt(stp, qg[0][0], qg[0][1], kvs[0], cmp_ctx[0]),
                    nsa_sample(sts, qg[1][0], qg[1][1], kvs[1], cmp_ctx[1], cache_nsa, page_table, state_win,
                               past_len)]
        xs = [out_proj_residual(st, o, wo, x, md, 2) for st, o, x, md in zip(streams, outs, xs, mods)]
        xs = moe_layer(streams, xs, g_norm[l, 1], mods, router_w[l], router_b[l], w_gu[l], b_gu[l],
                       w_down[l], b_down[l])
        if l == n_a - 1:
            wkv = w_kv.astype(BF16)
            kvs = [kv_proj(st, x, g_kv, md, wkv, kn_kv) for st, x, md in zip(streams, xs, split(m_kv))]
            w1, w2 = cmp_w1.astype(BF16), cmp_w2.astype(BF16)
            cmp_w = 2 * B_KV_HEADS * HEAD_DIM
            rows_p = _pick_tile(tp, 1024, CMP_STRIDE)
            steps_p = tp // rows_p
            pq = compress_partials(
                kvs[0], lambda j: pl.BlockSpec((rows_p, cmp_w), lambda b, i: (b * steps_p + i, 0)),
                1, rows_p, (bp, steps_p), cmp_pe, w1, tp // CMP_STRIDE)
            cmp_p = compress_finish(pq[0], pq[1], cmp_b1, w2, cmp_b2)
            assert (past_len + ts) // CMP_STRIDE == past_len // CMP_STRIDE
            pps = _pick_tile(n_pages, 8, 1)
            cache3 = cache_nsa.reshape(cache_nsa.shape[0], PAGE_SIZE, -1)
            pq = compress_partials(
                cache3,
                lambda j: pl.BlockSpec((None, PAGE_SIZE, cmp_w),
                                       lambda b, i, pt: (pt[b * n_pages + i * pps + j], 0, 0)),
                pps, PAGE_SIZE, (bs, n_pages // pps), cmp_pe, w1, past_len // CMP_STRIDE,
                prefetch=page_table.reshape(-1))
            cmp_s = compress_finish(pq[0], pq[1], cmp_b1, w2, cmp_b2)
            cmp_ctx = [cmp_p, cmp_s]

    def a_rows(qkv, nb, t, kind, g):
        return qkv.reshape(nb, t, 3, ng, A_HEADS, HEAD_DIM)[:, :, kind, g]

    p_dil, s_dil = [], []
    for g in range(ng):
        keep = min(DIL_WINDOWS[g], tp)
        p_dil.append(jnp.stack([
            jnp.stack([a_rows(a_qkv[l][0], bp, tp, 1, g)[:, tp - keep:],
                       a_rows(a_qkv[l][0], bp, tp, 2, g)[:, tp - keep:]], axis=2) for l in range(n_a)]))
        lb = dil_bufs[g].shape[2]
        keep = min(DIL_WINDOWS[g], lb + ts)
        per_layer = []
        for l in range(n_a):
            ka = jnp.concatenate([dil_bufs[g][l][:, :, 0], a_rows(a_qkv[l][1], bs, ts, 1, g)], axis=1)
            va = jnp.concatenate([dil_bufs[g][l][:, :, 1], a_rows(a_qkv[l][1], bs, ts, 2, g)], axis=1)
            per_layer.append(jnp.stack([ka[:, lb + ts - keep:], va[:, lb + ts - keep:]], axis=2))
        s_dil.append(jnp.stack(per_layer))
    rows_w = 4 * B_KV_HEADS * HEAD_DIM
    p_rows = kvs[0][:, :rows_w].reshape(bp, tp, 4, B_KV_HEADS, HEAD_DIM)
    p_winr = kvs[0][:, rows_w:].reshape(bp, tp, 2, B_KV_HEADS, HEAD_DIM)
    p_win = p_winr[:, tp - min(B_WINDOW, tp):]
    s_rows = kvs[1][:, :rows_w].reshape(bs, ts, 4, B_KV_HEADS, HEAD_DIM)
    lbw = state_win.shape[1]
    win_all = jnp.concatenate([state_win, kvs[1][:, rows_w:].reshape(bs, ts, 2, B_KV_HEADS, HEAD_DIM)], axis=1)
    s_win = win_all[:, lbw + ts - min(B_WINDOW, lbw + ts):]
    return (xs[0].reshape(bp, tp, d), xs[1].reshape(bs, ts, d), p_dil[0], p_dil[1], p_dil[2], p_rows, p_win,
            s_dil[0], s_dil[1], s_dil[2], s_rows, s_win)
```

```python
import functools

import numpy as np
import jax
import jax.numpy as jnp
from jax import lax
from jax.experimental import pallas as pl
from jax.experimental.pallas import tpu as pltpu

F32 = jnp.float32
BF16 = jnp.bfloat16
I32 = jnp.int32

HEAD_DIM = 128
LANES = 128
DIL_WINDOWS = (128, 512, 2048)
DIL_RATES = (1, 4, 16)
A_HEADS = 8
B_HEADS = 16
B_KV_HEADS = 4
B_HEADS_PER_KV = B_HEADS // B_KV_HEADS
CMP_LEN = 32
CMP_STRIDE = 16
SLC_LEN = 64
N_SELECT = 16
B_WINDOW = 512
FORCE_BONUS = 1.0e4
N_EXPERTS = 32
TOP_K = 4
SWIGLU_ALPHA = 1.702
SWIGLU_LIMIT = 7.0
MOE_BLOCK = 128
MOE_TB = 1152
PAGE_SIZE = 128
ROPE_THETA = 10000.0
NORM_EPS = 1e-6
N_MODS = 6
NEG = -1e30
VMEM_LIMIT = 56 * 1024 * 1024

_NT = (((1,), (1,)), ((), ()))


def _params(sem, **kw):
    return pltpu.CompilerParams(dimension_semantics=sem, vmem_limit_bytes=VMEM_LIMIT, **kw)


def _dot(a, b):
    return jnp.dot(a.astype(BF16), b.astype(BF16), preferred_element_type=F32)


def _dot_nt(a, b):
    return lax.dot_general(a.astype(BF16), b.astype(BF16), _NT, preferred_element_type=F32)


def _pick_tile(n, cap, mult=LANES):
    if n <= cap:
        return n
    t = (cap // mult) * mult
    while t > mult and n % t:
        t -= mult
    assert n % t == 0, (n, cap)
    return t


def _rope_tables(pos):
    half = HEAD_DIM // 2
    inv_freq = 1.0 / (ROPE_THETA ** (jnp.arange(half, dtype=F32) * 2.0 / HEAD_DIM))
    ang = pos.astype(F32)[:, None] * inv_freq[None, :]
    c, s = jnp.cos(ang), jnp.sin(ang)
    return jnp.concatenate([c, c], axis=-1), jnp.concatenate([-s, s], axis=-1)


def _modulate(x, g, shift, scale):
    y = x * lax.rsqrt(jnp.mean(x * x, axis=-1, keepdims=True) + NORM_EPS) * g
    return y * (1.0 + scale) + shift


def _head_norm_rope(blk, gain, cos2, sin2):
    y = blk * lax.rsqrt(jnp.mean(blk * blk, axis=-1, keepdims=True) + NORM_EPS) * gain
    return y * cos2 + pltpu.roll(y, HEAD_DIM // 2, 1) * sin2


def _mod_kernel(c_ref, w_ref, b_ref, o_ref):
    c = c_ref[...]
    o_ref[...] = _dot(c * jax.nn.sigmoid(c), w_ref[...]) + b_ref[...]


def mod_vectors(c, w, b):
    n_layers, d, n = w.shape
    nc = c.shape[0]
    tn = _pick_tile(n, max(LANES, (8 << 20) // (4 * d)))
    return pl.pallas_call(
        _mod_kernel,
        out_shape=jax.ShapeDtypeStruct((n_layers, nc, n), F32),
        grid=(n_layers, n // tn),
        in_specs=[pl.BlockSpec((nc, d), lambda l, j: (0, 0)),
                  pl.BlockSpec((None, d, tn), lambda l, j: (l, 0, j)),
                  pl.BlockSpec((None, 1, tn), lambda l, j: (l, 0, j))],
        out_specs=pl.BlockSpec((None, nc, tn), lambda l, j: (l, 0, j)),
        compiler_params=_params(("arbitrary", "arbitrary")),
        name="mod_vectors",
    )(c, w, b.reshape(n_layers, 1, n))


class Stream:
    def __init__(self, b, t, pos, tm_cap):
        self.b, self.t, self.r = b, t, b * t
        self.per_row_mods = t < LANES
        self.tm = self.r if self.per_row_mods else _pick_tile(t, tm_cap, 8)
        self.tiles_per_seq = 1 if self.per_row_mods else t // self.tm
        cos2, sin2 = _rope_tables(pos)
        if self.per_row_mods:
            cos2, sin2 = jnp.tile(cos2, (b, 1)), jnp.tile(sin2, (b, 1))
        self.cos2, self.sin2 = cos2, sin2

    def mods(self, m):
        if self.per_row_mods:
            return jnp.repeat(m, self.t, axis=0)[None]
        return m[:, None, :]

    def mod_spec(self, d, k, n_grid_axes=2):
        rows = self.r if self.per_row_mods else 1
        tps = self.tiles_per_seq
        return pl.BlockSpec((None, rows, d), lambda i, *_: (i // tps, 0, k))

    def rope_spec(self):
        tps = self.tiles_per_seq
        return pl.BlockSpec((self.tm, HEAD_DIM), lambda i, *_: (i % tps, 0))


def _lin_kernel(flag_ref, gidx_ref, x_ref, g_ref, sh_ref, sc_ref, w_ref, gain_ref, cos_ref, sin_ref, b_ref,
                o_ref, h_scr, *, act):
    del gidx_ref
    j = pl.program_id(1)

    @pl.when(j == 0)
    def _():
        h_scr[...] = _modulate(x_ref[...], g_ref[...], sh_ref[...], sc_ref[...]).astype(BF16)

    acc = jnp.dot(h_scr[...], w_ref[...], preferred_element_type=F32) + b_ref[...]
    tn = acc.shape[1]

    @pl.when(flag_ref[j] == 0)
    def _():
        o_ref[...] = (jax.nn.sigmoid(acc) if act == "sigmoid" else acc).astype(o_ref.dtype)

    @pl.when(flag_ref[j] != 0)
    def _():
        for hh in range(tn // HEAD_DIM):
            sl = slice(hh * HEAD_DIM, (hh + 1) * HEAD_DIM)
            o_ref[:, sl] = _head_norm_rope(acc[:, sl], gain_ref[...], cos_ref[...], sin_ref[...]).astype(o_ref.dtype)


def mod_linear(st, x, g, mods, ksh, ksc, w_bf, *, tn, name, flags=None, gidx=None, gains=None, bias=None,
               act=None):
    r, d = x.shape
    n = w_bf.shape[1]
    nj = n // tn
    if flags is None:
        flags = np.zeros((nj,), np.int32)
        gidx = np.zeros((nj,), np.int32)
        gains = jnp.ones((1, HEAD_DIM), F32)
    if bias is None:
        bias = jnp.zeros((n,), F32)
    gains = gains.reshape(-1, 1, HEAD_DIM)
    tm = st.tm
    grid_spec = pltpu.PrefetchScalarGridSpec(
        num_scalar_prefetch=2,
        grid=(r // tm, nj),
        in_specs=[pl.BlockSpec((tm, d), lambda i, j, *_: (i, 0)),
                  pl.BlockSpec((1, d), lambda i, j, *_: (0, 0)),
                  st.mod_spec(d, ksh), st.mod_spec(d, ksc),
                  pl.BlockSpec((d, tn), lambda i, j, *_: (0, j)),
                  pl.BlockSpec((None, 1, HEAD_DIM), lambda i, j, fl, gi: (gi[j], 0, 0)),
                  st.rope_spec(), st.rope_spec(),
                  pl.BlockSpec((1, tn), lambda i, j, *_: (0, j))],
        out_specs=pl.BlockSpec((tm, tn), lambda i, j, *_: (i, j)),
        scratch_shapes=[pltpu.VMEM((tm, d), BF16)])
    return pl.pallas_call(
        functools.partial(_lin_kernel, act=act),
        out_shape=jax.ShapeDtypeStruct((r, n), F32),
        grid_spec=grid_spec,
        compiler_params=_params(("arbitrary", "arbitrary")),
        name=name,
    )(jnp.asarray(flags, I32), jnp.asarray(gidx, I32), x, g.reshape(1, d), mods, mods, w_bf, gains,
      st.cos2, st.sin2, bias.reshape(1, n))


def _out_kernel(o_ref, w_ref, x_ref, gate_ref, y_ref):
    y_ref[...] = x_ref[...] + gate_ref[...] * _dot(o_ref[...], w_ref[...])


def out_proj_residual(st, o, w_bf, x, mods, kgate):
    r, kd = o.shape
    d = x.shape[1]
    tn = _pick_tile(d, 512)
    tm = st.tm
    nj = d // tn
    rows = st.r if st.per_row_mods else 1
    tps = st.tiles_per_seq
    return pl.pallas_call(
        _out_kernel,
        out_shape=jax.ShapeDtypeStruct((r, d), F32),
        grid=(r // tm, nj),
        in_specs=[pl.BlockSpec((tm, kd), lambda i, j: (i, 0)),
                  pl.BlockSpec((kd, tn), lambda i, j: (0, j)),
                  pl.BlockSpec((tm, tn), lambda i, j: (i, j)),
                  pl.BlockSpec((None, rows, tn), lambda i, j: (i // tps, 0, kgate * nj + j))],
        out_specs=pl.BlockSpec((tm, tn), lambda i, j: (i, j)),
        compiler_params=_params(("arbitrary", "arbitrary")),
        name="out_proj",
    )(o, w_bf, x, mods)


def _dil_prompt_kernel(*refs, wr, rates, sup):
    ng = len(rates)
    o_ref = refs[5 * ng]
    kf, vf, m_scr, l_scr, acc_scr = refs[5 * ng + 1:]
    i = pl.program_id(1)
    scale = HEAD_DIM ** -0.5
    qi = lax.broadcasted_iota(I32, (wr, 2 * wr), 0)
    kk = lax.broadcasted_iota(I32, (wr, 2 * wr), 1)
    band = (kk >= qi) & (kk <= qi + wr)
    for g, rate in enumerate(rates):
        q_ref, kp_ref, kc_ref, vp_ref, vc_ref = refs[5 * g:5 * g + 5]
        prev = wr * rate
        kf[0:prev, :] = kp_ref[...]
        kf[prev:prev + sup, :] = kc_ref[...]
        vf[0:prev, :] = vp_ref[...]
        vf[prev:prev + sup, :] = vc_ref[...]

        def body(idx, carry, q_ref=q_ref, rate=rate, g=g):
            c = idx % rate
            qb = idx // rate
            start = c + rate * qb * wr
            if rate == 1:
                q = q_ref[pl.ds(start, wr), :]
                k = kf[pl.ds(start, 2 * wr), :]
                v = vf[pl.ds(start, 2 * wr), :]
            else:
                q = q_ref[pl.ds(start, wr, stride=rate), :]
                k = kf[pl.ds(start, 2 * wr, stride=rate), :]
                v = vf[pl.ds(start, 2 * wr, stride=rate), :]
            s = _dot_nt(q, k) * scale
            ok = band & ((kk >= wr) | (qb > 0) | (i > 0))
            s = jnp.where(ok, s, NEG)
            m = jnp.max(s, axis=-1, keepdims=True)
            p = jnp.exp(s - m)
            l = jnp.sum(p, axis=-1, keepdims=True)
            acc = _dot(p, v)
            rows = pl.ds(start, wr) if rate == 1 else pl.ds(start, wr, stride=rate)
            if g == 0:
                m_scr[rows, :] = jnp.broadcast_to(m, (wr, HEAD_DIM))
                l_scr[rows, :] = jnp.broadcast_to(l, (wr, HEAD_DIM))
                acc_scr[rows, :] = acc
            else:
                m_old = m_scr[rows, :]
                m_new = jnp.maximum(m_old, m)
                a = jnp.exp(m_old - m_new)
                b = jnp.exp(m - m_new)
                l_scr[rows, :] = a * l_scr[rows, :] + b * l
                acc_scr[rows, :] = a * acc_scr[rows, :] + b * acc
                m_scr[rows, :] = m_new
            return carry

        lax.fori_loop(0, sup // wr, body, 0)
    o_ref[...] = acc_scr[...] / l_scr[...]


def dilated_prompt(st, qkv):
    ng = len(DIL_RATES)
    wr = DIL_WINDOWS[0] // DIL_RATES[0]
    assert all(w // r == wr for w, r in zip(DIL_WINDOWS, DIL_RATES))
    sup = wr * max(DIL_RATES)
    t = st.t
    assert t % sup == 0
    nsup = t // sup
    kind_cols = ng * A_HEADS
    in_specs, args = [], []
    for g, rate in enumerate(DIL_RATES):
        prev = wr * rate
        ratio = sup // prev
        qcol = lambda h, g=g: g * A_HEADS + h
        cur = lambda kind, g=g: pl.BlockSpec(
            (sup, HEAD_DIM), lambda b, i, h: (b * nsup + i, kind * kind_cols + g * A_HEADS + h))
        prv = lambda kind, g=g, ratio=ratio, prev=prev: pl.BlockSpec(
            (prev, HEAD_DIM),
            lambda b, i, h: (jnp.maximum((b * nsup + i) * ratio - 1, 0), kind * kind_cols + g * A_HEADS + h))
        in_specs += [cur(0), prv(1), cur(1), prv(2), cur(2)]
        args += [qkv] * 5
    max_prev = wr * max(DIL_RATES)
    return pl.pallas_call(
        functools.partial(_dil_prompt_kernel, wr=wr, rates=DIL_RATES, sup=sup),
        out_shape=jax.ShapeDtypeStruct((st.r, A_HEADS * HEAD_DIM), F32),
        grid=(st.b, nsup, A_HEADS),
        in_specs=in_specs,
        out_specs=pl.BlockSpec((sup, HEAD_DIM), lambda b, i, h: (b * nsup + i, h)),
        scratch_shapes=[pltpu.VMEM((max_prev + sup, HEAD_DIM), F32), pltpu.VMEM((max_prev + sup, HEAD_DIM), F32),
                        pltpu.VMEM((sup, HEAD_DIM), F32), pltpu.VMEM((sup, HEAD_DIM), F32),
                        pltpu.VMEM((sup, HEAD_DIM), F32)],
        compiler_params=_params(("arbitrary", "arbitrary", "arbitrary")),
        name="dilated_prompt",
    )(*args)


def _dil_sample_kernel(*refs, wr, rates, windows):
    ng = len(rates)
    o_ref = refs[5 * ng]
    scale = HEAD_DIM ** -0.5
    m_run = l_run = acc_run = None
    for g, (rate, win) in enumerate(zip(rates, windows)):
        q_ref, kn_ref, vn_ref, ks_ref, vs_ref = refs[5 * g:5 * g + 5]
        q = q_ref[...]
        tq = q.shape[0]
        lb = ks_ref.shape[0]
        s1 = _dot_nt(q, ks_ref[...]) * scale
        d1 = lb + lax.broadcasted_iota(I32, (tq, lb), 0) - lax.broadcasted_iota(I32, (tq, lb), 1)
        ok1 = (d1 % rate == 0) & (d1 <= win)
        s1 = jnp.where(ok1, s1, NEG)
        s2 = _dot_nt(q, kn_ref[...]) * scale
        d2 = lax.broadcasted_iota(I32, (tq, tq), 0) - lax.broadcasted_iota(I32, (tq, tq), 1)
        ok2 = (d2 >= 0) & (d2 % rate == 0) & (d2 <= win)
        s2 = jnp.where(ok2, s2, NEG)
        m = jnp.maximum(jnp.max(s1, axis=-1, keepdims=True), jnp.max(s2, axis=-1, keepdims=True))
        p1 = jnp.exp(s1 - m)
        p2 = jnp.exp(s2 - m)
        l = jnp.sum(p1, axis=-1, keepdims=True) + jnp.sum(p2, axis=-1, keepdims=True)
        acc = _dot(p1, vs_ref[...]) + _dot(p2, vn_ref[...])
        if g == 0:
            m_run, l_run, acc_run = m, l, acc
        else:
            m_new = jnp.maximum(m_run, m)
            a = jnp.exp(m_run - m_new)
            b = jnp.exp(m - m_new)
            l_run = a * l_run + b * l
            acc_run = a * acc_run + b * acc
            m_run = m_new
    o_ref[...] = acc_run / l_run


def dilated_sample(st, qkv, bufs, layer):
    ng = len(DIL_RATES)
    wr = DIL_WINDOWS[0] // DIL_RATES[0]
    t = st.t
    kind_cols = ng * A_HEADS
    in_specs, args = [], []
    for g in range(ng):
        buf = bufs[g]
        lb = buf.shape[2]
        assert lb == DIL_WINDOWS[g]
        buf2 = buf.reshape(buf.shape[0], buf.shape[1], lb, 2 * A_HEADS * HEAD_DIM)
        new = lambda kind, g=g: pl.BlockSpec((t, HEAD_DIM), lambda b, h: (b, kind * kind_cols + g * A_HEADS + h))
        old = lambda kv: pl.BlockSpec((None, None, lb, HEAD_DIM), lambda b, h: (layer, b, 0, kv * A_HEADS + h))
        in_specs += [new(0), new(1), new(2), old(0), old(1)]
        args += [qkv, qkv, qkv, buf2, buf2]
    return pl.pallas_call(
        functools.partial(_dil_sample_kernel, wr=wr, rates=DIL_RATES, windows=DIL_WINDOWS),
        out_shape=jax.ShapeDtypeStruct((st.r, A_HEADS * HEAD_DIM), F32),
        grid=(st.b, A_HEADS),
        in_specs=in_specs,
        out_specs=pl.BlockSpec((t, HEAD_DIM), lambda b, h: (b, h)),
        compiler_params=_params(("arbitrary", "arbitrary")),
        name="dilated_sample",
    )(*args)


def qkv_a(st, x, g, mods, w_bf, qn, kn):
    n = w_bf.shape[1]
    per_kind = n // 3
    tn = _pick_tile(per_kind, 512)
    kind = np.arange(n // tn) // (per_kind // tn)
    return mod_linear(st, x, g, mods, 0, 1, w_bf, tn=tn, name="qkv_a", flags=(kind < 2).astype(np.int32),
                      gidx=np.minimum(kind, 1).astype(np.int32), gains=jnp.stack([qn, kn]))


META_E, META_W, META_RANK = 0, TOP_K, 2 * TOP_K


def _route_kernel(x_ref, g_ref, sh_ref, sc_ref, rw_ref, rb_ref, cin_ref, h_ref, meta_ref, cnt_ref, carry):
    i = pl.program_id(0)

    @pl.when(i == 0)
    def _():
        carry[...] = cin_ref[...]

    h = _modulate(x_ref[...], g_ref[...], sh_ref[...], sc_ref[...])
    h_ref[...] = h
    tm = h.shape[0]
    logits = _dot(h, rw_ref[...]) + rb_ref[...]
    lane = lax.broadcasted_iota(I32, (tm, LANES), 1)
    work = logits
    sel = jnp.zeros((tm, LANES), F32)
    idxs, vals = [], []
    for _ in range(TOP_K):
        v = jnp.max(work, axis=-1, keepdims=True)
        idx = jnp.min(jnp.where(work == v, lane, LANES), axis=-1, keepdims=True)
        hit = lane == idx
        sel = jnp.where(hit, 1.0, sel)
        work = jnp.where(hit, -jnp.inf, work)
        idxs.append(idx)
        vals.append(v)
    es = [jnp.exp(v - vals[0]) for v in vals]
    den = es[0] + es[1] + es[2] + es[3]
    rr = lax.broadcasted_iota(I32, (tm, tm), 0)
    cc = lax.broadcasted_iota(I32, (tm, tm), 1)
    tri = jnp.where(cc < rr, 1.0, 0.0)
    rank = carry[...] + _dot(tri, sel)
    meta = jnp.zeros((tm, LANES), F32)
    for k in range(TOP_K):
        rk = jnp.sum(jnp.where(lane == idxs[k], rank, 0.0), axis=-1, keepdims=True)
        meta = jnp.where(lane == META_E + k, idxs[k].astype(F32), meta)
        meta = jnp.where(lane == META_W + k, es[k] / den, meta)
        meta = jnp.where(lane == META_RANK + k, rk, meta)
    meta_ref[...] = meta
    carry[...] = carry[...] + jnp.sum(sel, axis=0, keepdims=True)
    cnt_ref[...] = carry[...]


def moe_route(st, x, g, mods, rw_bf, rb, counts_in):
    r, d = x.shape
    tm = st.tm
    return pl.pallas_call(
        _route_kernel,
        out_shape=(jax.ShapeDtypeStruct((r, d), F32), jax.ShapeDtypeStruct((r, LANES), F32),
                   jax.ShapeDtypeStruct((1, LANES), F32)),
        grid=(r // tm,),
        in_specs=[pl.BlockSpec((tm, d), lambda i: (i, 0)),
                  pl.BlockSpec((1, d), lambda i: (0, 0)),
                  st.mod_spec(d, 3), st.mod_spec(d, 4),
                  pl.BlockSpec((d, LANES), lambda i: (0, 0)),
                  pl.BlockSpec((1, LANES), lambda i: (0, 0)),
                  pl.BlockSpec((1, LANES), lambda i: (0, 0))],
        out_specs=(pl.BlockSpec((tm, d), lambda i: (i, 0)),
                   pl.BlockSpec((tm, LANES), lambda i: (i, 0)),
                   pl.BlockSpec((1, LANES), lambda i: (0, 0))),
        scratch_shapes=[pltpu.VMEM((1, LANES), F32)],
        compiler_params=_params(("arbitrary",)),
        name="moe_route",
    )(x, g.reshape(1, d), mods, mods, rw_bf, rb, counts_in)


def _dispatch_kernel(dest_ref, h_ref, xb_in, xb_out, sem, *, tm):
    del xb_in
    i = pl.program_id(0)

    def row(t, carry):
        for k in range(TOP_K):
            dst = dest_ref[(i * tm + t) * TOP_K + k]
            pltpu.make_async_copy(h_ref.at[pl.ds(t, 1)], xb_out.at[pl.ds(dst, 1)], sem).start()
        return carry

    lax.fori_loop(0, tm, row, 0)

    def drain(t, carry):
        for k in range(TOP_K):
            pltpu.make_async_copy(h_ref.at[pl.ds(0, 1)], xb_out.at[pl.ds(0, 1)], sem).wait()
        return carry

    lax.fori_loop(0, tm, drain, 0)


def moe_dispatch(st, h, dest, xb):
    r, d = h.shape
    tm = st.tm
    grid_spec = pltpu.PrefetchScalarGridSpec(
        num_scalar_prefetch=1,
        grid=(r // tm,),
        in_specs=[pl.BlockSpec((tm, d), lambda i, *_: (i, 0)),
                  pl.BlockSpec(memory_space=pl.ANY)],
        out_specs=pl.BlockSpec(memory_space=pl.ANY),
        scratch_shapes=[pltpu.SemaphoreType.DMA(())])
    return pl.pallas_call(
        functools.partial(_dispatch_kernel, tm=tm),
        out_shape=jax.ShapeDtypeStruct(xb.shape, xb.dtype),
        grid_spec=grid_spec,
        input_output_aliases={2: 0},
        compiler_params=_params(("arbitrary",), has_side_effects=True),
        name="moe_dispatch",
    )(dest, h, xb)


def _ffn_kernel(e_ref, row_ref, nsub_ref, xb_hbm, wgu_ref, bgu_ref, wdn_ref, bdn_ref, perm_ref, yb_in, yb_hbm,
                xs, xsb, acc, wg_bf, wd_bf, sem_in, sem_out, *, tb):
    del e_ref, yb_in
    w = pl.program_id(0)
    f = pl.program_id(1)
    nw = pl.num_programs(0)
    nf = pl.num_programs(1)
    nsub = nsub_ref[w]
    live = nsub > 0
    row0 = pl.multiple_of(row_ref[w], MOE_BLOCK)

    def x_copy(item):
        return pltpu.make_async_copy(xb_hbm.at[pl.ds(pl.multiple_of(row_ref[item], MOE_BLOCK), tb)], xs, sem_in)

    @pl.when((w == 0) & (f == 0) & live)
    def _():
        x_copy(0).start()

    @pl.when((f == 0) & live)
    def _():
        x_copy(w).wait()
        xsb[...] = xs[...].astype(BF16)
        acc[...] = jnp.broadcast_to(bdn_ref[...], acc.shape)

    nxt = jnp.minimum(w + 1, nw - 1)

    @pl.when((f == nf - 1) & (w + 1 < nw) & (nsub_ref[nxt] > 0))
    def _():
        x_copy(nxt).start()

    @pl.when(live)
    def _():
        wg_bf[...] = wgu_ref[...].astype(BF16)
        wd_bf[...] = wdn_ref[...].astype(BF16)
        gu = jnp.dot(xsb[...], wg_bf[...], preferred_element_type=F32) + bgu_ref[...]
        lane = lax.broadcasted_iota(I32, gu.shape, 1)
        gate = jnp.minimum(gu, SWIGLU_LIMIT)
        glu = gate * jax.nn.sigmoid(SWIGLU_ALPHA * gate)
        up = jnp.clip(gu, -SWIGLU_LIMIT, SWIGLU_LIMIT) + 1.0
        prod = jnp.where((lane % 2) == 1, up * pltpu.roll(glu, 1, 1), 0.0).astype(BF16)
        act = jnp.dot(prod, perm_ref[...], preferred_element_type=F32).astype(BF16)
        acc[...] += jnp.dot(act, wd_bf[...], preferred_element_type=F32)

    @pl.when((f == nf - 1) & live)
    def _():
        def copy(s):
            return pltpu.make_async_copy(acc.at[pl.ds(s * MOE_BLOCK, MOE_BLOCK)],
                                         yb_hbm.at[pl.ds(row0 + s * MOE_BLOCK, MOE_BLOCK)], sem_out)
        for s in range(tb // MOE_BLOCK):
            @pl.when(s < nsub)
            def _(s=s):
                copy(s).start()
        for s in range(tb // MOE_BLOCK):
            @pl.when(s < nsub)
            def _(s=s):
                copy(s).wait()


def moe_ffn(xb, yb, item_e, item_row, item_nsub, w_gu, b_gu, w_dn, b_dn, layer, *, tb, tf):
    n_layers, n_exp, d, f2 = w_gu.shape
    dff = f2 // 2
    nf = dff // tf
    n_items = item_e.shape[0]
    perm = np.zeros((2 * tf, tf), np.float32)
    perm[2 * np.arange(tf) + 1, np.arange(tf)] = 1.0

    def fsel(w, f, ns):
        return jnp.where(ns[w] > 0, f, nf - 1)

    grid_spec = pltpu.PrefetchScalarGridSpec(
        num_scalar_prefetch=3,
        grid=(n_items, nf),
        in_specs=[pl.BlockSpec(memory_space=pl.ANY),
                  pl.BlockSpec((None, None, d, 2 * tf), lambda w, f, e, r_, ns: (layer, e[w], 0, fsel(w, f, ns))),
                  pl.BlockSpec((None, None, 1, 2 * tf), lambda w, f, e, r_, ns: (layer, e[w], 0, fsel(w, f, ns))),
                  pl.BlockSpec((None, None, tf, d), lambda w, f, e, r_, ns: (layer, e[w], fsel(w, f, ns), 0)),
                  pl.BlockSpec((None, None, 1, d), lambda w, f, e, *_: (layer, e[w], 0, 0)),
                  pl.BlockSpec((2 * tf, tf), lambda w, f, *_: (0, 0)),
                  pl.BlockSpec(memory_space=pl.ANY)],
        out_specs=pl.BlockSpec(memory_space=pl.ANY),
        scratch_shapes=[pltpu.VMEM((tb, d), F32), pltpu.VMEM((tb, d), BF16), pltpu.VMEM((tb, d), F32),
                        pltpu.VMEM((d, 2 * tf), BF16), pltpu.VMEM((tf, d), BF16),
                        pltpu.SemaphoreType.DMA(()), pltpu.SemaphoreType.DMA(())])
    return pl.pallas_call(
        functools.partial(_ffn_kernel, tb=tb),
        out_shape=jax.ShapeDtypeStruct(yb.shape, F32),
        grid_spec=grid_spec,
        input_output_aliases={9: 0},
        compiler_params=_params(("arbitrary", "arbitrary"), has_side_effects=True),
        name="moe_ffn",
    )(item_e, item_row, item_nsub, xb, w_gu, b_gu.reshape(n_layers, n_exp, 1, f2), w_dn,
      b_dn.reshape(n_layers, n_exp, 1, d), jnp.asarray(perm, BF16), yb)


def _combine_kernel(dest_ref, yb_hbm, meta_ref, x_ref, gate_ref, o_ref, ybuf, sem, *, tm):
    i = pl.program_id(0)

    def row(t, carry):
        for k in range(TOP_K):
            src = dest_ref[(i * tm + t) * TOP_K + k]
            pltpu.make_async_copy(yb_hbm.at[pl.ds(src, 1)], ybuf.at[k, pl.ds(t, 1)], sem).start()
        return carry

    lax.fori_loop(0, tm, row, 0)

    def drain(t, carry):
        for k in range(TOP_K):
            pltpu.make_async_copy(yb_hbm.at[pl.ds(0, 1)], ybuf.at[0, pl.ds(0, 1)], sem).wait()
        return carry

    lax.fori_loop(0, tm, drain, 0)
    meta = meta_ref[...]
    y = jnp.zeros(x_ref.shape, F32)
    for k in range(TOP_K):
        y = y + meta[:, META_W + k:META_W + k + 1] * ybuf[k]
    o_ref[...] = x_ref[...] + gate_ref[...] * y


def moe_combine(st, yb, dest, meta, x, mods):
    r, d = x.shape
    tm = min(st.tm, 256)
    tps = st.tiles_per_seq * (st.tm // tm)
    rows = st.r if st.per_row_mods else 1
    grid_spec = pltpu.PrefetchScalarGridSpec(
        num_scalar_prefetch=1,
        grid=(r // tm,),
        in_specs=[pl.BlockSpec(memory_space=pl.ANY),
                  pl.BlockSpec((tm, LANES), lambda i, *_: (i, 0)),
                  pl.BlockSpec((tm, d), lambda i, *_: (i, 0)),
                  pl.BlockSpec((None, rows, d), lambda i, *_: (i // tps, 0, 5))],
        out_specs=pl.BlockSpec((tm, d), lambda i, *_: (i, 0)),
        scratch_shapes=[pltpu.VMEM((TOP_K, tm, d), F32), pltpu.SemaphoreType.DMA(())])
    return pl.pallas_call(
        functools.partial(_combine_kernel, tm=tm),
        out_shape=jax.ShapeDtypeStruct((r, d), F32),
        grid_spec=grid_spec,
        compiler_params=_params(("arbitrary",)),
        name="moe_combine",
    )(dest, yb, meta, x, mods)


def moe_layer(streams, xs, g, mods_list, router_w, router_b, w_gu, b_gu, w_dn, b_dn, layer):
    d = xs[0].shape[1]
    n_exp = router_w.shape[1]
    rw = jnp.zeros((d, LANES), BF16).at[:, :n_exp].set(router_w.astype(BF16))
    rb = jnp.full((1, LANES), NEG, F32).at[0, :n_exp].set(router_b)
    counts = jnp.zeros((1, LANES), F32)
    hs, metas = [], []
    for st, x, mods in zip(streams, xs, mods_list):
        h, meta, counts = moe_route(st, x, g, mods, rw, rb, counts)
        hs.append(h)
        metas.append(meta)
    n_asg = sum(st.r for st in streams) * TOP_K
    tb = MOE_TB
    cnt = counts[0, :n_exp].astype(I32)
    padded = (cnt + MOE_BLOCK - 1) // MOE_BLOCK * MOE_BLOCK
    pend = jnp.cumsum(padded)
    pstart = pend - padded
    n_items = (n_asg + n_exp * (MOE_BLOCK - 1)) // tb + n_exp + 1
    per_e = (padded + tb - 1) // tb
    iend = jnp.cumsum(per_e)
    total = iend[-1]
    wi = jnp.arange(n_items, dtype=I32)
    live = wi < total
    e_of = jnp.minimum(jnp.searchsorted(iend, wi, side='right'), n_exp - 1).astype(I32)
    last_e = jnp.minimum(jnp.searchsorted(iend, total - 1, side='right'), n_exp - 1).astype(I32)
    k_of = wi - (iend - per_e)[e_of]
    item_e = jnp.where(live, e_of, last_e).astype(I32)
    item_row = jnp.where(live, pstart[e_of] + k_of * tb, 0).astype(I32)
    item_nsub = jnp.where(live, jnp.minimum(tb, padded[e_of] - k_of * tb) // MOE_BLOCK, 0).astype(I32)
    n_rows = n_asg + n_exp * (MOE_BLOCK - 1) + tb
    n_rows = -(-n_rows // MOE_BLOCK) * MOE_BLOCK
    xb = jnp.zeros((n_rows, d), F32)
    dests = []
    for st, h, meta in zip(streams, hs, metas):
        e = meta[:, META_E:META_E + TOP_K].astype(I32)
        rank = meta[:, META_RANK:META_RANK + TOP_K].astype(I32)
        dest = (pstart[e] + rank).reshape(-1).astype(I32)
        dests.append(dest)
        xb = moe_dispatch(st, h, dest, xb)
    tf = _pick_tile(w_dn.shape[2], 256)
    yb = moe_ffn(xb, jnp.zeros((n_rows - tb, d), F32), item_e, item_row, item_nsub, w_gu, b_gu, w_dn, b_dn, layer,
                 tb=tb, tf=tf)
    return [moe_combine(st, yb, dest, meta, x, mods)
            for st, dest, meta, x, mods in zip(streams, dests, metas, xs, mods_list)]


def kv_proj(st, x, g, mods, w_bf, kn_kv):
    n = w_bf.shape[1]
    tn = B_KV_HEADS * HEAD_DIM
    j = np.arange(n // tn)
    return mod_linear(st, x, g, mods, 0, 1, w_bf, tn=tn, name="kv_proj", flags=(j % 2 == 0).astype(np.int32),
                      gidx=(j // 2).astype(np.int32), gains=kn_kv)


def q_proj_b(st, x, g, mods, wq_bf, qn, w_gate, b_gate):
    d = x.shape[1]
    n = wq_bf.shape[1]
    nj = n // 512
    q = mod_linear(st, x, g, mods, 0, 1, wq_bf, tn=512, name="q_proj_b", flags=np.ones((nj,), np.int32),
                   gidx=np.zeros((nj,), np.int32), gains=qn)
    per_g = B_HEADS_PER_KV * 3
    wg = jnp.zeros((d, B_KV_HEADS, LANES), F32).at[:, :, :per_g].set(w_gate.reshape(d, B_KV_HEADS, per_g))
    bg = jnp.zeros((B_KV_HEADS, LANES), F32).at[:, :per_g].set(b_gate.reshape(B_KV_HEADS, per_g))
    gates = mod_linear(st, x, g, mods, 0, 1, wg.reshape(d, -1).astype(BF16), tn=B_KV_HEADS * LANES,
                       name="gates_b", bias=bg.reshape(-1), act="sigmoid")
    return q, gates


def _cmp1_kernel(*refs, n_pages, cpp):
    page_refs = refs[:n_pages]
    pe_ref, w1_ref, p_ref, q_ref, rows_scr = refs[n_pages:]
    ch = n_pages * cpp
    rpr = cpp * CMP_STRIDE
    half_len = CMP_LEN // 2
    for j, pr in enumerate(page_refs):
        for c in range(2 * B_KV_HEADS):
            rows_scr[c, j * rpr:(j + 1) * rpr, :] = pr[:, c * HEAD_DIM:(c + 1) * HEAD_DIM]
    for kind in range(2):
        for half, out in ((0, p_ref), (1, q_ref)):
            acc = jnp.zeros((B_KV_HEADS * ch, w1_ref.shape[-1]), F32)
            for l in range(half_len):
                pieces = [rows_scr[kind * B_KV_HEADS + g, pl.ds(l, ch, stride=CMP_STRIDE), :]
                          for g in range(B_KV_HEADS)]
                xl = jnp.concatenate(pieces, axis=0) + pe_ref[kind, pl.ds(half * half_len + l, 1), :]
                acc = acc + _dot(xl, w1_ref[kind, half * half_len + l])
            for g in range(B_KV_HEADS):
                out[kind, g] = acc[g * ch:(g + 1) * ch]


def compress_partials(rows_arr, spec_fn, n_refs, rows_per_ref, grid, pe, w1_bf, n_chunks, prefetch=None):
    assert CMP_LEN == 2 * CMP_STRIDE
    cpp = rows_per_ref // CMP_STRIDE
    ch = n_refs * cpp
    hid = w1_bf.shape[-1]
    nb = grid[0]
    out_shape = jax.ShapeDtypeStruct((nb, 2, B_KV_HEADS, n_chunks, hid), F32)
    out_spec = pl.BlockSpec((None, 2, B_KV_HEADS, ch, hid), lambda b, i, *_: (b, 0, 0, i, 0))
    in_specs = [spec_fn(j) for j in range(n_refs)] + [
        pl.BlockSpec(pe.shape, lambda b, i, *_: (0, 0, 0)),
        pl.BlockSpec(w1_bf.shape, lambda b, i, *_: (0, 0, 0, 0))]
    kern = functools.partial(_cmp1_kernel, n_pages=n_refs, cpp=cpp)
    args = [rows_arr] * n_refs + [pe, w1_bf]
    scratch = [pltpu.VMEM((2 * B_KV_HEADS, ch * CMP_STRIDE, HEAD_DIM), F32)]
    if prefetch is None:
        return pl.pallas_call(kern, out_shape=(out_shape, out_shape), grid=grid, in_specs=in_specs,
                              out_specs=(out_spec, out_spec), scratch_shapes=scratch,
                              compiler_params=_params(("arbitrary", "arbitrary")), name="cmp_partials")(*args)
    gs = pltpu.PrefetchScalarGridSpec(num_scalar_prefetch=1, grid=grid, in_specs=in_specs,
                                      out_specs=(out_spec, out_spec), scratch_shapes=scratch)
    return pl.pallas_call(lambda pt, *r: kern(*r), out_shape=(out_shape, out_shape), grid_spec=gs,
                          compiler_params=_params(("arbitrary", "arbitrary")), name="cmp_partials_paged")(prefetch, *args)


def _cmp2_kernel(p_ref, q_ref, b1_ref, w2_ref, b2_ref, o_ref):
    q = q_ref[...]
    qs = jnp.concatenate([q[1:], jnp.zeros((1, q.shape[1]), F32)], axis=0)
    hid = jax.nn.gelu(p_ref[...] + qs + b1_ref[...])
    o_ref[...] = _dot(hid, w2_ref[...]) + b2_ref[...]


def compress_finish(p, q, b1, w2_bf, b2):
    nb, _, ng, nc, hid = p.shape
    blk = pl.BlockSpec((None, None, None, nc, hid), lambda b, k, g: (b, k, g, 0, 0))
    return pl.pallas_call(
        _cmp2_kernel,
        out_shape=jax.ShapeDtypeStruct((nb, 2, ng, nc, HEAD_DIM), F32),
        grid=(nb, 2, ng),
        in_specs=[blk, blk,
                  pl.BlockSpec((None, 1, hid), lambda b, k, g: (k, 0, 0)),
                  pl.BlockSpec((None, hid, HEAD_DIM), lambda b, k, g: (k, 0, 0)),
                  pl.BlockSpec((None, 1, HEAD_DIM), lambda b, k, g: (k, 0, 0))],
        out_specs=pl.BlockSpec((None, None, None, nc, HEAD_DIM), lambda b, k, g: (b, k, g, 0, 0)),
        compiler_params=_params(("arbitrary", "arbitrary", "arbitrary")),
        name="cmp_finish",
    )(p, q, b1.reshape(2, 1, hid), w2_bf, b2.reshape(2, 1, HEAD_DIM))


def _selection_map(n_cmp_pad, n_cmp, n_slc, n_slc_pad):
    r_s = SLC_LEN // CMP_STRIDE
    r_c = CMP_LEN // CMP_STRIDE
    mult = np.zeros(r_s + r_c - 1, np.float32)
    for m in range(r_s):
        for n in range(r_c):
            mult[m + n] += 1.0
    off = r_s * np.arange(n_slc)[None, :] - np.arange(n_cmp)[:, None]
    ok = (off >= 0) & (off < mult.shape[0])
    out = np.zeros((n_cmp_pad, n_slc_pad), np.float32)
    out[:n_cmp, :n_slc] = np.where(ok, mult[np.clip(off, 0, mult.shape[0] - 1)], 0.0)
    return out


def _stack_heads(q):
    return jnp.concatenate([q[:, h * HEAD_DIM:(h + 1) * HEAD_DIM] for h in range(q.shape[1] // HEAD_DIM)], axis=0)


def _rep_rows(x, n):
    return jnp.concatenate([x] * n, axis=0)


def _cmp_and_select(qs, pos, kc, vc, selmap, *, n_cmp, n_slc, n_sel):
    tq = pos.shape[0]
    hpk = qs.shape[0] // tq
    pos_h = _rep_rows(pos, hpk)
    ncp = kc.shape[0]
    s = _dot_nt(qs, kc) * (HEAD_DIM ** -0.5)
    ci = lax.broadcasted_iota(I32, (hpk * tq, ncp), 1)
    vis = (ci * CMP_STRIDE + (CMP_LEN - 1) <= pos_h) & (ci < n_cmp)
    s = jnp.where(vis, s, NEG)
    e = jnp.where(vis, jnp.exp(s - jnp.max(s, axis=-1, keepdims=True)), 0.0)
    p = e / jnp.maximum(jnp.sum(e, axis=-1, keepdims=True), 1e-30)
    o_cmp = _dot(p, vc)
    pi = _dot(p, selmap)
    imp = pi[0:tq]
    for h in range(1, hpk):
        imp = imp + pi[h * tq:(h + 1) * tq]
    ns = imp.shape[1]
    blk = lax.broadcasted_iota(I32, (tq, ns), 1)
    cur = pos // SLC_LEN
    forced = (blk == 0) | (blk == cur) | (blk == cur - 1)
    valid = blk * SLC_LEN <= pos
    score = jnp.where(valid, imp + jnp.where(forced, FORCE_BONUS, 0.0), -FORCE_BONUS)
    score = jnp.where(blk < n_slc, score, -jnp.inf)

    half = LANES // 2
    if ns == LANES and n_slc <= half and tq % 8 == 0:
        both = jnp.where(blk < half, score, pltpu.roll(score, half, 1))
        j8 = lax.broadcasted_iota(I32, (8, LANES), 1) % half
        rank = jnp.zeros((tq, ns), F32)
        for dd in range(1, half):
            other = pltpu.roll(both, dd, 1)
            tie = _rep_rows(jnp.where(j8 >= dd, 1.0, 0.0), tq // 8)
            rank = rank + jnp.where(other > both, 1.0, 0.0) + jnp.where(other == both, tie, 0.0)
        sel = jnp.where(blk < n_slc, jnp.where(rank < n_sel, 1.0, 0.0), 0.0)
        return o_cmp, sel

    def pick(_, carry):
        sel, work = carry
        v = jnp.max(work, axis=-1, keepdims=True)
        idx = jnp.min(jnp.where(work == v, blk, ns), axis=-1, keepdims=True)
        hit = blk == idx
        return jnp.where(hit, 1.0, sel), jnp.where(hit, -jnp.inf, work)

    sel, _ = lax.fori_loop(0, n_sel, pick, (jnp.zeros((tq, ns), F32), score))
    return o_cmp, sel


def _flash_tiles(qs, k_ref, v_ref, lo, hi, tk, hpk, bias_fn):
    rows = qs.shape[0]
    tq = rows // hpk

    def body(kt, carry):
        m, l, acc = carry
        start = pl.multiple_of(kt * tk, tk)
        k = k_ref[pl.ds(start, tk), :]
        v = v_ref[pl.ds(start, tk), :]
        kpos = kt * tk + lax.broadcasted_iota(I32, (tq, tk), 1)
        s = _dot_nt(qs, k) * (HEAD_DIM ** -0.5) + _rep_rows(bias_fn(kt, kpos), hpk)
        m_new = jnp.maximum(m, jnp.max(s, axis=-1, keepdims=True))
        alpha = jnp.exp(m - m_new)
        p = jnp.exp(s - m_new)
        return m_new, alpha * l + jnp.sum(p, axis=-1, keepdims=True), alpha * acc + _dot(p, v)

    init = (jnp.full((rows, 1), NEG, F32), jnp.zeros((rows, 1), F32), jnp.zeros((rows, HEAD_DIM), F32))
    m, l, acc = lax.fori_loop(lo, hi, body, init)
    return acc / jnp.maximum(l, 1e-30)


def _gate_cols(gt, branch, hpk):
    return jnp.concatenate([gt[:, h * 3 + branch:h * 3 + branch + 1] for h in range(hpk)], axis=0)


def _nsa_prompt_kernel(q_ref, gate_ref, kc_ref, vc_ref, ks_ref, vs_ref, kw_ref, vw_ref, selmap_ref, expand_ref,
                       o_ref, *, tq, tk, n_cmp, n_slc, n_sel):
    qi = pl.program_id(2)
    hpk = B_HEADS_PER_KV
    qs = _stack_heads(q_ref[...]).astype(BF16)
    pos = qi * tq + lax.broadcasted_iota(I32, (tq, 1), 0)
    o_cmp, sel = _cmp_and_select(qs, pos, kc_ref[...], vc_ref[...], selmap_ref[...],
                                 n_cmp=n_cmp, n_slc=n_slc, n_sel=n_sel)
    sel_bf = sel.astype(BF16)
    hi = (qi * tq + tq - 1) // tk + 1

    def bias_slc(kt, kpos):
        picked = jnp.dot(sel_bf, expand_ref[kt], preferred_element_type=F32)
        return jnp.where(kpos <= pos, (picked - 1.0) * (-NEG), NEG)

    o_slc = _flash_tiles(qs, ks_ref, vs_ref, 0, hi, tk, hpk, bias_slc)

    def bias_win(kt, kpos):
        rel = pos - kpos
        return jnp.where(rel >= 0, jnp.where(rel < B_WINDOW, 0.0, NEG), NEG)

    lo = jnp.maximum(qi * tq - (B_WINDOW - 1), 0) // tk
    o_win = _flash_tiles(qs, kw_ref, vw_ref, lo, hi, tk, hpk, bias_win)
    gt = gate_ref[...]
    o = _gate_cols(gt, 0, hpk) * o_cmp + _gate_cols(gt, 1, hpk) * o_slc + _gate_cols(gt, 2, hpk) * o_win
    for h in range(hpk):
        o_ref[:, h * HEAD_DIM:(h + 1) * HEAD_DIM] = o[h * tq:(h + 1) * tq]


def nsa_prompt(st, q, gates, kvs, cmp_kv):
    t = st.t
    tq = _pick_tile(t, 128, 8)
    tk = _pick_tile(t, 256, 8)
    nq = t // tq
    ncp = cmp_kv.shape[3]
    n_cmp = t // CMP_STRIDE - CMP_LEN // CMP_STRIDE + 1
    n_slc = -(-t // SLC_LEN)
    ns = -(-n_slc // LANES) * LANES
    selmap = jnp.asarray(_selection_map(ncp, n_cmp, n_slc, ns), BF16)
    expand = np.zeros((t // tk, ns, tk), np.float32)
    kk = np.arange(t)
    expand[kk // tk, kk // SLC_LEN, kk % tk] = 1.0
    gw = B_HEADS_PER_KV * HEAD_DIM
    kv_blk = lambda col: pl.BlockSpec((t, HEAD_DIM), lambda b, g, i: (b, col + g))
    cmp_blk = lambda kind: pl.BlockSpec((None, None, None, ncp, HEAD_DIM), lambda b, g, i: (b, kind, g, 0, 0))
    return pl.pallas_call(
        functools.partial(_nsa_prompt_kernel, tq=tq, tk=tk, n_cmp=n_cmp, n_slc=n_slc, n_sel=min(N_SELECT, n_slc)),
        out_shape=jax.ShapeDtypeStruct((st.r, B_HEADS * HEAD_DIM), F32),
        grid=(st.b, B_KV_HEADS, nq),
        in_specs=[pl.BlockSpec((tq, gw), lambda b, g, i: (b * nq + i, g)),
                  pl.BlockSpec((tq, LANES), lambda b, g, i: (b * nq + i, g)),
                  cmp_blk(0), cmp_blk(1),
                  kv_blk(2 * B_KV_HEADS), kv_blk(3 * B_KV_HEADS), kv_blk(4 * B_KV_HEADS), kv_blk(5 * B_KV_HEADS),
                  pl.BlockSpec(selmap.shape, lambda b, g, i: (0, 0)),
                  pl.BlockSpec(expand.shape, lambda b, g, i: (0, 0, 0))],
        out_specs=pl.BlockSpec((tq, gw), lambda b, g, i: (b * nq + i, g)),
        compiler_params=_params(("arbitrary", "arbitrary", "arbitrary")),
        name="nsa_prompt",
    )(q, gates, cmp_kv, cmp_kv, kvs, kvs, kvs, kvs, selmap, jnp.asarray(expand, BF16))


def _nsa_s1_kernel(q_ref, kc_ref, vc_ref, selmap_ref, ocmp_ref, sel_ref, *, tq, past_len, n_cmp, n_slc, n_sel):
    qs = _stack_heads(q_ref[...]).astype(BF16)
    pos = past_len + lax.broadcasted_iota(I32, (tq, 1), 0)
    o_cmp, sel = _cmp_and_select(qs, pos, kc_ref[...], vc_ref[...], selmap_ref[...],
                                 n_cmp=n_cmp, n_slc=n_slc, n_sel=n_sel)
    ocmp_ref[...] = o_cmp
    sel_ref[...] = sel


def _nsa_s2_kernel(*refs, tq, past_len, pps):
    q_ref = refs[1]
    page_refs = refs[2:2 + pps]
    sel_ref, new_ref, o_ref, m_scr, l_scr, acc_scr = refs[2 + pps:]
    p = pl.program_id(1)
    n_steps = pl.num_programs(1)
    hpk = B_HEADS_PER_KV
    gw = B_KV_HEADS * HEAD_DIM
    scale = HEAD_DIM ** -0.5
    ns = sel_ref.shape[-1]

    @pl.when(p == 0)
    def _():
        m_scr[...] = jnp.full(m_scr.shape, NEG, F32)
        l_scr[...] = jnp.zeros(l_scr.shape, F32)
        acc_scr[...] = jnp.zeros(acc_scr.shape, F32)

    q = q_ref[...]

    def update(g, s, ok, v):
        s = jnp.where(ok, s, NEG)
        m_old = m_scr[g]
        m_new = jnp.maximum(m_old, jnp.max(s, axis=-1, keepdims=True))
        alpha = jnp.exp(m_old - m_new)
        pr = jnp.where(ok, jnp.exp(s - m_new), 0.0)
        l_scr[g] = alpha * l_scr[g] + jnp.sum(pr, axis=-1, keepdims=True)
        acc_scr[g] = alpha * acc_scr[g] + _dot(pr, v)
        m_scr[g] = m_new

    nk = pps * PAGE_SIZE
    nn = lax.broadcasted_iota(I32, (ns, nk), 0)
    jj = lax.broadcasted_iota(I32, (ns, nk), 1)
    expand = jnp.where(nn == p * (nk // SLC_LEN) + jj // SLC_LEN, 1.0, 0.0).astype(BF16)
    for g in range(B_KV_HEADS):
        qs = _stack_heads(q[:, g * hpk * HEAD_DIM:(g + 1) * hpk * HEAD_DIM]).astype(BF16)
        k = jnp.concatenate([pr[:, g * HEAD_DIM:(g + 1) * HEAD_DIM] for pr in page_refs], axis=0)
        v = jnp.concatenate([pr[:, gw + g * HEAD_DIM:gw + (g + 1) * HEAD_DIM] for pr in page_refs], axis=0)
        picked = jnp.dot(sel_ref[g].astype(BF16), expand, preferred_element_type=F32)
        update(g, _dot_nt(qs, k) * scale, _rep_rows(picked, hpk) > 0.5, v)

    @pl.when(p == n_steps - 1)
    def _():
        nn2 = lax.broadcasted_iota(I32, (ns, tq), 0)
        tt2 = lax.broadcasted_iota(I32, (ns, tq), 1)
        expand2 = jnp.where(nn2 == (past_len + tt2) // SLC_LEN, 1.0, 0.0).astype(BF16)
        t_q = _rep_rows(lax.broadcasted_iota(I32, (tq, tq), 0), hpk)
        t_k = _rep_rows(lax.broadcasted_iota(I32, (tq, tq), 1), hpk)
        for g in range(B_KV_HEADS):
            qs = _stack_heads(q[:, g * hpk * HEAD_DIM:(g + 1) * hpk * HEAD_DIM]).astype(BF16)
            k = new_ref[:, g * HEAD_DIM:(g + 1) * HEAD_DIM]
            v = new_ref[:, gw + g * HEAD_DIM:gw + (g + 1) * HEAD_DIM]
            picked = jnp.dot(sel_ref[g].astype(BF16), expand2, preferred_element_type=F32)
            update(g, _dot_nt(qs, k) * scale, (t_k <= t_q) & (_rep_rows(picked, hpk) > 0.5), v)
            o_ref[g] = acc_scr[g] / jnp.maximum(l_scr[g], 1e-30)


def _nsa_s3_kernel(q_ref, gate_ref, win_ref, new_ref, ocmp_ref, oslc_ref, o_ref, *, tq):
    hpk = B_HEADS_PER_KV
    gw = B_KV_HEADS * HEAD_DIM
    scale = HEAD_DIM ** -0.5
    lb = win_ref.shape[0]
    rows = hpk * tq
    q = q_ref[...]
    gt_all = gate_ref[...]
    t_q1 = _rep_rows(lax.broadcasted_iota(I32, (tq, lb), 0), hpk)
    i_k1 = _rep_rows(lax.broadcasted_iota(I32, (tq, lb), 1), hpk)
    ok1 = (t_q1 + lb - i_k1) < B_WINDOW
    t_q2 = _rep_rows(lax.broadcasted_iota(I32, (tq, tq), 0), hpk)
    t_k2 = _rep_rows(lax.broadcasted_iota(I32, (tq, tq), 1), hpk)
    ok2 = t_k2 <= t_q2
    for g in range(B_KV_HEADS):
        qs = _stack_heads(q[:, g * hpk * HEAD_DIM:(g + 1) * hpk * HEAD_DIM]).astype(BF16)
        s1 = jnp.where(ok1, _dot_nt(qs, win_ref[:, g * HEAD_DIM:(g + 1) * HEAD_DIM]) * scale, NEG)
        s2 = jnp.where(ok2, _dot_nt(qs, new_ref[:, g * HEAD_DIM:(g + 1) * HEAD_DIM]) * scale, NEG)
        m = jnp.maximum(jnp.max(s1, axis=-1, keepdims=True), jnp.max(s2, axis=-1, keepdims=True))
        p1 = jnp.where(ok1, jnp.exp(s1 - m), 0.0)
        p2 = jnp.where(ok2, jnp.exp(s2 - m), 0.0)
        den = jnp.sum(p1, axis=-1, keepdims=True) + jnp.sum(p2, axis=-1, keepdims=True)
        o_win = (_dot(p1, win_ref[:, gw + g * HEAD_DIM:gw + (g + 1) * HEAD_DIM])
                 + _dot(p2, new_ref[:, gw + g * HEAD_DIM:gw + (g + 1) * HEAD_DIM])) / jnp.maximum(den, 1e-30)
        gt = gt_all[:, g * LANES:(g + 1) * LANES]
        o = (_gate_cols(gt, 0, hpk) * ocmp_ref[g] + _gate_cols(gt, 1, hpk) * oslc_ref[g]
             + _gate_cols(gt, 2, hpk) * o_win)
        for h in range(hpk):
            col = (g * hpk + h) * HEAD_DIM
            o_ref[:, col:col + HEAD_DIM] = o[h * tq:(h + 1) * tq]


def nsa_sample(st, q, gates, kvs, cmp_kv, cache, page_table, state_win, past_len):
    nb, tq = st.b, st.t
    hpk = B_HEADS_PER_KV
    rows = hpk * tq
    ncp = cmp_kv.shape[3]
    l_all = past_len + tq
    n_cmp = l_all // CMP_STRIDE - CMP_LEN // CMP_STRIDE + 1
    n_slc = -(-l_all // SLC_LEN)
    ns = -(-n_slc // LANES) * LANES
    assert n_cmp <= ncp
    selmap = jnp.asarray(_selection_map(ncp, n_cmp, n_slc, ns), BF16)
    gw = hpk * HEAD_DIM
    cmp_blk = lambda kind: pl.BlockSpec((None, None, None, ncp, HEAD_DIM), lambda b, g: (b, kind, g, 0, 0))
    o_cmp, sel = pl.pallas_call(
        functools.partial(_nsa_s1_kernel, tq=tq, past_len=past_len, n_cmp=n_cmp, n_slc=n_slc,
                          n_sel=min(N_SELECT, n_slc)),
        out_shape=(jax.ShapeDtypeStruct((nb, B_KV_HEADS, rows, HEAD_DIM), F32),
                   jax.ShapeDtypeStruct((nb, B_KV_HEADS, tq, ns), F32)),
        grid=(nb, B_KV_HEADS),
        in_specs=[pl.BlockSpec((tq, gw), lambda b, g: (b, g)), cmp_blk(0), cmp_blk(1),
                  pl.BlockSpec(selmap.shape, lambda b, g: (0, 0))],
        out_specs=(pl.BlockSpec((None, None, rows, HEAD_DIM), lambda b, g: (b, g, 0, 0)),
                   pl.BlockSpec((None, None, tq, ns), lambda b, g: (b, g, 0, 0))),
        compiler_params=_params(("arbitrary", "arbitrary")),
        name="nsa_sample_cmp",
    )(q, cmp_kv, cmp_kv, selmap)

    n_pages = page_table.shape[1]
    page_w = cache.shape[2] * cache.shape[3] * cache.shape[4]
    cache3 = cache.reshape(cache.shape[0], cache.shape[1], page_w)
    half = page_w // 2
    grp = pl.BlockSpec((None, B_KV_HEADS, rows, HEAD_DIM), lambda b, *_: (b, 0, 0, 0))
    pps = _pick_tile(n_pages, 8, 1)
    page_spec = lambda j: pl.BlockSpec((None, PAGE_SIZE, half),
                                       lambda b, p, pt: (pt[b * n_pages + p * pps + j], 0, 1))
    o_slc = pl.pallas_call(
        functools.partial(_nsa_s2_kernel, tq=tq, past_len=past_len, pps=pps),
        out_shape=jax.ShapeDtypeStruct((nb, B_KV_HEADS, rows, HEAD_DIM), F32),
        grid_spec=pltpu.PrefetchScalarGridSpec(
            num_scalar_prefetch=1,
            grid=(nb, n_pages // pps),
            in_specs=[pl.BlockSpec((tq, B_HEADS * HEAD_DIM), lambda b, p, pt: (b, 0))]
            + [page_spec(j) for j in range(pps)]
            + [pl.BlockSpec((None, B_KV_HEADS, tq, ns), lambda b, p, pt: (b, 0, 0, 0)),
               pl.BlockSpec((tq, half), lambda b, p, pt: (b, 1))],
            out_specs=grp,
            scratch_shapes=[pltpu.VMEM((B_KV_HEADS, rows, 1), F32), pltpu.VMEM((B_KV_HEADS, rows, 1), F32),
                            pltpu.VMEM((B_KV_HEADS, rows, HEAD_DIM), F32)]),
        compiler_params=_params(("arbitrary", "arbitrary")),
        name="nsa_sample_slc",
    )(page_table.reshape(-1), q, *([cache3] * pps), sel, kvs)

    lb = state_win.shape[1]
    win2 = state_win.reshape(nb, lb, 2 * B_KV_HEADS * HEAD_DIM)
    return pl.pallas_call(
        functools.partial(_nsa_s3_kernel, tq=tq),
        out_shape=jax.ShapeDtypeStruct((st.r, B_HEADS * HEAD_DIM), F32),
        grid=(nb,),
        in_specs=[pl.BlockSpec((tq, B_HEADS * HEAD_DIM), lambda b: (b, 0)),
                  pl.BlockSpec((tq, B_KV_HEADS * LANES), lambda b: (b, 0)),
                  pl.BlockSpec((None, lb, half), lambda b: (b, 0, 0)),
                  pl.BlockSpec((tq, half), lambda b: (b, 2)),
                  grp, grp],
        out_specs=pl.BlockSpec((tq, B_HEADS * HEAD_DIM), lambda b: (b, 0)),
        compiler_params=_params(("arbitrary",)),
        name="nsa_sample_win",
    )(q, gates, win2, kvs, o_cmp, o_slc)


def kernel(x_prompt, x_sample, state_dil0, state_dil1, state_dil2, cache_nsa, state_win, page_table, c_prompt, c_sample, w_mod, b_mod, g_norm, w_qkv_a, w_o_a, qn_a, kn_a, w_q_b, qn_b, w_gate_b, b_gate_b, w_o_b, w_mod_kv, b_mod_kv, g_kv, w_kv, kn_kv, cmp_pe, cmp_w1, cmp_b1, cmp_w2, cmp_b2, router_w, router_b, w_gu, b_gu, w_down, b_down):
    bp, tp, d = x_prompt.shape
    bs, ts, _ = x_sample.shape
    past_len = page_table.shape[1] * PAGE_SIZE
    stp = Stream(bp, tp, jnp.arange(tp), 512)
    sts = Stream(bs, ts, past_len + jnp.arange(ts), 512)
    streams = [stp, sts]
    n_pages = page_table.shape[1]
    depth = w_mod.shape[0]
    n_a = w_qkv_a.shape[0]
    ng = len(DIL_RATES)
    dil_bufs = (state_dil0, state_dil1, state_dil2)

    c_all = jnp.concatenate([c_prompt, c_sample], axis=0)
    m_all = mod_vectors(c_all, w_mod, b_mod)
    m_kv = mod_vectors(c_all, w_mod_kv[None], b_mod_kv[None])[0]

    def split(m):
        return [stp.mods(m[:bp]), sts.mods(m[bp:])]

    xs = [x_prompt.reshape(-1, d), x_sample.reshape(-1, d)]
    a_qkv = []
    kvs = cmp_ctx = None
    for l in range(depth):
        mods = split(m_all[l])
        g_attn = g_norm[l, 0]
        if l < n_a:
            wq, wo = w_qkv_a[l].astype(BF16), w_o_a[l].astype(BF16)
            qkvs = [qkv_a(st, x, g_attn, md, wq, qn_a[l], kn_a[l]) for st, x, md in zip(streams, xs, mods)]
            a_qkv.append(qkvs)
            outs = [dilated_prompt(stp, qkvs[0]), dilated_sample(sts, qkvs[1], dil_bufs, l)]
        else:
            lb = l - n_a
            wq, wo = w_q_b[lb].astype(BF16), w_o_b[lb].astype(BF16)
            qg = [q_proj_b(st, x, g_attn, md, wq, qn_b[lb], w_gate_b[lb], b_gate_b[lb])
                  for st, x, md in zip(streams, xs, mods)]
            outs = [nsa_prompt(stp, qg[0][0], qg[0][1], kvs[0], cmp_ctx[0]),
                    nsa_sample(sts, qg[1][0], qg[1][1], kvs[1], cmp_ctx[1], cache_nsa, page_table, state_win,
                               past_len)]
        xs = [out_proj_residual(st, o, wo, x, md, 2) for st, o, x, md in zip(streams, outs, xs, mods)]
        xs = moe_layer(streams, xs, g_norm[l, 1], mods, router_w[l], router_b[l], w_gu, b_gu, w_down, b_down, l)
        if l == n_a - 1:
            wkv = w_kv.astype(BF16)
            kvs = [kv_proj(st, x, g_kv, md, wkv, kn_kv) for st, x, md in zip(streams, xs, split(m_kv))]
            w1, w2 = cmp_w1.astype(BF16), cmp_w2.astype(BF16)
            cmp_w = 2 * B_KV_HEADS * HEAD_DIM
            rows_p = _pick_tile(tp, 1024, CMP_STRIDE)
            steps_p = tp // rows_p
            pq = compress_partials(
                kvs[0], lambda j: pl.BlockSpec((rows_p, cmp_w), lambda b, i: (b * steps_p + i, 0)),
                1, rows_p, (bp, steps_p), cmp_pe, w1, tp // CMP_STRIDE)
            cmp_p = compress_finish(pq[0], pq[1], cmp_b1, w2, cmp_b2)
            assert (past_len + ts) // CMP_STRIDE == past_len // CMP_STRIDE
            pps = _pick_tile(n_pages, 8, 1)
            cache3 = cache_nsa.reshape(cache_nsa.shape[0], PAGE_SIZE, -1)
            pq = compress_partials(
                cache3,
                lambda j: pl.BlockSpec((None, PAGE_SIZE, cmp_w),
                                       lambda b, i, pt: (pt[b * n_pages + i * pps + j], 0, 0)),
                pps, PAGE_SIZE, (bs, n_pages // pps), cmp_pe, w1, past_len // CMP_STRIDE,
                prefetch=page_table.reshape(-1))
            cmp_s = compress_finish(pq[0], pq[1], cmp_b1, w2, cmp_b2)
            cmp_ctx = [cmp_p, cmp_s]

    def a_rows(qkv, nb, t, kind, g):
        return qkv.reshape(nb, t, 3, ng, A_HEADS, HEAD_DIM)[:, :, kind, g]

    p_dil, s_dil = [], []
    for g in range(ng):
        keep = min(DIL_WINDOWS[g], tp)
        p_dil.append(jnp.stack([
            jnp.stack([a_rows(a_qkv[l][0], bp, tp, 1, g)[:, tp - keep:],
                       a_rows(a_qkv[l][0], bp, tp, 2, g)[:, tp - keep:]], axis=2) for l in range(n_a)]))
        lb = dil_bufs[g].shape[2]
        keep = min(DIL_WINDOWS[g], lb + ts)
        per_layer = []
        for l in range(n_a):
            ka = jnp.concatenate([dil_bufs[g][l][:, :, 0], a_rows(a_qkv[l][1], bs, ts, 1, g)], axis=1)
            va = jnp.concatenate([dil_bufs[g][l][:, :, 1], a_rows(a_qkv[l][1], bs, ts, 2, g)], axis=1)
            per_layer.append(jnp.stack([ka[:, lb + ts - keep:], va[:, lb + ts - keep:]], axis=2))
        s_dil.append(jnp.stack(per_layer))
    rows_w = 4 * B_KV_HEADS * HEAD_DIM
    p_rows = kvs[0][:, :rows_w].reshape(bp, tp, 4, B_KV_HEADS, HEAD_DIM)
    p_winr = kvs[0][:, rows_w:].reshape(bp, tp, 2, B_KV_HEADS, HEAD_DIM)
    p_win = p_winr[:, tp - min(B_WINDOW, tp):]
    s_rows = kvs[1][:, :rows_w].reshape(bs, ts, 4, B_KV_HEADS, HEAD_DIM)
    lbw = state_win.shape[1]
    win_all = jnp.concatenate([state_win, kvs[1][:, rows_w:].reshape(bs, ts, 2, B_KV_HEADS, HEAD_DIM)], axis=1)
    s_win = win_all[:, lbw + ts - min(B_WINDOW, lbw + ts):]
    return (xs[0].reshape(bp, tp, d), xs[1].reshape(bs, ts, d), p_dil[0], p_dil[1], p_dil[2], p_rows, p_win,
            s_dil[0], s_dil[1], s_dil[2], s_rows, s_win)
```

```python
import functools

import numpy as np
import jax
import jax.numpy as jnp
from jax import lax
from jax.experimental import pallas as pl
from jax.experimental.pallas import tpu as pltpu

F32 = jnp.float32
BF16 = jnp.bfloat16
I32 = jnp.int32

HEAD_DIM = 128
LANES = 128
DIL_WINDOWS = (128, 512, 2048)
DIL_RATES = (1, 4, 16)
A_HEADS = 8
B_HEADS = 16
B_KV_HEADS = 4
B_HEADS_PER_KV = B_HEADS // B_KV_HEADS
CMP_LEN = 32
CMP_STRIDE = 16
SLC_LEN = 64
N_SELECT = 16
B_WINDOW = 512
FORCE_BONUS = 1.0e4
N_EXPERTS = 32
TOP_K = 4
SWIGLU_ALPHA = 1.702
SWIGLU_LIMIT = 7.0
MOE_BLOCK = 128
MOE_TB = 1152
PAGE_SIZE = 128
ROPE_THETA = 10000.0
NORM_EPS = 1e-6
N_MODS = 6
NEG = -1e30
VMEM_LIMIT = 56 * 1024 * 1024

_NT = (((1,), (1,)), ((), ()))


def _params(sem, **kw):
    return pltpu.CompilerParams(dimension_semantics=sem, vmem_limit_bytes=VMEM_LIMIT, **kw)


def _dot(a, b):
    return jnp.dot(a.astype(BF16), b.astype(BF16), preferred_element_type=F32)


def _dot_nt(a, b):
    return lax.dot_general(a.astype(BF16), b.astype(BF16), _NT, preferred_element_type=F32)


def _pick_tile(n, cap, mult=LANES):
    if n <= cap:
        return n
    t = (cap // mult) * mult
    while t > mult and n % t:
        t -= mult
    assert n % t == 0, (n, cap)
    return t


def _rope_tables(pos):
    half = HEAD_DIM // 2
    inv_freq = 1.0 / (ROPE_THETA ** (jnp.arange(half, dtype=F32) * 2.0 / HEAD_DIM))
    ang = pos.astype(F32)[:, None] * inv_freq[None, :]
    c, s = jnp.cos(ang), jnp.sin(ang)
    return jnp.concatenate([c, c], axis=-1), jnp.concatenate([-s, s], axis=-1)


def _modulate(x, g, shift, scale):
    y = x * lax.rsqrt(jnp.mean(x * x, axis=-1, keepdims=True) + NORM_EPS) * g
    return y * (1.0 + scale) + shift


def _head_norm_rope(blk, gain, cos2, sin2):
    y = blk * lax.rsqrt(jnp.mean(blk * blk, axis=-1, keepdims=True) + NORM_EPS) * gain
    return y * cos2 + pltpu.roll(y, HEAD_DIM // 2, 1) * sin2


def _mod_kernel(c_ref, w_ref, b_ref, o_ref):
    c = c_ref[...]
    o_ref[...] = _dot(c * jax.nn.sigmoid(c), w_ref[...]) + b_ref[...]


def mod_vectors(c, w, b):
    n_layers, d, n = w.shape
    nc = c.shape[0]
    tn = _pick_tile(n, max(LANES, (8 << 20) // (4 * d)))
    return pl.pallas_call(
        _mod_kernel,
        out_shape=jax.ShapeDtypeStruct((n_layers, nc, n), F32),
        grid=(n_layers, n // tn),
        in_specs=[pl.BlockSpec((nc, d), lambda l, j: (0, 0)),
                  pl.BlockSpec((None, d, tn), lambda l, j: (l, 0, j)),
                  pl.BlockSpec((None, 1, tn), lambda l, j: (l, 0, j))],
        out_specs=pl.BlockSpec((None, nc, tn), lambda l, j: (l, 0, j)),
        compiler_params=_params(("arbitrary", "arbitrary")),
        name="mod_vectors",
    )(c, w, b.reshape(n_layers, 1, n))


class Stream:
    def __init__(self, b, t, pos, tm_cap):
        self.b, self.t, self.r = b, t, b * t
        self.per_row_mods = t < LANES
        self.tm = self.r if self.per_row_mods else _pick_tile(t, tm_cap, 8)
        self.tiles_per_seq = 1 if self.per_row_mods else t // self.tm
        cos2, sin2 = _rope_tables(pos)
        if self.per_row_mods:
            cos2, sin2 = jnp.tile(cos2, (b, 1)), jnp.tile(sin2, (b, 1))
        self.cos2, self.sin2 = cos2, sin2

    def mods(self, m):
        if self.per_row_mods:
            return jnp.repeat(m, self.t, axis=0)[None]
        return m[:, None, :]

    def mod_spec(self, d, k, n_grid_axes=2):
        rows = self.r if self.per_row_mods else 1
        tps = self.tiles_per_seq
        return pl.BlockSpec((None, rows, d), lambda i, *_: (i // tps, 0, k))

    def rope_spec(self):
        tps = self.tiles_per_seq
        return pl.BlockSpec((self.tm, HEAD_DIM), lambda i, *_: (i % tps, 0))


def _lin_kernel(flag_ref, gidx_ref, x_ref, g_ref, sh_ref, sc_ref, w_ref, gain_ref, cos_ref, sin_ref, b_ref,
                o_ref, h_scr, *, act):
    del gidx_ref
    j = pl.program_id(1)

    @pl.when(j == 0)
    def _():
        h_scr[...] = _modulate(x_ref[...], g_ref[...], sh_ref[...], sc_ref[...]).astype(BF16)

    acc = jnp.dot(h_scr[...], w_ref[...], preferred_element_type=F32) + b_ref[...]
    tn = acc.shape[1]

    @pl.when(flag_ref[j] == 0)
    def _():
        o_ref[...] = (jax.nn.sigmoid(acc) if act == "sigmoid" else acc).astype(o_ref.dtype)

    @pl.when(flag_ref[j] != 0)
    def _():
        for hh in range(tn // HEAD_DIM):
            sl = slice(hh * HEAD_DIM, (hh + 1) * HEAD_DIM)
            o_ref[:, sl] = _head_norm_rope(acc[:, sl], gain_ref[...], cos_ref[...], sin_ref[...]).astype(o_ref.dtype)


def mod_linear(st, x, g, mods, ksh, ksc, w_bf, *, tn, name, flags=None, gidx=None, gains=None, bias=None,
               act=None):
    r, d = x.shape
    n = w_bf.shape[1]
    nj = n // tn
    if flags is None:
        flags = np.zeros((nj,), np.int32)
        gidx = np.zeros((nj,), np.int32)
        gains = jnp.ones((1, HEAD_DIM), F32)
    if bias is None:
        bias = jnp.zeros((n,), F32)
    gains = gains.reshape(-1, 1, HEAD_DIM)
    tm = st.tm
    grid_spec = pltpu.PrefetchScalarGridSpec(
        num_scalar_prefetch=2,
        grid=(r // tm, nj),
        in_specs=[pl.BlockSpec((tm, d), lambda i, j, *_: (i, 0)),
                  pl.BlockSpec((1, d), lambda i, j, *_: (0, 0)),
                  st.mod_spec(d, ksh), st.mod_spec(d, ksc),
                  pl.BlockSpec((d, tn), lambda i, j, *_: (0, j)),
                  pl.BlockSpec((None, 1, HEAD_DIM), lambda i, j, fl, gi: (gi[j], 0, 0)),
                  st.rope_spec(), st.rope_spec(),
                  pl.BlockSpec((1, tn), lambda i, j, *_: (0, j))],
        out_specs=pl.BlockSpec((tm, tn), lambda i, j, *_: (i, j)),
        scratch_shapes=[pltpu.VMEM((tm, d), BF16)])
    return pl.pallas_call(
        functools.partial(_lin_kernel, act=act),
        out_shape=jax.ShapeDtypeStruct((r, n), F32),
        grid_spec=grid_spec,
        compiler_params=_params(("arbitrary", "arbitrary")),
        name=name,
    )(jnp.asarray(flags, I32), jnp.asarray(gidx, I32), x, g.reshape(1, d), mods, mods, w_bf, gains,
      st.cos2, st.sin2, bias.reshape(1, n))


def _out_kernel(o_ref, w_ref, x_ref, gate_ref, y_ref):
    y_ref[...] = x_ref[...] + gate_ref[...] * _dot(o_ref[...], w_ref[...])


def out_proj_residual(st, o, w_bf, x, mods, kgate):
    r, kd = o.shape
    d = x.shape[1]
    tn = _pick_tile(d, 512)
    tm = st.tm
    nj = d // tn
    rows = st.r if st.per_row_mods else 1
    tps = st.tiles_per_seq
    return pl.pallas_call(
        _out_kernel,
        out_shape=jax.ShapeDtypeStruct((r, d), F32),
        grid=(r // tm, nj),
        in_specs=[pl.BlockSpec((tm, kd), lambda i, j: (i, 0)),
                  pl.BlockSpec((kd, tn), lambda i, j: (0, j)),
                  pl.BlockSpec((tm, tn), lambda i, j: (i, j)),
                  pl.BlockSpec((None, rows, tn), lambda i, j: (i // tps, 0, kgate * nj + j))],
        out_specs=pl.BlockSpec((tm, tn), lambda i, j: (i, j)),
        compiler_params=_params(("arbitrary", "arbitrary")),
        name="out_proj",
    )(o, w_bf, x, mods)


def _dil_prompt_kernel(*refs, wr, rates, sup):
    ng = len(rates)
    o_ref = refs[5 * ng]
    kf, vf, m_scr, l_scr, acc_scr = refs[5 * ng + 1:]
    i = pl.program_id(1)
    scale = HEAD_DIM ** -0.5
    qi = lax.broadcasted_iota(I32, (wr, 2 * wr), 0)
    kk = lax.broadcasted_iota(I32, (wr, 2 * wr), 1)
    band = (kk >= qi) & (kk <= qi + wr)
    for g, rate in enumerate(rates):
        q_ref, kp_ref, kc_ref, vp_ref, vc_ref = refs[5 * g:5 * g + 5]
        prev = wr * rate
        kf[0:prev, :] = kp_ref[...]
        kf[prev:prev + sup, :] = kc_ref[...]
        vf[0:prev, :] = vp_ref[...]
        vf[prev:prev + sup, :] = vc_ref[...]

        def body(idx, carry, q_ref=q_ref, rate=rate, g=g):
            c = idx % rate
            qb = idx // rate
            start = c + rate * qb * wr
            if rate == 1:
                q = q_ref[pl.ds(start, wr), :]
                k = kf[pl.ds(start, 2 * wr), :]
                v = vf[pl.ds(start, 2 * wr), :]
            else:
                q = q_ref[pl.ds(start, wr, stride=rate), :]
                k = kf[pl.ds(start, 2 * wr, stride=rate), :]
                v = vf[pl.ds(start, 2 * wr, stride=rate), :]
            s = _dot_nt(q, k) * scale
            ok = band & ((kk >= wr) | (qb > 0) | (i > 0))
            s = jnp.where(ok, s, NEG)
            m = jnp.max(s, axis=-1, keepdims=True)
            p = jnp.exp(s - m)
            l = jnp.sum(p, axis=-1, keepdims=True)
            acc = _dot(p, v)
            rows = pl.ds(start, wr) if rate == 1 else pl.ds(start, wr, stride=rate)
            if g == 0:
                m_scr[rows, :] = jnp.broadcast_to(m, (wr, HEAD_DIM))
                l_scr[rows, :] = jnp.broadcast_to(l, (wr, HEAD_DIM))
                acc_scr[rows, :] = acc
            else:
                m_old = m_scr[rows, :]
                m_new = jnp.maximum(m_old, m)
                a = jnp.exp(m_old - m_new)
                b = jnp.exp(m - m_new)
                l_scr[rows, :] = a * l_scr[rows, :] + b * l
                acc_scr[rows, :] = a * acc_scr[rows, :] + b * acc
                m_scr[rows, :] = m_new
            return carry

        n_blk = sup // wr
        lax.fori_loop(0, n_blk, body, 0, unroll=4 if n_blk % 4 == 0 else 1)
    o_ref[...] = acc_scr[...] / l_scr[...]


def dilated_prompt(st, qkv):
    ng = len(DIL_RATES)
    wr = DIL_WINDOWS[0] // DIL_RATES[0]
    assert all(w // r == wr for w, r in zip(DIL_WINDOWS, DIL_RATES))
    sup = wr * max(DIL_RATES)
    t = st.t
    assert t % sup == 0
    nsup = t // sup
    kind_cols = ng * A_HEADS
    in_specs, args = [], []
    for g, rate in enumerate(DIL_RATES):
        prev = wr * rate
        ratio = sup // prev
        qcol = lambda h, g=g: g * A_HEADS + h
        cur = lambda kind, g=g: pl.BlockSpec(
            (sup, HEAD_DIM), lambda b, i, h: (b * nsup + i, kind * kind_cols + g * A_HEADS + h))
        prv = lambda kind, g=g, ratio=ratio, prev=prev: pl.BlockSpec(
            (prev, HEAD_DIM),
            lambda b, i, h: (jnp.maximum((b * nsup + i) * ratio - 1, 0), kind * kind_cols + g * A_HEADS + h))
        in_specs += [cur(0), prv(1), cur(1), prv(2), cur(2)]
        args += [qkv] * 5
    max_prev = wr * max(DIL_RATES)
    return pl.pallas_call(
        functools.partial(_dil_prompt_kernel, wr=wr, rates=DIL_RATES, sup=sup),
        out_shape=jax.ShapeDtypeStruct((st.r, A_HEADS * HEAD_DIM), F32),
        grid=(st.b, nsup, A_HEADS),
        in_specs=in_specs,
        out_specs=pl.BlockSpec((sup, HEAD_DIM), lambda b, i, h: (b * nsup + i, h)),
        scratch_shapes=[pltpu.VMEM((max_prev + sup, HEAD_DIM), F32), pltpu.VMEM((max_prev + sup, HEAD_DIM), F32),
                        pltpu.VMEM((sup, HEAD_DIM), F32), pltpu.VMEM((sup, HEAD_DIM), F32),
                        pltpu.VMEM((sup, HEAD_DIM), F32)],
        compiler_params=_params(("arbitrary", "arbitrary", "arbitrary")),
        name="dilated_prompt",
    )(*args)


def _dil_sample_kernel(*refs, wr, rates, windows):
    ng = len(rates)
    o_ref = refs[5 * ng]
    scale = HEAD_DIM ** -0.5
    m_run = l_run = acc_run = None
    for g, (rate, win) in enumerate(zip(rates, windows)):
        q_ref, kn_ref, vn_ref, ks_ref, vs_ref = refs[5 * g:5 * g + 5]
        q = q_ref[...]
        tq = q.shape[0]
        lb = ks_ref.shape[0]
        s1 = _dot_nt(q, ks_ref[...]) * scale
        d1 = lb + lax.broadcasted_iota(I32, (tq, lb), 0) - lax.broadcasted_iota(I32, (tq, lb), 1)
        ok1 = (d1 % rate == 0) & (d1 <= win)
        s1 = jnp.where(ok1, s1, NEG)
        s2 = _dot_nt(q, kn_ref[...]) * scale
        d2 = lax.broadcasted_iota(I32, (tq, tq), 0) - lax.broadcasted_iota(I32, (tq, tq), 1)
        ok2 = (d2 >= 0) & (d2 % rate == 0) & (d2 <= win)
        s2 = jnp.where(ok2, s2, NEG)
        m = jnp.maximum(jnp.max(s1, axis=-1, keepdims=True), jnp.max(s2, axis=-1, keepdims=True))
        p1 = jnp.exp(s1 - m)
        p2 = jnp.exp(s2 - m)
        l = jnp.sum(p1, axis=-1, keepdims=True) + jnp.sum(p2, axis=-1, keepdims=True)
        acc = _dot(p1, vs_ref[...]) + _dot(p2, vn_ref[...])
        if g == 0:
            m_run, l_run, acc_run = m, l, acc
        else:
            m_new = jnp.maximum(m_run, m)
            a = jnp.exp(m_run - m_new)
            b = jnp.exp(m - m_new)
            l_run = a * l_run + b * l
            acc_run = a * acc_run + b * acc
            m_run = m_new
    o_ref[...] = acc_run / l_run


def dilated_sample(st, qkv, bufs, layer):
    ng = len(DIL_RATES)
    wr = DIL_WINDOWS[0] // DIL_RATES[0]
    t = st.t
    kind_cols = ng * A_HEADS
    in_specs, args = [], []
    for g in range(ng):
        buf = bufs[g]
        lb = buf.shape[2]
        assert lb == DIL_WINDOWS[g]
        buf2 = buf.reshape(buf.shape[0], buf.shape[1], lb, 2 * A_HEADS * HEAD_DIM)
        new = lambda kind, g=g: pl.BlockSpec((t, HEAD_DIM), lambda b, h: (b, kind * kind_cols + g * A_HEADS + h))
        old = lambda kv: pl.BlockSpec((None, None, lb, HEAD_DIM), lambda b, h: (layer, b, 0, kv * A_HEADS + h))
        in_specs += [new(0), new(1), new(2), old(0), old(1)]
        args += [qkv, qkv, qkv, buf2, buf2]
    return pl.pallas_call(
        functools.partial(_dil_sample_kernel, wr=wr, rates=DIL_RATES, windows=DIL_WINDOWS),
        out_shape=jax.ShapeDtypeStruct((st.r, A_HEADS * HEAD_DIM), F32),
        grid=(st.b, A_HEADS),
        in_specs=in_specs,
        out_specs=pl.BlockSpec((t, HEAD_DIM), lambda b, h: (b, h)),
        compiler_params=_params(("arbitrary", "arbitrary")),
        name="dilated_sample",
    )(*args)


def _roll_kernel(*refs, n, shifts):
    states, news, outs, sem = refs[:n], refs[n:2 * n], refs[2 * n:3 * n], refs[3 * n]
    l, b = pl.program_id(0), pl.program_id(1)
    copies = []
    for state, new, out, (drop, kept_old, kept_new) in zip(states, news, outs, shifts):
        if kept_old:
            copies.append(pltpu.make_async_copy(state.at[l, b, pl.ds(drop, kept_old)],
                                                out.at[l, b, pl.ds(0, kept_old)], sem))
        t_new = new.shape[2]
        copies.append(pltpu.make_async_copy(new.at[l, b, pl.ds(t_new - kept_new, kept_new)],
                                            out.at[l, b, pl.ds(kept_old, kept_new)], sem))
    for c in copies:
        c.start()
    for c in copies:
        c.wait()


def roll_states(states, news, windows):
    n = len(states)
    shifts, out_shapes = [], []
    for state, new, win in zip(states, news, windows):
        lb, t = state.shape[2], new.shape[2]
        keep = min(win, lb + t)
        kept_new = min(keep, t)
        kept_old = keep - kept_new
        shifts.append((lb - kept_old, kept_old, kept_new))
        out_shapes.append(jax.ShapeDtypeStruct(state.shape[:2] + (keep,) + state.shape[3:], state.dtype))
    any_spec = pl.BlockSpec(memory_space=pl.ANY)
    return pl.pallas_call(
        functools.partial(_roll_kernel, n=n, shifts=tuple(shifts)),
        out_shape=tuple(out_shapes),
        grid=states[0].shape[:2],
        in_specs=[any_spec] * (2 * n),
        out_specs=tuple([any_spec] * n),
        scratch_shapes=[pltpu.SemaphoreType.DMA(())],
        compiler_params=_params(("arbitrary", "arbitrary"), has_side_effects=True),
        name="roll_states",
    )(*states, *news)


def qkv_a(st, x, g, mods, w_bf, qn, kn):
    n = w_bf.shape[1]
    per_kind = n // 3
    tn = _pick_tile(per_kind, 512)
    kind = np.arange(n // tn) // (per_kind // tn)
    return mod_linear(st, x, g, mods, 0, 1, w_bf, tn=tn, name="qkv_a", flags=(kind < 2).astype(np.int32),
                      gidx=np.minimum(kind, 1).astype(np.int32), gains=jnp.stack([qn, kn]))


META_E, META_W, META_RANK = 0, TOP_K, 2 * TOP_K


def _route_kernel(x_ref, g_ref, sh_ref, sc_ref, rw_ref, rb_ref, cin_ref, h_ref, meta_ref, cnt_ref, carry):
    i = pl.program_id(0)

    @pl.when(i == 0)
    def _():
        carry[...] = cin_ref[...]

    h = _modulate(x_ref[...], g_ref[...], sh_ref[...], sc_ref[...])
    h_ref[...] = h
    tm = h.shape[0]
    logits = _dot(h, rw_ref[...]) + rb_ref[...]
    lane = lax.broadcasted_iota(I32, (tm, LANES), 1)
    work = logits
    sel = jnp.zeros((tm, LANES), F32)
    idxs, vals = [], []
    for _ in range(TOP_K):
        v = jnp.max(work, axis=-1, keepdims=True)
        idx = jnp.min(jnp.where(work == v, lane, LANES), axis=-1, keepdims=True)
        hit = lane == idx
        sel = jnp.where(hit, 1.0, sel)
        work = jnp.where(hit, -jnp.inf, work)
        idxs.append(idx)
        vals.append(v)
    es = [jnp.exp(v - vals[0]) for v in vals]
    den = es[0] + es[1] + es[2] + es[3]
    rr = lax.broadcasted_iota(I32, (tm, tm), 0)
    cc = lax.broadcasted_iota(I32, (tm, tm), 1)
    tri = jnp.where(cc < rr, 1.0, 0.0)
    rank = carry[...] + _dot(tri, sel)
    meta = jnp.zeros((tm, LANES), F32)
    for k in range(TOP_K):
        rk = jnp.sum(jnp.where(lane == idxs[k], rank, 0.0), axis=-1, keepdims=True)
        meta = jnp.where(lane == META_E + k, idxs[k].astype(F32), meta)
        meta = jnp.where(lane == META_W + k, es[k] / den, meta)
        meta = jnp.where(lane == META_RANK + k, rk, meta)
    meta_ref[...] = meta
    carry[...] = carry[...] + jnp.sum(sel, axis=0, keepdims=True)
    cnt_ref[...] = carry[...]


def moe_route(st, x, g, mods, rw_bf, rb, counts_in):
    r, d = x.shape
    tm = st.tm
    return pl.pallas_call(
        _route_kernel,
        out_shape=(jax.ShapeDtypeStruct((r, d), F32), jax.ShapeDtypeStruct((r, LANES), F32),
                   jax.ShapeDtypeStruct((1, LANES), F32)),
        grid=(r // tm,),
        in_specs=[pl.BlockSpec((tm, d), lambda i: (i, 0)),
                  pl.BlockSpec((1, d), lambda i: (0, 0)),
                  st.mod_spec(d, 3), st.mod_spec(d, 4),
                  pl.BlockSpec((d, LANES), lambda i: (0, 0)),
                  pl.BlockSpec((1, LANES), lambda i: (0, 0)),
                  pl.BlockSpec((1, LANES), lambda i: (0, 0))],
        out_specs=(pl.BlockSpec((tm, d), lambda i: (i, 0)),
                   pl.BlockSpec((tm, LANES), lambda i: (i, 0)),
                   pl.BlockSpec((1, LANES), lambda i: (0, 0))),
        scratch_shapes=[pltpu.VMEM((1, LANES), F32)],
        compiler_params=_params(("arbitrary",)),
        name="moe_route",
    )(x, g.reshape(1, d), mods, mods, rw_bf, rb, counts_in)


def _dispatch_kernel(dest_ref, pend_ref, *refs, tms, tiles, n_exp, n_rows):
    h_refs = refs[:len(tms)]
    xb_out, zero_tile, sem = refs[len(tms):]
    i = pl.program_id(0)

    @pl.when(i == 0)
    def _():
        zero_tile[...] = jnp.zeros(zero_tile.shape, zero_tile.dtype)

        def clear(start):
            return pltpu.make_async_copy(
                zero_tile, xb_out.at[pl.ds(pl.multiple_of(start, MOE_BLOCK), MOE_BLOCK)], sem)

        total = pend_ref[n_exp - 1]
        for phase in ("start", "wait"):
            for e in range(n_exp):
                prev = pend_ref[e - 1] if e else 0

                @pl.when(pend_ref[e] > prev)
                def _(e=e):
                    getattr(clear(pend_ref[e] - MOE_BLOCK), phase)()

            def tail(j, carry, phase=phase):
                getattr(clear(total + j * MOE_BLOCK), phase)()
                return carry

            lax.fori_loop(0, (n_rows - total) // MOE_BLOCK, tail, 0)

    first_tile = first_row = 0
    for h_ref, tm, nt in zip(h_refs, tms, tiles):
        @pl.when((i >= first_tile) & (i < first_tile + nt))
        def _(h_ref=h_ref, tm=tm, first_tile=first_tile, first_row=first_row):
            row0 = first_row + (i - first_tile) * tm

            def row(t, carry):
                for k in range(TOP_K):
                    dst = dest_ref[(row0 + t) * TOP_K + k]
                    pltpu.make_async_copy(h_ref.at[pl.ds(t, 1)], xb_out.at[pl.ds(dst, 1)], sem).start()
                return carry

            lax.fori_loop(0, tm, row, 0)

            def drain(t, carry):
                for k in range(TOP_K):
                    pltpu.make_async_copy(h_ref.at[pl.ds(0, 1)], xb_out.at[pl.ds(0, 1)], sem).wait()
                return carry

            lax.fori_loop(0, tm, drain, 0)

        first_tile += nt
        first_row += nt * tm


def moe_dispatch(streams, hs, dest, pend, n_rows):
    d = hs[0].shape[1]
    tms = tuple(st.tm for st in streams)
    tiles = tuple(st.r // st.tm for st in streams)
    firsts = tuple(int(v) for v in np.cumsum((0,) + tiles[:-1]))

    def h_spec(tm, first, nt):
        return pl.BlockSpec((tm, d), lambda i, *_: (jnp.clip(i - first, 0, nt - 1), 0))

    grid_spec = pltpu.PrefetchScalarGridSpec(
        num_scalar_prefetch=2, grid=(sum(tiles),),
        in_specs=[h_spec(tm, first, nt) for tm, first, nt in zip(tms, firsts, tiles)],
        out_specs=pl.BlockSpec(memory_space=pl.ANY),
        scratch_shapes=[pltpu.VMEM((MOE_BLOCK, d), F32), pltpu.SemaphoreType.DMA(())])
    return pl.pallas_call(
        functools.partial(_dispatch_kernel, tms=tms, tiles=tiles, n_exp=pend.shape[0], n_rows=n_rows),
        out_shape=jax.ShapeDtypeStruct((n_rows, d), F32), grid_spec=grid_spec,
        compiler_params=_params(("arbitrary",), has_side_effects=True), name="moe_dispatch",
    )(dest, pend, *hs)


def _ffn_kernel(e_ref, row_ref, nsub_ref, xb_hbm, wgu_ref, bgu_ref, wdn_ref, bdn_ref, perm_ref, yb_in, yb_hbm,
                xs, xsb, acc, wg_bf, wd_bf, sem_in, sem_out, *, tb):
    del e_ref, yb_in
    w = pl.program_id(0)
    f = pl.program_id(1)
    nw = pl.num_programs(0)
    nf = pl.num_programs(1)
    nsub = nsub_ref[w]
    live = nsub > 0
    row0 = pl.multiple_of(row_ref[w], MOE_BLOCK)

    def x_copy(item):
        return pltpu.make_async_copy(xb_hbm.at[pl.ds(pl.multiple_of(row_ref[item], MOE_BLOCK), tb)], xs, sem_in)

    @pl.when((w == 0) & (f == 0) & live)
    def _():
        x_copy(0).start()

    @pl.when((f == 0) & live)
    def _():
        x_copy(w).wait()
        xsb[...] = xs[...].astype(BF16)
        acc[...] = jnp.broadcast_to(bdn_ref[...], acc.shape)

    nxt = jnp.minimum(w + 1, nw - 1)

    @pl.when((f == nf - 1) & (w + 1 < nw) & (nsub_ref[nxt] > 0))
    def _():
        x_copy(nxt).start()

    @pl.when(live)
    def _():
        wg_bf[...] = wgu_ref[...].astype(BF16)
        wd_bf[...] = wdn_ref[...].astype(BF16)
        gu = jnp.dot(xsb[...], wg_bf[...], preferred_element_type=F32) + bgu_ref[...]
        lane = lax.broadcasted_iota(I32, gu.shape, 1)
        gate = jnp.minimum(gu, SWIGLU_LIMIT)
        glu = gate * jax.nn.sigmoid(SWIGLU_ALPHA * gate)
        up = jnp.clip(gu, -SWIGLU_LIMIT, SWIGLU_LIMIT) + 1.0
        prod = jnp.where((lane % 2) == 1, up * pltpu.roll(glu, 1, 1), 0.0).astype(BF16)
        act = jnp.dot(prod, perm_ref[...], preferred_element_type=F32).astype(BF16)
        acc[...] += jnp.dot(act, wd_bf[...], preferred_element_type=F32)

    @pl.when((f == nf - 1) & live)
    def _():
        def copy(s):
            return pltpu.make_async_copy(acc.at[pl.ds(s * MOE_BLOCK, MOE_BLOCK)],
                                         yb_hbm.at[pl.ds(row0 + s * MOE_BLOCK, MOE_BLOCK)], sem_out)
        for s in range(tb // MOE_BLOCK):
            @pl.when(s < nsub)
            def _(s=s):
                copy(s).start()
        for s in range(tb // MOE_BLOCK):
            @pl.when(s < nsub)
            def _(s=s):
                copy(s).wait()


def moe_ffn(xb, yb, item_e, item_row, item_nsub, w_gu, b_gu, w_dn, b_dn, layer, *, tb, tf):
    n_layers, n_exp, d, f2 = w_gu.shape
    dff = f2 // 2
    nf = dff // tf
    n_items = item_e.shape[0]
    perm = np.zeros((2 * tf, tf), np.float32)
    perm[2 * np.arange(tf) + 1, np.arange(tf)] = 1.0

    def fsel(w, f, ns):
        return jnp.where(ns[w] > 0, f, nf - 1)

    grid_spec = pltpu.PrefetchScalarGridSpec(
        num_scalar_prefetch=3,
        grid=(n_items, nf),
        in_specs=[pl.BlockSpec(memory_space=pl.ANY),
                  pl.BlockSpec((None, None, d, 2 * tf), lambda w, f, e, r_, ns: (layer, e[w], 0, fsel(w, f, ns))),
                  pl.BlockSpec((None, None, 1, 2 * tf), lambda w, f, e, r_, ns: (layer, e[w], 0, fsel(w, f, ns))),
                  pl.BlockSpec((None, None, tf, d), lambda w, f, e, r_, ns: (layer, e[w], fsel(w, f, ns), 0)),
                  pl.BlockSpec((None, None, 1, d), lambda w, f, e, *_: (layer, e[w], 0, 0)),
                  pl.BlockSpec((2 * tf, tf), lambda w, f, *_: (0, 0)),
                  pl.BlockSpec(memory_space=pl.ANY)],
        out_specs=pl.BlockSpec(memory_space=pl.ANY),
        scratch_shapes=[pltpu.VMEM((tb, d), F32), pltpu.VMEM((tb, d), BF16), pltpu.VMEM((tb, d), F32),
                        pltpu.VMEM((d, 2 * tf), BF16), pltpu.VMEM((tf, d), BF16),
                        pltpu.SemaphoreType.DMA(()), pltpu.SemaphoreType.DMA(())])
    return pl.pallas_call(
        functools.partial(_ffn_kernel, tb=tb),
        out_shape=jax.ShapeDtypeStruct(yb.shape, F32),
        grid_spec=grid_spec,
        input_output_aliases={9: 0},
        compiler_params=_params(("arbitrary", "arbitrary"), has_side_effects=True),
        name="moe_ffn",
    )(item_e, item_row, item_nsub, xb, w_gu, b_gu.reshape(n_layers, n_exp, 1, f2), w_dn,
      b_dn.reshape(n_layers, n_exp, 1, d), jnp.asarray(perm, BF16), yb)


def _combine_kernel(dest_ref, yb_hbm, meta_ref, x_ref, gate_ref, o_ref, ybuf, sem, *, tm):
    i = pl.program_id(0)

    def row(t, carry):
        for k in range(TOP_K):
            src = dest_ref[(i * tm + t) * TOP_K + k]
            pltpu.make_async_copy(yb_hbm.at[pl.ds(src, 1)], ybuf.at[k, pl.ds(t, 1)], sem).start()
        return carry

    lax.fori_loop(0, tm, row, 0)

    def drain(t, carry):
        for k in range(TOP_K):
            pltpu.make_async_copy(yb_hbm.at[pl.ds(0, 1)], ybuf.at[0, pl.ds(0, 1)], sem).wait()
        return carry

    lax.fori_loop(0, tm, drain, 0)
    meta = meta_ref[...]
    y = jnp.zeros(x_ref.shape, F32)
    for k in range(TOP_K):
        y = y + meta[:, META_W + k:META_W + k + 1] * ybuf[k]
    o_ref[...] = x_ref[...] + gate_ref[...] * y


def moe_combine(st, yb, dest, meta, x, mods):
    r, d = x.shape
    tm = min(st.tm, 256)
    tps = st.tiles_per_seq * (st.tm // tm)
    rows = st.r if st.per_row_mods else 1
    grid_spec = pltpu.PrefetchScalarGridSpec(
        num_scalar_prefetch=1,
        grid=(r // tm,),
        in_specs=[pl.BlockSpec(memory_space=pl.ANY),
                  pl.BlockSpec((tm, LANES), lambda i, *_: (i, 0)),
                  pl.BlockSpec((tm, d), lambda i, *_: (i, 0)),
                  pl.BlockSpec((None, rows, d), lambda i, *_: (i // tps, 0, 5))],
        out_specs=pl.BlockSpec((tm, d), lambda i, *_: (i, 0)),
        scratch_shapes=[pltpu.VMEM((TOP_K, tm, d), F32), pltpu.SemaphoreType.DMA(())])
    return pl.pallas_call(
        functools.partial(_combine_kernel, tm=tm),
        out_shape=jax.ShapeDtypeStruct((r, d), F32),
        grid_spec=grid_spec,
        compiler_params=_params(("arbitrary",)),
        name="moe_combine",
    )(dest, yb, meta, x, mods)


def moe_layer(streams, xs, g, mods_list, router_w, router_b, w_gu, b_gu, w_dn, b_dn, layer):
    d = xs[0].shape[1]
    n_exp = router_w.shape[1]
    rw = jnp.zeros((d, LANES), BF16).at[:, :n_exp].set(router_w.astype(BF16))
    rb = jnp.full((1, LANES), NEG, F32).at[0, :n_exp].set(router_b)
    counts = jnp.zeros((1, LANES), F32)
    hs, metas = [], []
    for st, x, mods in zip(streams, xs, mods_list):
        h, meta, counts = moe_route(st, x, g, mods, rw, rb, counts)
        hs.append(h)
        metas.append(meta)
    n_asg = sum(st.r for st in streams) * TOP_K
    tb = MOE_TB
    cnt = counts[0, :n_exp].astype(I32)
    padded = (cnt + MOE_BLOCK - 1) // MOE_BLOCK * MOE_BLOCK
    pend = jnp.cumsum(padded)
    pstart = pend - padded
    n_items = (n_asg + n_exp * (MOE_BLOCK - 1)) // tb + n_exp + 1
    per_e = (padded + tb - 1) // tb
    iend = jnp.cumsum(per_e)
    total = iend[-1]
    wi = jnp.arange(n_items, dtype=I32)
    live = wi < total
    e_of = jnp.minimum(jnp.sum(wi[:, None] >= iend[None, :], axis=1), n_exp - 1).astype(I32)
    last_e = jnp.minimum(jnp.sum(total - 1 >= iend), n_exp - 1).astype(I32)
    k_of = wi - (iend - per_e)[e_of]
    item_e = jnp.where(live, e_of, last_e).astype(I32)
    item_row = jnp.where(live, pstart[e_of] + k_of * tb, 0).astype(I32)
    item_nsub = jnp.where(live, jnp.minimum(tb, padded[e_of] - k_of * tb) // MOE_BLOCK, 0).astype(I32)
    n_rows = n_asg + n_exp * (MOE_BLOCK - 1) + tb
    n_rows = -(-n_rows // MOE_BLOCK) * MOE_BLOCK
    dests = []
    for meta in metas:
        e = meta[:, META_E:META_E + TOP_K].astype(I32)
        rank = meta[:, META_RANK:META_RANK + TOP_K].astype(I32)
        dests.append((pstart[e] + rank).reshape(-1).astype(I32))
    xb = moe_dispatch(streams, hs, jnp.concatenate(dests), pend.astype(I32), n_rows)
    tf = _pick_tile(w_dn.shape[2], 256)
    yb = moe_ffn(xb, jnp.zeros((n_rows - tb, d), F32), item_e, item_row, item_nsub, w_gu, b_gu, w_dn, b_dn, layer,
                 tb=tb, tf=tf)
    return [moe_combine(st, yb, dest, meta, x, mods)
            for st, dest, meta, x, mods in zip(streams, dests, metas, xs, mods_list)]


def kv_proj(st, x, g, mods, w_bf, kn_kv):
    n = w_bf.shape[1]
    tn = B_KV_HEADS * HEAD_DIM
    j = np.arange(n // tn)
    return mod_linear(st, x, g, mods, 0, 1, w_bf, tn=tn, name="kv_proj", flags=(j % 2 == 0).astype(np.int32),
                      gidx=(j // 2).astype(np.int32), gains=kn_kv)


def q_proj_b(st, x, g, mods, wq_bf, qn, w_gate, b_gate):
    d = x.shape[1]
    n = wq_bf.shape[1]
    nj = n // 512
    q = mod_linear(st, x, g, mods, 0, 1, wq_bf, tn=512, name="q_proj_b", flags=np.ones((nj,), np.int32),
                   gidx=np.zeros((nj,), np.int32), gains=qn)
    per_g = B_HEADS_PER_KV * 3
    wg = jnp.zeros((d, B_KV_HEADS, LANES), F32).at[:, :, :per_g].set(w_gate.reshape(d, B_KV_HEADS, per_g))
    bg = jnp.zeros((B_KV_HEADS, LANES), F32).at[:, :per_g].set(b_gate.reshape(B_KV_HEADS, per_g))
    gates = mod_linear(st, x, g, mods, 0, 1, wg.reshape(d, -1).astype(BF16), tn=B_KV_HEADS * LANES,
                       name="gates_b", bias=bg.reshape(-1), act="sigmoid")
    return q, gates


def _cmp1_kernel(*refs, n_pages, cpp):
    page_refs = refs[:n_pages]
    pe_ref, w1_ref, p_ref, q_ref, rows_scr = refs[n_pages:]
    ch = n_pages * cpp
    rpr = cpp * CMP_STRIDE
    half_len = CMP_LEN // 2
    for j, pr in enumerate(page_refs):
        for c in range(2 * B_KV_HEADS):
            rows_scr[c, j * rpr:(j + 1) * rpr, :] = pr[:, c * HEAD_DIM:(c + 1) * HEAD_DIM]
    for kind in range(2):
        for half, out in ((0, p_ref), (1, q_ref)):
            acc = jnp.zeros((B_KV_HEADS * ch, w1_ref.shape[-1]), F32)
            for l in range(half_len):
                pieces = [rows_scr[kind * B_KV_HEADS + g, pl.ds(l, ch, stride=CMP_STRIDE), :]
                          for g in range(B_KV_HEADS)]
                xl = jnp.concatenate(pieces, axis=0) + pe_ref[kind, pl.ds(half * half_len + l, 1), :]
                acc = acc + _dot(xl, w1_ref[kind, half * half_len + l])
            for g in range(B_KV_HEADS):
                out[kind, g] = acc[g * ch:(g + 1) * ch]


def compress_partials(rows_arr, spec_fn, n_refs, rows_per_ref, grid, pe, w1_bf, n_chunks, prefetch=None):
    assert CMP_LEN == 2 * CMP_STRIDE
    cpp = rows_per_ref // CMP_STRIDE
    ch = n_refs * cpp
    hid = w1_bf.shape[-1]
    nb = grid[0]
    out_shape = jax.ShapeDtypeStruct((nb, 2, B_KV_HEADS, n_chunks, hid), F32)
    out_spec = pl.BlockSpec((None, 2, B_KV_HEADS, ch, hid), lambda b, i, *_: (b, 0, 0, i, 0))
    in_specs = [spec_fn(j) for j in range(n_refs)] + [
        pl.BlockSpec(pe.shape, lambda b, i, *_: (0, 0, 0)),
        pl.BlockSpec(w1_bf.shape, lambda b, i, *_: (0, 0, 0, 0))]
    kern = functools.partial(_cmp1_kernel, n_pages=n_refs, cpp=cpp)
    args = [rows_arr] * n_refs + [pe, w1_bf]
    scratch = [pltpu.VMEM((2 * B_KV_HEADS, ch * CMP_STRIDE, HEAD_DIM), F32)]
    if prefetch is None:
        return pl.pallas_call(kern, out_shape=(out_shape, out_shape), grid=grid, in_specs=in_specs,
                              out_specs=(out_spec, out_spec), scratch_shapes=scratch,
                              compiler_params=_params(("arbitrary", "arbitrary")), name="cmp_partials")(*args)
    gs = pltpu.PrefetchScalarGridSpec(num_scalar_prefetch=1, grid=grid, in_specs=in_specs,
                                      out_specs=(out_spec, out_spec), scratch_shapes=scratch)
    return pl.pallas_call(lambda pt, *r: kern(*r), out_shape=(out_shape, out_shape), grid_spec=gs,
                          compiler_params=_params(("arbitrary", "arbitrary")), name="cmp_partials_paged")(prefetch, *args)


def _cmp2_kernel(p_ref, q_ref, b1_ref, w2_ref, b2_ref, o_ref):
    q = q_ref[...]
    qs = jnp.concatenate([q[1:], jnp.zeros((1, q.shape[1]), F32)], axis=0)
    hid = jax.nn.gelu(p_ref[...] + qs + b1_ref[...])
    o_ref[...] = _dot(hid, w2_ref[...]) + b2_ref[...]


def compress_finish(p, q, b1, w2_bf, b2):
    nb, _, ng, nc, hid = p.shape
    blk = pl.BlockSpec((None, None, None, nc, hid), lambda b, k, g: (b, k, g, 0, 0))
    return pl.pallas_call(
        _cmp2_kernel,
        out_shape=jax.ShapeDtypeStruct((nb, 2, ng, nc, HEAD_DIM), F32),
        grid=(nb, 2, ng),
        in_specs=[blk, blk,
                  pl.BlockSpec((None, 1, hid), lambda b, k, g: (k, 0, 0)),
                  pl.BlockSpec((None, hid, HEAD_DIM), lambda b, k, g: (k, 0, 0)),
                  pl.BlockSpec((None, 1, HEAD_DIM), lambda b, k, g: (k, 0, 0))],
        out_specs=pl.BlockSpec((None, None, None, nc, HEAD_DIM), lambda b, k, g: (b, k, g, 0, 0)),
        compiler_params=_params(("arbitrary", "arbitrary", "arbitrary")),
        name="cmp_finish",
    )(p, q, b1.reshape(2, 1, hid), w2_bf, b2.reshape(2, 1, HEAD_DIM))


def _selection_map(n_cmp_pad, n_cmp, n_slc, n_slc_pad):
    r_s = SLC_LEN // CMP_STRIDE
    r_c = CMP_LEN // CMP_STRIDE
    mult = np.zeros(r_s + r_c - 1, np.float32)
    for m in range(r_s):
        for n in range(r_c):
            mult[m + n] += 1.0
    off = r_s * np.arange(n_slc)[None, :] - np.arange(n_cmp)[:, None]
    ok = (off >= 0) & (off < mult.shape[0])
    out = np.zeros((n_cmp_pad, n_slc_pad), np.float32)
    out[:n_cmp, :n_slc] = np.where(ok, mult[np.clip(off, 0, mult.shape[0] - 1)], 0.0)
    return out


def _stack_heads(q):
    return jnp.concatenate([q[:, h * HEAD_DIM:(h + 1) * HEAD_DIM] for h in range(q.shape[1] // HEAD_DIM)], axis=0)


def _rep_rows(x, n):
    return jnp.concatenate([x] * n, axis=0)


def _cmp_and_select(qs, pos, kc, vc, selmap, *, n_cmp, n_slc, n_sel):
    tq = pos.shape[0]
    hpk = qs.shape[0] // tq
    pos_h = _rep_rows(pos, hpk)
    ncp = kc.shape[0]
    s = _dot_nt(qs, kc) * (HEAD_DIM ** -0.5)
    ci = lax.broadcasted_iota(I32, (hpk * tq, ncp), 1)
    vis = (ci * CMP_STRIDE + (CMP_LEN - 1) <= pos_h) & (ci < n_cmp)
    s = jnp.where(vis, s, NEG)
    e = jnp.where(vis, jnp.exp(s - jnp.max(s, axis=-1, keepdims=True)), 0.0)
    p = e / jnp.maximum(jnp.sum(e, axis=-1, keepdims=True), 1e-30)
    o_cmp = _dot(p, vc)
    pi = _dot(p, selmap)
    imp = pi[0:tq]
    for h in range(1, hpk):
        imp = imp + pi[h * tq:(h + 1) * tq]
    ns = imp.shape[1]
    blk = lax.broadcasted_iota(I32, (tq, ns), 1)
    cur = pos // SLC_LEN
    forced = (blk == 0) | (blk == cur) | (blk == cur - 1)
    valid = blk * SLC_LEN <= pos
    score = jnp.where(valid, imp + jnp.where(forced, FORCE_BONUS, 0.0), -FORCE_BONUS)
    score = jnp.where(blk < n_slc, score, -jnp.inf)

    half = LANES // 2
    if ns == LANES and n_slc <= half and tq % 8 == 0:
        both = jnp.where(blk < half, score, pltpu.roll(score, half, 1))
        j8 = lax.broadcasted_iota(I32, (8, LANES), 1) % half
        rank = jnp.zeros((tq, ns), F32)
        for dd in range(1, half):
            other = pltpu.roll(both, dd, 1)
            tie = _rep_rows(jnp.where(j8 >= dd, 1.0, 0.0), tq // 8)
            rank = rank + jnp.where(other > both, 1.0, 0.0) + jnp.where(other == both, tie, 0.0)
        sel = jnp.where(blk < n_slc, jnp.where(rank < n_sel, 1.0, 0.0), 0.0)
        return o_cmp, sel

    def pick(_, carry):
        sel, work = carry
        v = jnp.max(work, axis=-1, keepdims=True)
        idx = jnp.min(jnp.where(work == v, blk, ns), axis=-1, keepdims=True)
        hit = blk == idx
        return jnp.where(hit, 1.0, sel), jnp.where(hit, -jnp.inf, work)

    sel, _ = lax.fori_loop(0, n_sel, pick, (jnp.zeros((tq, ns), F32), score))
    return o_cmp, sel


def _flash_tiles(qs, k_ref, v_ref, lo, hi, tk, hpk, bias_fn):
    rows = qs.shape[0]
    tq = rows // hpk

    def body(kt, carry):
        m, l, acc = carry
        start = pl.multiple_of(kt * tk, tk)
        k = k_ref[pl.ds(start, tk), :]
        v = v_ref[pl.ds(start, tk), :]
        kpos = kt * tk + lax.broadcasted_iota(I32, (tq, tk), 1)
        s = _dot_nt(qs, k) * (HEAD_DIM ** -0.5) + _rep_rows(bias_fn(kt, kpos), hpk)
        m_new = jnp.maximum(m, jnp.max(s, axis=-1, keepdims=True))
        alpha = jnp.exp(m - m_new)
        p = jnp.exp(s - m_new)
        return m_new, alpha * l + jnp.sum(p, axis=-1, keepdims=True), alpha * acc + _dot(p, v)

    init = (jnp.full((rows, 1), NEG, F32), jnp.zeros((rows, 1), F32), jnp.zeros((rows, HEAD_DIM), F32))
    m, l, acc = lax.fori_loop(lo, hi, body, init)
    return acc / jnp.maximum(l, 1e-30)


def _gate_cols(gt, branch, hpk):
    return jnp.concatenate([gt[:, h * 3 + branch:h * 3 + branch + 1] for h in range(hpk)], axis=0)


def _nsa_prompt_kernel(q_ref, gate_ref, kc_ref, vc_ref, ks_ref, vs_ref, kw_ref, vw_ref, selmap_ref, expand_ref,
                       o_ref, *, tq, tk, wspan, n_cmp, n_slc, n_sel):
    qi = pl.program_id(2)
    hpk = B_HEADS_PER_KV
    qs = _stack_heads(q_ref[...]).astype(BF16)
    pos = qi * tq + lax.broadcasted_iota(I32, (tq, 1), 0)
    o_cmp, sel = _cmp_and_select(qs, pos, kc_ref[...], vc_ref[...], selmap_ref[...],
                                 n_cmp=n_cmp, n_slc=n_slc, n_sel=n_sel)
    sel_bf = sel.astype(BF16)
    hi = (qi * tq + tq - 1) // tk + 1

    def bias_slc(kt, kpos):
        picked = jnp.dot(sel_bf, expand_ref[kt], preferred_element_type=F32)
        return jnp.where(kpos <= pos, (picked - 1.0) * (-NEG), NEG)

    o_slc = _flash_tiles(qs, ks_ref, vs_ref, 0, hi, tk, hpk, bias_slc)

    start = pl.multiple_of(jnp.maximum(qi * tq + tq - wspan, 0), 8)
    kpos = start + lax.broadcasted_iota(I32, (tq, wspan), 1)
    rel = pos - kpos
    bias_w = _rep_rows(jnp.where(rel >= 0, jnp.where(rel < B_WINDOW, 0.0, NEG), NEG), hpk)
    s = _dot_nt(qs, kw_ref[pl.ds(start, wspan), :]) * (HEAD_DIM ** -0.5) + bias_w
    p = jnp.exp(s - jnp.max(s, axis=-1, keepdims=True))
    o_win = _dot(p, vw_ref[pl.ds(start, wspan), :]) / jnp.sum(p, axis=-1, keepdims=True)

    gt = gate_ref[...]
    o = _gate_cols(gt, 0, hpk) * o_cmp + _gate_cols(gt, 1, hpk) * o_slc + _gate_cols(gt, 2, hpk) * o_win
    for h in range(hpk):
        o_ref[:, h * HEAD_DIM:(h + 1) * HEAD_DIM] = o[h * tq:(h + 1) * tq]


def nsa_prompt(st, q, gates, kvs, cmp_kv):
    t = st.t
    tq = _pick_tile(t, 128, 8)
    tk = _pick_tile(t, 512, 8)
    wspan = min(t, -(-(tq + B_WINDOW - 1) // LANES) * LANES)
    nq = t // tq
    ncp = cmp_kv.shape[3]
    n_cmp = t // CMP_STRIDE - CMP_LEN // CMP_STRIDE + 1
    n_slc = -(-t // SLC_LEN)
    ns = -(-n_slc // LANES) * LANES
    selmap = jnp.asarray(_selection_map(ncp, n_cmp, n_slc, ns), BF16)
    expand = np.zeros((t // tk, ns, tk), np.float32)
    kk = np.arange(t)
    expand[kk // tk, kk // SLC_LEN, kk % tk] = 1.0
    gw = B_HEADS_PER_KV * HEAD_DIM
    kv_blk = lambda col: pl.BlockSpec((t, HEAD_DIM), lambda b, g, i: (b, col + g))
    cmp_blk = lambda kind: pl.BlockSpec((None, None, None, ncp, HEAD_DIM), lambda b, g, i: (b, kind, g, 0, 0))
    return pl.pallas_call(
        functools.partial(_nsa_prompt_kernel, tq=tq, tk=tk, wspan=wspan, n_cmp=n_cmp, n_slc=n_slc, n_sel=min(N_SELECT, n_slc)),
        out_shape=jax.ShapeDtypeStruct((st.r, B_HEADS * HEAD_DIM), F32),
        grid=(st.b, B_KV_HEADS, nq),
        in_specs=[pl.BlockSpec((tq, gw), lambda b, g, i: (b * nq + i, g)),
                  pl.BlockSpec((tq, LANES), lambda b, g, i: (b * nq + i, g)),
                  cmp_blk(0), cmp_blk(1),
                  kv_blk(2 * B_KV_HEADS), kv_blk(3 * B_KV_HEADS), kv_blk(4 * B_KV_HEADS), kv_blk(5 * B_KV_HEADS),
                  pl.BlockSpec(selmap.shape, lambda b, g, i: (0, 0)),
                  pl.BlockSpec(expand.shape, lambda b, g, i: (0, 0, 0))],
        out_specs=pl.BlockSpec((tq, gw), lambda b, g, i: (b * nq + i, g)),
        compiler_params=_params(("arbitrary", "arbitrary", "arbitrary")),
        name="nsa_prompt",
    )(q, gates, cmp_kv, cmp_kv, kvs, kvs, kvs, kvs, selmap, jnp.asarray(expand, BF16))


def _nsa_s1_kernel(q_ref, kc_ref, vc_ref, selmap_ref, ocmp_ref, sel_ref, *, tq, past_len, n_cmp, n_slc, n_sel):
    qs = _stack_heads(q_ref[...]).astype(BF16)
    pos = past_len + lax.broadcasted_iota(I32, (tq, 1), 0)
    o_cmp, sel = _cmp_and_select(qs, pos, kc_ref[...], vc_ref[...], selmap_ref[...],
                                 n_cmp=n_cmp, n_slc=n_slc, n_sel=n_sel)
    ocmp_ref[...] = o_cmp
    sel_ref[...] = sel


def _nsa_s2_kernel(*refs, tq, past_len, pps):
    q_ref = refs[1]
    page_refs = refs[2:2 + pps]
    sel_ref, new_ref, o_ref, m_scr, l_scr, acc_scr = refs[2 + pps:]
    p = pl.program_id(1)
    n_steps = pl.num_programs(1)
    hpk = B_HEADS_PER_KV
    gw = B_KV_HEADS * HEAD_DIM
    scale = HEAD_DIM ** -0.5
    ns = sel_ref.shape[-1]

    @pl.when(p == 0)
    def _():
        m_scr[...] = jnp.full(m_scr.shape, NEG, F32)
        l_scr[...] = jnp.zeros(l_scr.shape, F32)
        acc_scr[...] = jnp.zeros(acc_scr.shape, F32)

    q = q_ref[...]

    def update(g, s, ok, v):
        s = jnp.where(ok, s, NEG)
        m_old = m_scr[g]
        m_new = jnp.maximum(m_old, jnp.max(s, axis=-1, keepdims=True))
        alpha = jnp.exp(m_old - m_new)
        pr = jnp.where(ok, jnp.exp(s - m_new), 0.0)
        l_scr[g] = alpha * l_scr[g] + jnp.sum(pr, axis=-1, keepdims=True)
        acc_scr[g] = alpha * acc_scr[g] + _dot(pr, v)
        m_scr[g] = m_new

    nk = pps * PAGE_SIZE
    nn = lax.broadcasted_iota(I32, (ns, nk), 0)
    jj = lax.broadcasted_iota(I32, (ns, nk), 1)
    expand = jnp.where(nn == p * (nk // SLC_LEN) + jj // SLC_LEN, 1.0, 0.0).astype(BF16)
    for g in range(B_KV_HEADS):
        qs = _stack_heads(q[:, g * hpk * HEAD_DIM:(g + 1) * hpk * HEAD_DIM]).astype(BF16)
        k = jnp.concatenate([pr[:, g * HEAD_DIM:(g + 1) * HEAD_DIM] for pr in page_refs], axis=0)
        v = jnp.concatenate([pr[:, gw + g * HEAD_DIM:gw + (g + 1) * HEAD_DIM] for pr in page_refs], axis=0)
        picked = jnp.dot(sel_ref[g].astype(BF16), expand, preferred_element_type=F32)
        update(g, _dot_nt(qs, k) * scale, _rep_rows(picked, hpk) > 0.5, v)

    @pl.when(p == n_steps - 1)
    def _():
        nn2 = lax.broadcasted_iota(I32, (ns, tq), 0)
        tt2 = lax.broadcasted_iota(I32, (ns, tq), 1)
        expand2 = jnp.where(nn2 == (past_len + tt2) // SLC_LEN, 1.0, 0.0).astype(BF16)
        t_q = _rep_rows(lax.broadcasted_iota(I32, (tq, tq), 0), hpk)
        t_k = _rep_rows(lax.broadcasted_iota(I32, (tq, tq), 1), hpk)
        for g in range(B_KV_HEADS):
            qs = _stack_heads(q[:, g * hpk * HEAD_DIM:(g + 1) * hpk * HEAD_DIM]).astype(BF16)
            k = new_ref[:, g * HEAD_DIM:(g + 1) * HEAD_DIM]
            v = new_ref[:, gw + g * HEAD_DIM:gw + (g + 1) * HEAD_DIM]
            picked = jnp.dot(sel_ref[g].astype(BF16), expand2, preferred_element_type=F32)
            update(g, _dot_nt(qs, k) * scale, (t_k <= t_q) & (_rep_rows(picked, hpk) > 0.5), v)
            o_ref[g] = acc_scr[g] / jnp.maximum(l_scr[g], 1e-30)


def _nsa_s3_kernel(q_ref, gate_ref, win_ref, new_ref, ocmp_ref, oslc_ref, o_ref, *, tq):
    hpk = B_HEADS_PER_KV
    gw = B_KV_HEADS * HEAD_DIM
    scale = HEAD_DIM ** -0.5
    lb = win_ref.shape[0]
    rows = hpk * tq
    q = q_ref[...]
    gt_all = gate_ref[...]
    t_q1 = _rep_rows(lax.broadcasted_iota(I32, (tq, lb), 0), hpk)
    i_k1 = _rep_rows(lax.broadcasted_iota(I32, (tq, lb), 1), hpk)
    ok1 = (t_q1 + lb - i_k1) < B_WINDOW
    t_q2 = _rep_rows(lax.broadcasted_iota(I32, (tq, tq), 0), hpk)
    t_k2 = _rep_rows(lax.broadcasted_iota(I32, (tq, tq), 1), hpk)
    ok2 = t_k2 <= t_q2
    for g in range(B_KV_HEADS):
        qs = _stack_heads(q[:, g * hpk * HEAD_DIM:(g + 1) * hpk * HEAD_DIM]).astype(BF16)
        s1 = jnp.where(ok1, _dot_nt(qs, win_ref[:, g * HEAD_DIM:(g + 1) * HEAD_DIM]) * scale, NEG)
        s2 = jnp.where(ok2, _dot_nt(qs, new_ref[:, g * HEAD_DIM:(g + 1) * HEAD_DIM]) * scale, NEG)
        m = jnp.maximum(jnp.max(s1, axis=-1, keepdims=True), jnp.max(s2, axis=-1, keepdims=True))
        p1 = jnp.where(ok1, jnp.exp(s1 - m), 0.0)
        p2 = jnp.where(ok2, jnp.exp(s2 - m), 0.0)
        den = jnp.sum(p1, axis=-1, keepdims=True) + jnp.sum(p2, axis=-1, keepdims=True)
        o_win = (_dot(p1, win_ref[:, gw + g * HEAD_DIM:gw + (g + 1) * HEAD_DIM])
                 + _dot(p2, new_ref[:, gw + g * HEAD_DIM:gw + (g + 1) * HEAD_DIM])) / jnp.maximum(den, 1e-30)
        gt = gt_all[:, g * LANES:(g + 1) * LANES]
        o = (_gate_cols(gt, 0, hpk) * ocmp_ref[g] + _gate_cols(gt, 1, hpk) * oslc_ref[g]
             + _gate_cols(gt, 2, hpk) * o_win)
        for h in range(hpk):
            col = (g * hpk + h) * HEAD_DIM
            o_ref[:, col:col + HEAD_DIM] = o[h * tq:(h + 1) * tq]


def nsa_sample(st, q, gates, kvs, cmp_kv, cache, page_table, state_win, past_len):
    nb, tq = st.b, st.t
    hpk = B_HEADS_PER_KV
    rows = hpk * tq
    ncp = cmp_kv.shape[3]
    l_all = past_len + tq
    n_cmp = l_all // CMP_STRIDE - CMP_LEN // CMP_STRIDE + 1
    n_slc = -(-l_all // SLC_LEN)
    ns = -(-n_slc // LANES) * LANES
    assert n_cmp <= ncp
    selmap = jnp.asarray(_selection_map(ncp, n_cmp, n_slc, ns), BF16)
    gw = hpk * HEAD_DIM
    cmp_blk = lambda kind: pl.BlockSpec((None, None, None, ncp, HEAD_DIM), lambda b, g: (b, kind, g, 0, 0))
    o_cmp, sel = pl.pallas_call(
        functools.partial(_nsa_s1_kernel, tq=tq, past_len=past_len, n_cmp=n_cmp, n_slc=n_slc,
                          n_sel=min(N_SELECT, n_slc)),
        out_shape=(jax.ShapeDtypeStruct((nb, B_KV_HEADS, rows, HEAD_DIM), F32),
                   jax.ShapeDtypeStruct((nb, B_KV_HEADS, tq, ns), F32)),
        grid=(nb, B_KV_HEADS),
        in_specs=[pl.BlockSpec((tq, gw), lambda b, g: (b, g)), cmp_blk(0), cmp_blk(1),
                  pl.BlockSpec(selmap.shape, lambda b, g: (0, 0))],
        out_specs=(pl.BlockSpec((None, None, rows, HEAD_DIM), lambda b, g: (b, g, 0, 0)),
                   pl.BlockSpec((None, None, tq, ns), lambda b, g: (b, g, 0, 0))),
        compiler_params=_params(("arbitrary", "arbitrary")),
        name="nsa_sample_cmp",
    )(q, cmp_kv, cmp_kv, selmap)

    n_pages = page_table.shape[1]
    page_w = cache.shape[2] * cache.shape[3] * cache.shape[4]
    cache3 = cache.reshape(cache.shape[0], cache.shape[1], page_w)
    half = page_w // 2
    grp = pl.BlockSpec((None, B_KV_HEADS, rows, HEAD_DIM), lambda b, *_: (b, 0, 0, 0))
    pps = _pick_tile(n_pages, 8, 1)
    page_spec = lambda j: pl.BlockSpec((None, PAGE_SIZE, half),
                                       lambda b, p, pt: (pt[b * n_pages + p * pps + j], 0, 1))
    o_slc = pl.pallas_call(
        functools.partial(_nsa_s2_kernel, tq=tq, past_len=past_len, pps=pps),
        out_shape=jax.ShapeDtypeStruct((nb, B_KV_HEADS, rows, HEAD_DIM), F32),
        grid_spec=pltpu.PrefetchScalarGridSpec(
            num_scalar_prefetch=1,
            grid=(nb, n_pages // pps),
            in_specs=[pl.BlockSpec((tq, B_HEADS * HEAD_DIM), lambda b, p, pt: (b, 0))]
            + [page_spec(j) for j in range(pps)]
            + [pl.BlockSpec((None, B_KV_HEADS, tq, ns), lambda b, p, pt: (b, 0, 0, 0)),
               pl.BlockSpec((tq, half), lambda b, p, pt: (b, 1))],
            out_specs=grp,
            scratch_shapes=[pltpu.VMEM((B_KV_HEADS, rows, 1), F32), pltpu.VMEM((B_KV_HEADS, rows, 1), F32),
                            pltpu.VMEM((B_KV_HEADS, rows, HEAD_DIM), F32)]),
        compiler_params=_params(("arbitrary", "arbitrary")),
        name="nsa_sample_slc",
    )(page_table.reshape(-1), q, *([cache3] * pps), sel, kvs)

    lb = state_win.shape[1]
    win2 = state_win.reshape(nb, lb, 2 * B_KV_HEADS * HEAD_DIM)
    return pl.pallas_call(
        functools.partial(_nsa_s3_kernel, tq=tq),
        out_shape=jax.ShapeDtypeStruct((st.r, B_HEADS * HEAD_DIM), F32),
        grid=(nb,),
        in_specs=[pl.BlockSpec((tq, B_HEADS * HEAD_DIM), lambda b: (b, 0)),
                  pl.BlockSpec((tq, B_KV_HEADS * LANES), lambda b: (b, 0)),
                  pl.BlockSpec((None, lb, half), lambda b: (b, 0, 0)),
                  pl.BlockSpec((tq, half), lambda b: (b, 2)),
                  grp, grp],
        out_specs=pl.BlockSpec((tq, B_HEADS * HEAD_DIM), lambda b: (b, 0)),
        compiler_params=_params(("arbitrary",)),
        name="nsa_sample_win",
    )(q, gates, win2, kvs, o_cmp, o_slc)


def kernel(x_prompt, x_sample, state_dil0, state_dil1, state_dil2, cache_nsa, state_win, page_table, c_prompt, c_sample, w_mod, b_mod, g_norm, w_qkv_a, w_o_a, qn_a, kn_a, w_q_b, qn_b, w_gate_b, b_gate_b, w_o_b, w_mod_kv, b_mod_kv, g_kv, w_kv, kn_kv, cmp_pe, cmp_w1, cmp_b1, cmp_w2, cmp_b2, router_w, router_b, w_gu, b_gu, w_down, b_down):
    bp, tp, d = x_prompt.shape
    bs, ts, _ = x_sample.shape
    past_len = page_table.shape[1] * PAGE_SIZE
    stp = Stream(bp, tp, jnp.arange(tp), 512)
    sts = Stream(bs, ts, past_len + jnp.arange(ts), 512)
    streams = [stp, sts]
    n_pages = page_table.shape[1]
    depth = w_mod.shape[0]
    n_a = w_qkv_a.shape[0]
    ng = len(DIL_RATES)
    dil_bufs = (state_dil0, state_dil1, state_dil2)

    c_all = jnp.concatenate([c_prompt, c_sample], axis=0)
    m_all = mod_vectors(c_all, w_mod, b_mod)
    m_kv = mod_vectors(c_all, w_mod_kv[None], b_mod_kv[None])[0]

    def split(m):
        return [stp.mods(m[:bp]), sts.mods(m[bp:])]

    xs = [x_prompt.reshape(-1, d), x_sample.reshape(-1, d)]
    a_qkv = []
    kvs = cmp_ctx = None
    for l in range(depth):
        mods = split(m_all[l])
        g_attn = g_norm[l, 0]
        if l < n_a:
            wq, wo = w_qkv_a[l].astype(BF16), w_o_a[l].astype(BF16)
            qkvs = [qkv_a(st, x, g_attn, md, wq, qn_a[l], kn_a[l]) for st, x, md in zip(streams, xs, mods)]
            a_qkv.append(qkvs)
            outs = [dilated_prompt(stp, qkvs[0]), dilated_sample(sts, qkvs[1], dil_bufs, l)]
        else:
            lb = l - n_a
            wq, wo = w_q_b[lb].astype(BF16), w_o_b[lb].astype(BF16)
            qg = [q_proj_b(st, x, g_attn, md, wq, qn_b[lb], w_gate_b[lb], b_gate_b[lb])
                  for st, x, md in zip(streams, xs, mods)]
            outs = [nsa_prompt(stp, qg[0][0], qg[0][1], kvs[0], cmp_ctx[0]),
                    nsa_sample(sts, qg[1][0], qg[1][1], kvs[1], cmp_ctx[1], cache_nsa, page_table, state_win,
                               past_len)]
        xs = [out_proj_residual(st, o, wo, x, md, 2) for st, o, x, md in zip(streams, outs, xs, mods)]
        xs = moe_layer(streams, xs, g_norm[l, 1], mods, router_w[l], router_b[l], w_gu, b_gu, w_down, b_down, l)
        if l == n_a - 1:
            wkv = w_kv.astype(BF16)
            kvs = [kv_proj(st, x, g_kv, md, wkv, kn_kv) for st, x, md in zip(streams, xs, split(m_kv))]
            w1, w2 = cmp_w1.astype(BF16), cmp_w2.astype(BF16)
            cmp_w = 2 * B_KV_HEADS * HEAD_DIM
            rows_p = _pick_tile(tp, 1024, CMP_STRIDE)
            steps_p = tp // rows_p
            pq = compress_partials(
                kvs[0], lambda j: pl.BlockSpec((rows_p, cmp_w), lambda b, i: (b * steps_p + i, 0)),
                1, rows_p, (bp, steps_p), cmp_pe, w1, tp // CMP_STRIDE)
            cmp_p = compress_finish(pq[0], pq[1], cmp_b1, w2, cmp_b2)
            assert (past_len + ts) // CMP_STRIDE == past_len // CMP_STRIDE
            pps = _pick_tile(n_pages, 8, 1)
            cache3 = cache_nsa.reshape(cache_nsa.shape[0], PAGE_SIZE, -1)
            pq = compress_partials(
                cache3,
                lambda j: pl.BlockSpec((None, PAGE_SIZE, cmp_w),
                                       lambda b, i, pt: (pt[b * n_pages + i * pps + j], 0, 0)),
                pps, PAGE_SIZE, (bs, n_pages // pps), cmp_pe, w1, past_len // CMP_STRIDE,
                prefetch=page_table.reshape(-1))
            cmp_s = compress_finish(pq[0], pq[1], cmp_b1, w2, cmp_b2)
            cmp_ctx = [cmp_p, cmp_s]

    def a_rows(qkv, nb, t, kind, g):
        return qkv.reshape(nb, t, 3, ng, A_HEADS, HEAD_DIM)[:, :, kind, g]

    p_dil, s_new = [], []
    for g in range(ng):
        keep = min(DIL_WINDOWS[g], tp)
        p_dil.append(jnp.stack([
            jnp.stack([a_rows(a_qkv[l][0], bp, tp, 1, g)[:, tp - keep:],
                       a_rows(a_qkv[l][0], bp, tp, 2, g)[:, tp - keep:]], axis=2) for l in range(n_a)]))
        s_new.append(jnp.stack([
            jnp.stack([a_rows(a_qkv[l][1], bs, ts, 1, g), a_rows(a_qkv[l][1], bs, ts, 2, g)], axis=2)
            for l in range(n_a)]))
    s_dil = roll_states(dil_bufs, s_new, DIL_WINDOWS)
    rows_w = 4 * B_KV_HEADS * HEAD_DIM
    p_rows = kvs[0][:, :rows_w].reshape(bp, tp, 4, B_KV_HEADS, HEAD_DIM)
    p_winr = kvs[0][:, rows_w:].reshape(bp, tp, 2, B_KV_HEADS, HEAD_DIM)
    p_win = p_winr[:, tp - min(B_WINDOW, tp):]
    s_rows = kvs[1][:, :rows_w].reshape(bs, ts, 4, B_KV_HEADS, HEAD_DIM)
    s_win = roll_states([state_win[None]], [kvs[1][:, rows_w:].reshape(1, bs, ts, 2, B_KV_HEADS, HEAD_DIM)],
                        [B_WINDOW])[0][0]
    return (xs[0].reshape(bp, tp, d), xs[1].reshape(bs, ts, d), p_dil[0], p_dil[1], p_dil[2], p_rows, p_win,
            s_dil[0], s_dil[1], s_dil[2], s_rows, s_win)
```

```python
import functools

import numpy as np
import jax
import jax.numpy as jnp
from jax import lax
from jax.experimental import pallas as pl
from jax.experimental.pallas import tpu as pltpu

F32 = jnp.float32
BF16 = jnp.bfloat16
I32 = jnp.int32

HEAD_DIM = 128
LANES = 128
DIL_WINDOWS = (128, 512, 2048)
DIL_RATES = (1, 4, 16)
A_HEADS = 8
B_HEADS = 16
B_KV_HEADS = 4
B_HEADS_PER_KV = B_HEADS // B_KV_HEADS
CMP_LEN = 32
CMP_STRIDE = 16
SLC_LEN = 64
N_SELECT = 16
B_WINDOW = 512
FORCE_BONUS = 1.0e4
N_EXPERTS = 32
TOP_K = 4
SWIGLU_ALPHA = 1.702
SWIGLU_LIMIT = 7.0
MOE_BLOCK = 128
MOE_TB = 1152
PAGE_SIZE = 128
ROPE_THETA = 10000.0
NORM_EPS = 1e-6
N_MODS = 6
NEG = -1e30
VMEM_LIMIT = 56 * 1024 * 1024

_NT = (((1,), (1,)), ((), ()))


def _params(sem, **kw):
    return pltpu.CompilerParams(dimension_semantics=sem, vmem_limit_bytes=VMEM_LIMIT, **kw)


def _dot(a, b):
    return jnp.dot(a.astype(BF16), b.astype(BF16), preferred_element_type=F32)


def _dot_nt(a, b):
    return lax.dot_general(a.astype(BF16), b.astype(BF16), _NT, preferred_element_type=F32)


def _pick_tile(n, cap, mult=LANES):
    if n <= cap:
        return n
    t = (cap // mult) * mult
    while t > mult and n % t:
        t -= mult
    assert n % t == 0, (n, cap)
    return t


def _rope_tables(pos):
    half = HEAD_DIM // 2
    inv_freq = 1.0 / (ROPE_THETA ** (jnp.arange(half, dtype=F32) * 2.0 / HEAD_DIM))
    ang = pos.astype(F32)[:, None] * inv_freq[None, :]
    c, s = jnp.cos(ang), jnp.sin(ang)
    return jnp.concatenate([c, c], axis=-1), jnp.concatenate([-s, s], axis=-1)


def _modulate(x, g, shift, scale):
    y = x * lax.rsqrt(jnp.mean(x * x, axis=-1, keepdims=True) + NORM_EPS) * g
    return y * (1.0 + scale) + shift


def _head_norm_rope(blk, gain, cos2, sin2):
    y = blk * lax.rsqrt(jnp.mean(blk * blk, axis=-1, keepdims=True) + NORM_EPS) * gain
    return y * cos2 + pltpu.roll(y, HEAD_DIM // 2, 1) * sin2


def _mod_kernel(c_ref, w_ref, b_ref, o_ref):
    c = c_ref[...]
    o_ref[...] = _dot(c * jax.nn.sigmoid(c), w_ref[...]) + b_ref[...]


def mod_vectors(c, w, b):
    n_layers, d, n = w.shape
    nc = c.shape[0]
    tn = _pick_tile(n, max(LANES, (8 << 20) // (4 * d)))
    return pl.pallas_call(
        _mod_kernel,
        out_shape=jax.ShapeDtypeStruct((n_layers, nc, n), F32),
        grid=(n_layers, n // tn),
        in_specs=[pl.BlockSpec((nc, d), lambda l, j: (0, 0)),
                  pl.BlockSpec((None, d, tn), lambda l, j: (l, 0, j)),
                  pl.BlockSpec((None, 1, tn), lambda l, j: (l, 0, j))],
        out_specs=pl.BlockSpec((None, nc, tn), lambda l, j: (l, 0, j)),
        compiler_params=_params(("arbitrary", "arbitrary")),
        name="mod_vectors",
    )(c, w, b.reshape(n_layers, 1, n))


class Stream:
    def __init__(self, b, t, pos, tm_cap):
        self.b, self.t, self.r = b, t, b * t
        self.per_row_mods = t < LANES
        self.tm = self.r if self.per_row_mods else _pick_tile(t, tm_cap, 8)
        self.tiles_per_seq = 1 if self.per_row_mods else t // self.tm
        cos2, sin2 = _rope_tables(pos)
        if self.per_row_mods:
            cos2, sin2 = jnp.tile(cos2, (b, 1)), jnp.tile(sin2, (b, 1))
        self.cos2, self.sin2 = cos2, sin2

    def mods(self, m):
        if self.per_row_mods:
            return jnp.repeat(m, self.t, axis=0)[None]
        return m[:, None, :]

    def mod_spec(self, d, k, n_grid_axes=2):
        rows = self.r if self.per_row_mods else 1
        tps = self.tiles_per_seq
        return pl.BlockSpec((None, rows, d), lambda i, *_: (i // tps, 0, k))

    def rope_spec(self):
        tps = self.tiles_per_seq
        return pl.BlockSpec((self.tm, HEAD_DIM), lambda i, *_: (i % tps, 0))


def _lin_kernel(flag_ref, gidx_ref, x_ref, g_ref, sh_ref, sc_ref, w_ref, gain_ref, cos_ref, sin_ref, b_ref,
                o_ref, h_scr, *, act):
    del gidx_ref
    j = pl.program_id(1)

    @pl.when(j == 0)
    def _():
        h_scr[...] = _modulate(x_ref[...], g_ref[...], sh_ref[...], sc_ref[...]).astype(BF16)

    acc = jnp.dot(h_scr[...], w_ref[...], preferred_element_type=F32) + b_ref[...]
    tn = acc.shape[1]

    @pl.when(flag_ref[j] == 0)
    def _():
        o_ref[...] = (jax.nn.sigmoid(acc) if act == "sigmoid" else acc).astype(o_ref.dtype)

    @pl.when(flag_ref[j] != 0)
    def _():
        for hh in range(tn // HEAD_DIM):
            sl = slice(hh * HEAD_DIM, (hh + 1) * HEAD_DIM)
            o_ref[:, sl] = _head_norm_rope(acc[:, sl], gain_ref[...], cos_ref[...], sin_ref[...]).astype(o_ref.dtype)


def mod_linear(st, x, g, mods, ksh, ksc, w_bf, *, tn, name, flags=None, gidx=None, gains=None, bias=None,
               act=None):
    r, d = x.shape
    n = w_bf.shape[1]
    nj = n // tn
    if flags is None:
        flags = np.zeros((nj,), np.int32)
        gidx = np.zeros((nj,), np.int32)
        gains = jnp.ones((1, HEAD_DIM), F32)
    if bias is None:
        bias = jnp.zeros((n,), F32)
    gains = gains.reshape(-1, 1, HEAD_DIM)
    tm = st.tm
    grid_spec = pltpu.PrefetchScalarGridSpec(
        num_scalar_prefetch=2,
        grid=(r // tm, nj),
        in_specs=[pl.BlockSpec((tm, d), lambda i, j, *_: (i, 0)),
                  pl.BlockSpec((1, d), lambda i, j, *_: (0, 0)),
                  st.mod_spec(d, ksh), st.mod_spec(d, ksc),
                  pl.BlockSpec((d, tn), lambda i, j, *_: (0, j)),
                  pl.BlockSpec((None, 1, HEAD_DIM), lambda i, j, fl, gi: (gi[j], 0, 0)),
                  st.rope_spec(), st.rope_spec(),
                  pl.BlockSpec((1, tn), lambda i, j, *_: (0, j))],
        out_specs=pl.BlockSpec((tm, tn), lambda i, j, *_: (i, j)),
        scratch_shapes=[pltpu.VMEM((tm, d), BF16)])
    return pl.pallas_call(
        functools.partial(_lin_kernel, act=act),
        out_shape=jax.ShapeDtypeStruct((r, n), F32),
        grid_spec=grid_spec,
        compiler_params=_params(("arbitrary", "arbitrary")),
        name=name,
    )(jnp.asarray(flags, I32), jnp.asarray(gidx, I32), x, g.reshape(1, d), mods, mods, w_bf, gains,
      st.cos2, st.sin2, bias.reshape(1, n))


def _out_kernel(o_ref, w_ref, x_ref, gate_ref, y_ref):
    y_ref[...] = x_ref[...] + gate_ref[...] * _dot(o_ref[...], w_ref[...])


def out_proj_residual(st, o, w_bf, x, mods, kgate):
    r, kd = o.shape
    d = x.shape[1]
    tn = _pick_tile(d, 512)
    tm = st.tm
    nj = d // tn
    rows = st.r if st.per_row_mods else 1
    tps = st.tiles_per_seq
    return pl.pallas_call(
        _out_kernel,
        out_shape=jax.ShapeDtypeStruct((r, d), F32),
        grid=(r // tm, nj),
        in_specs=[pl.BlockSpec((tm, kd), lambda i, j: (i, 0)),
                  pl.BlockSpec((kd, tn), lambda i, j: (0, j)),
                  pl.BlockSpec((tm, tn), lambda i, j: (i, j)),
                  pl.BlockSpec((None, rows, tn), lambda i, j: (i // tps, 0, kgate * nj + j))],
        out_specs=pl.BlockSpec((tm, tn), lambda i, j: (i, j)),
        compiler_params=_params(("arbitrary", "arbitrary")),
        name="out_proj",
    )(o, w_bf, x, mods)


def _dil_prompt_kernel(*refs, wr, rates, sup):
    ng = len(rates)
    o_ref = refs[5 * ng]
    kf, vf, m_scr, l_scr, acc_scr = refs[5 * ng + 1:]
    i = pl.program_id(1)
    scale = HEAD_DIM ** -0.5
    qi = lax.broadcasted_iota(I32, (wr, 2 * wr), 0)
    kk = lax.broadcasted_iota(I32, (wr, 2 * wr), 1)
    band = (kk >= qi) & (kk <= qi + wr)
    for g, rate in enumerate(rates):
        q_ref, kp_ref, kc_ref, vp_ref, vc_ref = refs[5 * g:5 * g + 5]
        prev = wr * rate
        kf[0:prev, :] = kp_ref[...]
        kf[prev:prev + sup, :] = kc_ref[...]
        vf[0:prev, :] = vp_ref[...]
        vf[prev:prev + sup, :] = vc_ref[...]

        def body(idx, carry, q_ref=q_ref, rate=rate, g=g):
            c = idx % rate
            qb = idx // rate
            start = c + rate * qb * wr
            if rate == 1:
                q = q_ref[pl.ds(start, wr), :]
                k = kf[pl.ds(start, 2 * wr), :]
                v = vf[pl.ds(start, 2 * wr), :]
            else:
                q = q_ref[pl.ds(start, wr, stride=rate), :]
                k = kf[pl.ds(start, 2 * wr, stride=rate), :]
                v = vf[pl.ds(start, 2 * wr, stride=rate), :]
            s = _dot_nt(q, k) * scale
            ok = band & ((kk >= wr) | (qb > 0) | (i > 0))
            s = jnp.where(ok, s, NEG)
            m = jnp.max(s, axis=-1, keepdims=True)
            p = jnp.exp(s - m)
            l = jnp.sum(p, axis=-1, keepdims=True)
            acc = _dot(p, v)
            rows = pl.ds(start, wr) if rate == 1 else pl.ds(start, wr, stride=rate)
            if g == 0:
                m_scr[rows, :] = jnp.broadcast_to(m, (wr, HEAD_DIM))
                l_scr[rows, :] = jnp.broadcast_to(l, (wr, HEAD_DIM))
                acc_scr[rows, :] = acc
            else:
                m_old = m_scr[rows, :]
                m_new = jnp.maximum(m_old, m)
                a = jnp.exp(m_old - m_new)
                b = jnp.exp(m - m_new)
                l_scr[rows, :] = a * l_scr[rows, :] + b * l
                acc_scr[rows, :] = a * acc_scr[rows, :] + b * acc
                m_scr[rows, :] = m_new
            return carry

        n_blk = sup // wr
        lax.fori_loop(0, n_blk, body, 0, unroll=4 if n_blk % 4 == 0 else 1)
    o_ref[...] = acc_scr[...] / l_scr[...]


def dilated_prompt(st, qkv):
    ng = len(DIL_RATES)
    wr = DIL_WINDOWS[0] // DIL_RATES[0]
    assert all(w // r == wr for w, r in zip(DIL_WINDOWS, DIL_RATES))
    sup = wr * max(DIL_RATES)
    t = st.t
    assert t % sup == 0
    nsup = t // sup
    kind_cols = ng * A_HEADS
    in_specs, args = [], []
    for g, rate in enumerate(DIL_RATES):
        prev = wr * rate
        ratio = sup // prev
        qcol = lambda h, g=g: g * A_HEADS + h
        cur = lambda kind, g=g: pl.BlockSpec(
            (sup, HEAD_DIM), lambda b, i, h: (b * nsup + i, kind * kind_cols + g * A_HEADS + h))
        prv = lambda kind, g=g, ratio=ratio, prev=prev: pl.BlockSpec(
            (prev, HEAD_DIM),
            lambda b, i, h: (jnp.maximum((b * nsup + i) * ratio - 1, 0), kind * kind_cols + g * A_HEADS + h))
        in_specs += [cur(0), prv(1), cur(1), prv(2), cur(2)]
        args += [qkv] * 5
    max_prev = wr * max(DIL_RATES)
    return pl.pallas_call(
        functools.partial(_dil_prompt_kernel, wr=wr, rates=DIL_RATES, sup=sup),
        out_shape=jax.ShapeDtypeStruct((st.r, A_HEADS * HEAD_DIM), F32),
        grid=(st.b, nsup, A_HEADS),
        in_specs=in_specs,
        out_specs=pl.BlockSpec((sup, HEAD_DIM), lambda b, i, h: (b * nsup + i, h)),
        scratch_shapes=[pltpu.VMEM((max_prev + sup, HEAD_DIM), F32), pltpu.VMEM((max_prev + sup, HEAD_DIM), F32),
                        pltpu.VMEM((sup, HEAD_DIM), F32), pltpu.VMEM((sup, HEAD_DIM), F32),
                        pltpu.VMEM((sup, HEAD_DIM), F32)],
        compiler_params=_params(("arbitrary", "arbitrary", "arbitrary")),
        name="dilated_prompt",
    )(*args)


def _dil_sample_kernel(*refs, wr, rates, windows):
    ng = len(rates)
    o_ref = refs[5 * ng]
    scale = HEAD_DIM ** -0.5
    m_run = l_run = acc_run = None
    for g, (rate, win) in enumerate(zip(rates, windows)):
        q_ref, kn_ref, vn_ref, ks_ref, vs_ref = refs[5 * g:5 * g + 5]
        q = q_ref[...]
        tq = q.shape[0]
        lb = ks_ref.shape[0]
        s1 = _dot_nt(q, ks_ref[...]) * scale
        d1 = lb + lax.broadcasted_iota(I32, (tq, lb), 0) - lax.broadcasted_iota(I32, (tq, lb), 1)
        ok1 = (d1 % rate == 0) & (d1 <= win)
        s1 = jnp.where(ok1, s1, NEG)
        s2 = _dot_nt(q, kn_ref[...]) * scale
        d2 = lax.broadcasted_iota(I32, (tq, tq), 0) - lax.broadcasted_iota(I32, (tq, tq), 1)
        ok2 = (d2 >= 0) & (d2 % rate == 0) & (d2 <= win)
        s2 = jnp.where(ok2, s2, NEG)
        m = jnp.maximum(jnp.max(s1, axis=-1, keepdims=True), jnp.max(s2, axis=-1, keepdims=True))
        p1 = jnp.exp(s1 - m)
        p2 = jnp.exp(s2 - m)
        l = jnp.sum(p1, axis=-1, keepdims=True) + jnp.sum(p2, axis=-1, keepdims=True)
        acc = _dot(p1, vs_ref[...]) + _dot(p2, vn_ref[...])
        if g == 0:
            m_run, l_run, acc_run = m, l, acc
        else:
            m_new = jnp.maximum(m_run, m)
            a = jnp.exp(m_run - m_new)
            b = jnp.exp(m - m_new)
            l_run = a * l_run + b * l
            acc_run = a * acc_run + b * acc
            m_run = m_new
    o_ref[...] = acc_run / l_run


def dilated_sample(st, qkv, bufs, layer):
    ng = len(DIL_RATES)
    wr = DIL_WINDOWS[0] // DIL_RATES[0]
    t = st.t
    kind_cols = ng * A_HEADS
    in_specs, args = [], []
    for g in range(ng):
        buf = bufs[g]
        lb = buf.shape[2]
        assert lb == DIL_WINDOWS[g]
        buf2 = buf.reshape(buf.shape[0], buf.shape[1], lb, 2 * A_HEADS * HEAD_DIM)
        new = lambda kind, g=g: pl.BlockSpec((t, HEAD_DIM), lambda b, h: (b, kind * kind_cols + g * A_HEADS + h))
        old = lambda kv: pl.BlockSpec((None, None, lb, HEAD_DIM), lambda b, h: (layer, b, 0, kv * A_HEADS + h))
        in_specs += [new(0), new(1), new(2), old(0), old(1)]
        args += [qkv, qkv, qkv, buf2, buf2]
    return pl.pallas_call(
        functools.partial(_dil_sample_kernel, wr=wr, rates=DIL_RATES, windows=DIL_WINDOWS),
        out_shape=jax.ShapeDtypeStruct((st.r, A_HEADS * HEAD_DIM), F32),
        grid=(st.b, A_HEADS),
        in_specs=in_specs,
        out_specs=pl.BlockSpec((t, HEAD_DIM), lambda b, h: (b, h)),
        compiler_params=_params(("arbitrary", "arbitrary")),
        name="dilated_sample",
    )(*args)


def _roll_kernel(cur_ref, nxt_ref, new_ref, out_ref):
    i = pl.program_id(2)
    rb, t = cur_ref.shape[0], new_ref.shape[0]
    out_ref[0:rb - t] = cur_ref[t:rb]

    @pl.when(i < pl.num_programs(2) - 1)
    def _():
        out_ref[rb - t:rb] = nxt_ref[...]

    @pl.when(i == pl.num_programs(2) - 1)
    def _():
        out_ref[rb - t:rb] = new_ref[...]


def roll_state(state, new, win):
    lb, t = state.shape[2], new.shape[2]
    keep = min(win, lb + t)
    rb = _pick_tile(lb, 512, t) if lb % t == 0 else 0
    if keep != lb or not rb:
        return jnp.concatenate([state, new], axis=2)[:, :, lb + t - keep:]
    tail = state.shape[3:]
    zeros = (0,) * len(tail)
    per = rb // t
    blk = lambda rows, fn: pl.BlockSpec((None, None, rows) + tail, lambda l, b, i: (l, b, fn(i)) + zeros)
    return pl.pallas_call(
        _roll_kernel,
        out_shape=jax.ShapeDtypeStruct(state.shape, state.dtype),
        grid=state.shape[:2] + (lb // rb,),
        in_specs=[blk(rb, lambda i: i),
                  blk(t, lambda i: jnp.minimum((i + 1) * per, lb // t - 1)),
                  blk(t, lambda i: 0)],
        out_specs=blk(rb, lambda i: i),
        compiler_params=_params(("arbitrary", "arbitrary", "arbitrary")),
        name="roll_state",
    )(state, state, new)


def qkv_a(st, x, g, mods, w_bf, qn, kn):
    n = w_bf.shape[1]
    per_kind = n // 3
    tn = _pick_tile(per_kind, 512)
    kind = np.arange(n // tn) // (per_kind // tn)
    return mod_linear(st, x, g, mods, 0, 1, w_bf, tn=tn, name="qkv_a", flags=(kind < 2).astype(np.int32),
                      gidx=np.minimum(kind, 1).astype(np.int32), gains=jnp.stack([qn, kn]))


META_E, META_W, META_RANK = 0, TOP_K, 2 * TOP_K


def _route_kernel(x_ref, g_ref, sh_ref, sc_ref, rw_ref, rb_ref, cin_ref, h_ref, meta_ref, cnt_ref, carry):
    i = pl.program_id(0)

    @pl.when(i == 0)
    def _():
        carry[...] = cin_ref[...]

    h = _modulate(x_ref[...], g_ref[...], sh_ref[...], sc_ref[...])
    h_ref[...] = h
    tm = h.shape[0]
    logits = _dot(h, rw_ref[...]) + rb_ref[...]
    lane = lax.broadcasted_iota(I32, (tm, LANES), 1)
    work = logits
    sel = jnp.zeros((tm, LANES), F32)
    idxs, vals = [], []
    for _ in range(TOP_K):
        v = jnp.max(work, axis=-1, keepdims=True)
        idx = jnp.min(jnp.where(work == v, lane, LANES), axis=-1, keepdims=True)
        hit = lane == idx
        sel = jnp.where(hit, 1.0, sel)
        work = jnp.where(hit, -jnp.inf, work)
        idxs.append(idx)
        vals.append(v)
    es = [jnp.exp(v - vals[0]) for v in vals]
    den = es[0] + es[1] + es[2] + es[3]
    rr = lax.broadcasted_iota(I32, (tm, tm), 0)
    cc = lax.broadcasted_iota(I32, (tm, tm), 1)
    tri = jnp.where(cc < rr, 1.0, 0.0)
    rank = carry[...] + _dot(tri, sel)
    meta = jnp.zeros((tm, LANES), F32)
    for k in range(TOP_K):
        rk = jnp.sum(jnp.where(lane == idxs[k], rank, 0.0), axis=-1, keepdims=True)
        meta = jnp.where(lane == META_E + k, idxs[k].astype(F32), meta)
        meta = jnp.where(lane == META_W + k, es[k] / den, meta)
        meta = jnp.where(lane == META_RANK + k, rk, meta)
    meta_ref[...] = meta
    carry[...] = carry[...] + jnp.sum(sel, axis=0, keepdims=True)
    cnt_ref[...] = carry[...]


def moe_route(st, x, g, mods, rw_bf, rb, counts_in):
    r, d = x.shape
    tm = st.tm
    return pl.pallas_call(
        _route_kernel,
        out_shape=(jax.ShapeDtypeStruct((r, d), F32), jax.ShapeDtypeStruct((r, LANES), F32),
                   jax.ShapeDtypeStruct((1, LANES), F32)),
        grid=(r // tm,),
        in_specs=[pl.BlockSpec((tm, d), lambda i: (i, 0)),
                  pl.BlockSpec((1, d), lambda i: (0, 0)),
                  st.mod_spec(d, 3), st.mod_spec(d, 4),
                  pl.BlockSpec((d, LANES), lambda i: (0, 0)),
                  pl.BlockSpec((1, LANES), lambda i: (0, 0)),
                  pl.BlockSpec((1, LANES), lambda i: (0, 0))],
        out_specs=(pl.BlockSpec((tm, d), lambda i: (i, 0)),
                   pl.BlockSpec((tm, LANES), lambda i: (i, 0)),
                   pl.BlockSpec((1, LANES), lambda i: (0, 0))),
        scratch_shapes=[pltpu.VMEM((1, LANES), F32)],
        compiler_params=_params(("arbitrary",)),
        name="moe_route",
    )(x, g.reshape(1, d), mods, mods, rw_bf, rb, counts_in)


def _dispatch_kernel(dest_ref, pend_ref, *refs, tms, tiles, n_exp, n_rows):
    h_refs = refs[:len(tms)]
    xb_out, zero_tile, sem = refs[len(tms):]
    i = pl.program_id(0)

    @pl.when(i == 0)
    def _():
        zero_tile[...] = jnp.zeros(zero_tile.shape, zero_tile.dtype)

        def clear(start):
            return pltpu.make_async_copy(
                zero_tile, xb_out.at[pl.ds(pl.multiple_of(start, MOE_BLOCK), MOE_BLOCK)], sem)

        total = pend_ref[n_exp - 1]
        for phase in ("start", "wait"):
            for e in range(n_exp):
                prev = pend_ref[e - 1] if e else 0

                @pl.when(pend_ref[e] > prev)
                def _(e=e):
                    getattr(clear(pend_ref[e] - MOE_BLOCK), phase)()

            def tail(j, carry, phase=phase):
                getattr(clear(total + j * MOE_BLOCK), phase)()
                return carry

            lax.fori_loop(0, (n_rows - total) // MOE_BLOCK, tail, 0)

    first_tile = first_row = 0
    for h_ref, tm, nt in zip(h_refs, tms, tiles):
        @pl.when((i >= first_tile) & (i < first_tile + nt))
        def _(h_ref=h_ref, tm=tm, first_tile=first_tile, first_row=first_row):
            row0 = first_row + (i - first_tile) * tm

            def row(t, carry):
                for k in range(TOP_K):
                    dst = dest_ref[(row0 + t) * TOP_K + k]
                    pltpu.make_async_copy(h_ref.at[pl.ds(t, 1)], xb_out.at[pl.ds(dst, 1)], sem).start()
                return carry

            lax.fori_loop(0, tm, row, 0)

            def drain(t, carry):
                for k in range(TOP_K):
                    pltpu.make_async_copy(h_ref.at[pl.ds(0, 1)], xb_out.at[pl.ds(0, 1)], sem).wait()
                return carry

            lax.fori_loop(0, tm, drain, 0)

        first_tile += nt
        first_row += nt * tm


def moe_dispatch(streams, hs, dest, pend, n_rows):
    d = hs[0].shape[1]
    tms = tuple(st.tm for st in streams)
    tiles = tuple(st.r // st.tm for st in streams)
    firsts = tuple(int(v) for v in np.cumsum((0,) + tiles[:-1]))

    def h_spec(tm, first, nt):
        return pl.BlockSpec((tm, d), lambda i, *_: (jnp.clip(i - first, 0, nt - 1), 0))

    grid_spec = pltpu.PrefetchScalarGridSpec(
        num_scalar_prefetch=2, grid=(sum(tiles),),
        in_specs=[h_spec(tm, first, nt) for tm, first, nt in zip(tms, firsts, tiles)],
        out_specs=pl.BlockSpec(memory_space=pl.ANY),
        scratch_shapes=[pltpu.VMEM((MOE_BLOCK, d), F32), pltpu.SemaphoreType.DMA(())])
    return pl.pallas_call(
        functools.partial(_dispatch_kernel, tms=tms, tiles=tiles, n_exp=pend.shape[0], n_rows=n_rows),
        out_shape=jax.ShapeDtypeStruct((n_rows, d), F32), grid_spec=grid_spec,
        compiler_params=_params(("arbitrary",), has_side_effects=True), name="moe_dispatch",
    )(dest, pend, *hs)


def _ffn_kernel(e_ref, row_ref, nsub_ref, xb_hbm, wgu_ref, bgu_ref, wdn_ref, bdn_ref, perm_ref, yb_in, yb_hbm,
                xs, xsb, acc, wg_bf, wd_bf, sem_in, sem_out, *, tb):
    del e_ref, yb_in
    w = pl.program_id(0)
    f = pl.program_id(1)
    nw = pl.num_programs(0)
    nf = pl.num_programs(1)
    nsub = nsub_ref[w]
    live = nsub > 0
    row0 = pl.multiple_of(row_ref[w], MOE_BLOCK)

    def x_copy(item):
        return pltpu.make_async_copy(xb_hbm.at[pl.ds(pl.multiple_of(row_ref[item], MOE_BLOCK), tb)], xs, sem_in)

    @pl.when((w == 0) & (f == 0) & live)
    def _():
        x_copy(0).start()

    @pl.when((f == 0) & live)
    def _():
        x_copy(w).wait()
        xsb[...] = xs[...].astype(BF16)
        acc[...] = jnp.broadcast_to(bdn_ref[...], acc.shape)

    nxt = jnp.minimum(w + 1, nw - 1)

    @pl.when((f == nf - 1) & (w + 1 < nw) & (nsub_ref[nxt] > 0))
    def _():
        x_copy(nxt).start()

    @pl.when(live)
    def _():
        wg_bf[...] = wgu_ref[...].astype(BF16)
        wd_bf[...] = wdn_ref[...].astype(BF16)
        gu = jnp.dot(xsb[...], wg_bf[...], preferred_element_type=F32) + bgu_ref[...]
        lane = lax.broadcasted_iota(I32, gu.shape, 1)
        gate = jnp.minimum(gu, SWIGLU_LIMIT)
        glu = gate * jax.nn.sigmoid(SWIGLU_ALPHA * gate)
        up = jnp.clip(gu, -SWIGLU_LIMIT, SWIGLU_LIMIT) + 1.0
        prod = jnp.where((lane % 2) == 1, up * pltpu.roll(glu, 1, 1), 0.0).astype(BF16)
        act = jnp.dot(prod, perm_ref[...], preferred_element_type=F32).astype(BF16)
        acc[...] += jnp.dot(act, wd_bf[...], preferred_element_type=F32)

    @pl.when((f == nf - 1) & live)
    def _():
        def copy(s):
            return pltpu.make_async_copy(acc.at[pl.ds(s * MOE_BLOCK, MOE_BLOCK)],
                                         yb_hbm.at[pl.ds(row0 + s * MOE_BLOCK, MOE_BLOCK)], sem_out)
        for s in range(tb // MOE_BLOCK):
            @pl.when(s < nsub)
            def _(s=s):
                copy(s).start()
        for s in range(tb // MOE_BLOCK):
            @pl.when(s < nsub)
            def _(s=s):
                copy(s).wait()


def moe_ffn(xb, yb, item_e, item_row, item_nsub, w_gu, b_gu, w_dn, b_dn, layer, *, tb, tf):
    n_layers, n_exp, d, f2 = w_gu.shape
    dff = f2 // 2
    nf = dff // tf
    n_items = item_e.shape[0]
    perm = np.zeros((2 * tf, tf), np.float32)
    perm[2 * np.arange(tf) + 1, np.arange(tf)] = 1.0

    def fsel(w, f, ns):
        return jnp.where(ns[w] > 0, f, nf - 1)

    grid_spec = pltpu.PrefetchScalarGridSpec(
        num_scalar_prefetch=3,
        grid=(n_items, nf),
        in_specs=[pl.BlockSpec(memory_space=pl.ANY),
                  pl.BlockSpec((None, None, d, 2 * tf), lambda w, f, e, r_, ns: (layer, e[w], 0, fsel(w, f, ns))),
                  pl.BlockSpec((None, None, 1, 2 * tf), lambda w, f, e, r_, ns: (layer, e[w], 0, fsel(w, f, ns))),
                  pl.BlockSpec((None, None, tf, d), lambda w, f, e, r_, ns: (layer, e[w], fsel(w, f, ns), 0)),
                  pl.BlockSpec((None, None, 1, d), lambda w, f, e, *_: (layer, e[w], 0, 0)),
                  pl.BlockSpec((2 * tf, tf), lambda w, f, *_: (0, 0)),
                  pl.BlockSpec(memory_space=pl.ANY)],
        out_specs=pl.BlockSpec(memory_space=pl.ANY),
        scratch_shapes=[pltpu.VMEM((tb, d), F32), pltpu.VMEM((tb, d), BF16), pltpu.VMEM((tb, d), F32),
                        pltpu.VMEM((d, 2 * tf), BF16), pltpu.VMEM((tf, d), BF16),
                        pltpu.SemaphoreType.DMA(()), pltpu.SemaphoreType.DMA(())])
    return pl.pallas_call(
        functools.partial(_ffn_kernel, tb=tb),
        out_shape=jax.ShapeDtypeStruct(yb.shape, F32),
        grid_spec=grid_spec,
        input_output_aliases={9: 0},
        compiler_params=_params(("arbitrary", "arbitrary"), has_side_effects=True),
        name="moe_ffn",
    )(item_e, item_row, item_nsub, xb, w_gu, b_gu.reshape(n_layers, n_exp, 1, f2), w_dn,
      b_dn.reshape(n_layers, n_exp, 1, d), jnp.asarray(perm, BF16), yb)


def _combine_kernel(dest_ref, yb_hbm, meta_ref, x_ref, gate_ref, o_ref, ybuf, sem, *, tm):
    i = pl.program_id(0)

    def row(t, carry):
        for k in range(TOP_K):
            src = dest_ref[(i * tm + t) * TOP_K + k]
            pltpu.make_async_copy(yb_hbm.at[pl.ds(src, 1)], ybuf.at[k, pl.ds(t, 1)], sem).start()
        return carry

    lax.fori_loop(0, tm, row, 0)

    def drain(t, carry):
        for k in range(TOP_K):
            pltpu.make_async_copy(yb_hbm.at[pl.ds(0, 1)], ybuf.at[0, pl.ds(0, 1)], sem).wait()
        return carry

    lax.fori_loop(0, tm, drain, 0)
    meta = meta_ref[...]
    y = jnp.zeros(x_ref.shape, F32)
    for k in range(TOP_K):
        y = y + meta[:, META_W + k:META_W + k + 1] * ybuf[k]
    o_ref[...] = x_ref[...] + gate_ref[...] * y


def moe_combine(st, yb, dest, meta, x, mods):
    r, d = x.shape
    tm = min(st.tm, 256)
    tps = st.tiles_per_seq * (st.tm // tm)
    rows = st.r if st.per_row_mods else 1
    grid_spec = pltpu.PrefetchScalarGridSpec(
        num_scalar_prefetch=1,
        grid=(r // tm,),
        in_specs=[pl.BlockSpec(memory_space=pl.ANY),
                  pl.BlockSpec((tm, LANES), lambda i, *_: (i, 0)),
                  pl.BlockSpec((tm, d), lambda i, *_: (i, 0)),
                  pl.BlockSpec((None, rows, d), lambda i, *_: (i // tps, 0, 5))],
        out_specs=pl.BlockSpec((tm, d), lambda i, *_: (i, 0)),
        scratch_shapes=[pltpu.VMEM((TOP_K, tm, d), F32), pltpu.SemaphoreType.DMA(())])
    return pl.pallas_call(
        functools.partial(_combine_kernel, tm=tm),
        out_shape=jax.ShapeDtypeStruct((r, d), F32),
        grid_spec=grid_spec,
        compiler_params=_params(("arbitrary",)),
        name="moe_combine",
    )(dest, yb, meta, x, mods)


def moe_layer(streams, xs, g, mods_list, router_w, router_b, w_gu, b_gu, w_dn, b_dn, layer):
    d = xs[0].shape[1]
    n_exp = router_w.shape[1]
    rw = jnp.zeros((d, LANES), BF16).at[:, :n_exp].set(router_w.astype(BF16))
    rb = jnp.full((1, LANES), NEG, F32).at[0, :n_exp].set(router_b)
    counts = jnp.zeros((1, LANES), F32)
    hs, metas = [], []
    for st, x, mods in zip(streams, xs, mods_list):
        h, meta, counts = moe_route(st, x, g, mods, rw, rb, counts)
        hs.append(h)
        metas.append(meta)
    n_asg = sum(st.r for st in streams) * TOP_K
    tb = MOE_TB
    cnt = counts[0, :n_exp].astype(I32)
    padded = (cnt + MOE_BLOCK - 1) // MOE_BLOCK * MOE_BLOCK
    pend = jnp.cumsum(padded)
    pstart = pend - padded
    n_items = (n_asg + n_exp * (MOE_BLOCK - 1)) // tb + n_exp + 1
    per_e = (padded + tb - 1) // tb
    iend = jnp.cumsum(per_e)
    total = iend[-1]
    wi = jnp.arange(n_items, dtype=I32)
    live = wi < total
    e_of = jnp.minimum(jnp.sum(wi[:, None] >= iend[None, :], axis=1), n_exp - 1).astype(I32)
    last_e = jnp.minimum(jnp.sum(total - 1 >= iend), n_exp - 1).astype(I32)
    k_of = wi - (iend - per_e)[e_of]
    item_e = jnp.where(live, e_of, last_e).astype(I32)
    item_row = jnp.where(live, pstart[e_of] + k_of * tb, 0).astype(I32)
    item_nsub = jnp.where(live, jnp.minimum(tb, padded[e_of] - k_of * tb) // MOE_BLOCK, 0).astype(I32)
    n_rows = n_asg + n_exp * (MOE_BLOCK - 1) + tb
    n_rows = -(-n_rows // MOE_BLOCK) * MOE_BLOCK
    dests = []
    for meta in metas:
        e = meta[:, META_E:META_E + TOP_K].astype(I32)
        rank = meta[:, META_RANK:META_RANK + TOP_K].astype(I32)
        dests.append((pstart[e] + rank).reshape(-1).astype(I32))
    xb = moe_dispatch(streams, hs, jnp.concatenate(dests), pend.astype(I32), n_rows)
    tf = _pick_tile(w_dn.shape[2], 256)
    yb = moe_ffn(xb, jnp.zeros((n_rows - tb, d), F32), item_e, item_row, item_nsub, w_gu, b_gu, w_dn, b_dn, layer,
                 tb=tb, tf=tf)
    return [moe_combine(st, yb, dest, meta, x, mods)
            for st, dest, meta, x, mods in zip(streams, dests, metas, xs, mods_list)]


def kv_proj(st, x, g, mods, w_bf, kn_kv):
    n = w_bf.shape[1]
    tn = B_KV_HEADS * HEAD_DIM
    j = np.arange(n // tn)
    return mod_linear(st, x, g, mods, 0, 1, w_bf, tn=tn, name="kv_proj", flags=(j % 2 == 0).astype(np.int32),
                      gidx=(j // 2).astype(np.int32), gains=kn_kv)


def q_proj_b(st, x, g, mods, wq_bf, qn, w_gate, b_gate):
    d = x.shape[1]
    n = wq_bf.shape[1]
    nj = n // 512
    q = mod_linear(st, x, g, mods, 0, 1, wq_bf, tn=512, name="q_proj_b", flags=np.ones((nj,), np.int32),
                   gidx=np.zeros((nj,), np.int32), gains=qn)
    per_g = B_HEADS_PER_KV * 3
    wg = jnp.zeros((d, B_KV_HEADS, LANES), F32).at[:, :, :per_g].set(w_gate.reshape(d, B_KV_HEADS, per_g))
    bg = jnp.zeros((B_KV_HEADS, LANES), F32).at[:, :per_g].set(b_gate.reshape(B_KV_HEADS, per_g))
    gates = mod_linear(st, x, g, mods, 0, 1, wg.reshape(d, -1).astype(BF16), tn=B_KV_HEADS * LANES,
                       name="gates_b", bias=bg.reshape(-1), act="sigmoid")
    return q, gates


def _cmp1_kernel(*refs, n_pages, cpp):
    page_refs = refs[:n_pages]
    pe_ref, w1_ref, p_ref, q_ref, rows_scr = refs[n_pages:]
    ch = n_pages * cpp
    rpr = cpp * CMP_STRIDE
    half_len = CMP_LEN // 2
    for j, pr in enumerate(page_refs):
        for c in range(2 * B_KV_HEADS):
            rows_scr[c, j * rpr:(j + 1) * rpr, :] = pr[:, c * HEAD_DIM:(c + 1) * HEAD_DIM]
    for kind in range(2):
        for half, out in ((0, p_ref), (1, q_ref)):
            acc = jnp.zeros((B_KV_HEADS * ch, w1_ref.shape[-1]), F32)
            for l in range(half_len):
                pieces = [rows_scr[kind * B_KV_HEADS + g, pl.ds(l, ch, stride=CMP_STRIDE), :]
                          for g in range(B_KV_HEADS)]
                xl = jnp.concatenate(pieces, axis=0) + pe_ref[kind, pl.ds(half * half_len + l, 1), :]
                acc = acc + _dot(xl, w1_ref[kind, half * half_len + l])
            for g in range(B_KV_HEADS):
                out[kind, g] = acc[g * ch:(g + 1) * ch]


def compress_partials(rows_arr, spec_fn, n_refs, rows_per_ref, grid, pe, w1_bf, n_chunks, prefetch=None):
    assert CMP_LEN == 2 * CMP_STRIDE
    cpp = rows_per_ref // CMP_STRIDE
    ch = n_refs * cpp
    hid = w1_bf.shape[-1]
    nb = grid[0]
    out_shape = jax.ShapeDtypeStruct((nb, 2, B_KV_HEADS, n_chunks, hid), F32)
    out_spec = pl.BlockSpec((None, 2, B_KV_HEADS, ch, hid), lambda b, i, *_: (b, 0, 0, i, 0))
    in_specs = [spec_fn(j) for j in range(n_refs)] + [
        pl.BlockSpec(pe.shape, lambda b, i, *_: (0, 0, 0)),
        pl.BlockSpec(w1_bf.shape, lambda b, i, *_: (0, 0, 0, 0))]
    kern = functools.partial(_cmp1_kernel, n_pages=n_refs, cpp=cpp)
    args = [rows_arr] * n_refs + [pe, w1_bf]
    scratch = [pltpu.VMEM((2 * B_KV_HEADS, ch * CMP_STRIDE, HEAD_DIM), F32)]
    if prefetch is None:
        return pl.pallas_call(kern, out_shape=(out_shape, out_shape), grid=grid, in_specs=in_specs,
                              out_specs=(out_spec, out_spec), scratch_shapes=scratch,
                              compiler_params=_params(("arbitrary", "arbitrary")), name="cmp_partials")(*args)
    gs = pltpu.PrefetchScalarGridSpec(num_scalar_prefetch=1, grid=grid, in_specs=in_specs,
                                      out_specs=(out_spec, out_spec), scratch_shapes=scratch)
    return pl.pallas_call(lambda pt, *r: kern(*r), out_shape=(out_shape, out_shape), grid_spec=gs,
                          compiler_params=_params(("arbitrary", "arbitrary")), name="cmp_partials_paged")(prefetch, *args)


def _cmp2_kernel(p_ref, q_ref, b1_ref, w2_ref, b2_ref, o_ref):
    q = q_ref[...]
    qs = jnp.concatenate([q[1:], jnp.zeros((1, q.shape[1]), F32)], axis=0)
    hid = jax.nn.gelu(p_ref[...] + qs + b1_ref[...])
    o_ref[...] = _dot(hid, w2_ref[...]) + b2_ref[...]


def compress_finish(p, q, b1, w2_bf, b2):
    nb, _, ng, nc, hid = p.shape
    blk = pl.BlockSpec((None, None, None, nc, hid), lambda b, k, g: (b, k, g, 0, 0))
    return pl.pallas_call(
        _cmp2_kernel,
        out_shape=jax.ShapeDtypeStruct((nb, 2, ng, nc, HEAD_DIM), F32),
        grid=(nb, 2, ng),
        in_specs=[blk, blk,
                  pl.BlockSpec((None, 1, hid), lambda b, k, g: (k, 0, 0)),
                  pl.BlockSpec((None, hid, HEAD_DIM), lambda b, k, g: (k, 0, 0)),
                  pl.BlockSpec((None, 1, HEAD_DIM), lambda b, k, g: (k, 0, 0))],
        out_specs=pl.BlockSpec((None, None, None, nc, HEAD_DIM), lambda b, k, g: (b, k, g, 0, 0)),
        compiler_params=_params(("arbitrary", "arbitrary", "arbitrary")),
        name="cmp_finish",
    )(p, q, b1.reshape(2, 1, hid), w2_bf, b2.reshape(2, 1, HEAD_DIM))


def _selection_map(n_cmp_pad, n_cmp, n_slc, n_slc_pad):
    r_s = SLC_LEN // CMP_STRIDE
    r_c = CMP_LEN // CMP_STRIDE
    mult = np.zeros(r_s + r_c - 1, np.float32)
    for m in range(r_s):
        for n in range(r_c):
            mult[m + n] += 1.0
    off = r_s * np.arange(n_slc)[None, :] - np.arange(n_cmp)[:, None]
    ok = (off >= 0) & (off < mult.shape[0])
    out = np.zeros((n_cmp_pad, n_slc_pad), np.float32)
    out[:n_cmp, :n_slc] = np.where(ok, mult[np.clip(off, 0, mult.shape[0] - 1)], 0.0)
    return out


def _stack_heads(q):
    return jnp.concatenate([q[:, h * HEAD_DIM:(h + 1) * HEAD_DIM] for h in range(q.shape[1] // HEAD_DIM)], axis=0)


def _rep_rows(x, n):
    return jnp.concatenate([x] * n, axis=0)


def _cmp_and_select(qs, pos, kc, vc, selmap, *, n_cmp, n_slc, n_sel):
    tq = pos.shape[0]
    hpk = qs.shape[0] // tq
    pos_h = _rep_rows(pos, hpk)
    ncp = kc.shape[0]
    s = _dot_nt(qs, kc) * (HEAD_DIM ** -0.5)
    ci = lax.broadcasted_iota(I32, (hpk * tq, ncp), 1)
    vis = (ci * CMP_STRIDE + (CMP_LEN - 1) <= pos_h) & (ci < n_cmp)
    s = jnp.where(vis, s, NEG)
    e = jnp.where(vis, jnp.exp(s - jnp.max(s, axis=-1, keepdims=True)), 0.0)
    p = e / jnp.maximum(jnp.sum(e, axis=-1, keepdims=True), 1e-30)
    o_cmp = _dot(p, vc)
    pi = _dot(p, selmap)
    imp = pi[0:tq]
    for h in range(1, hpk):
        imp = imp + pi[h * tq:(h + 1) * tq]
    ns = imp.shape[1]
    blk = lax.broadcasted_iota(I32, (tq, ns), 1)
    cur = pos // SLC_LEN
    forced = (blk == 0) | (blk == cur) | (blk == cur - 1)
    valid = blk * SLC_LEN <= pos
    score = jnp.where(valid, imp + jnp.where(forced, FORCE_BONUS, 0.0), -FORCE_BONUS)
    score = jnp.where(blk < n_slc, score, -jnp.inf)

    half = LANES // 2
    if ns == LANES and n_slc <= half and tq % 8 == 0:
        both = jnp.where(blk < half, score, pltpu.roll(score, half, 1))
        j8 = lax.broadcasted_iota(I32, (8, LANES), 1) % half
        rank = jnp.zeros((tq, ns), F32)
        for dd in range(1, half):
            other = pltpu.roll(both, dd, 1)
            tie = _rep_rows(jnp.where(j8 >= dd, 1.0, 0.0), tq // 8)
            rank = rank + jnp.where(other > both, 1.0, 0.0) + jnp.where(other == both, tie, 0.0)
        sel = jnp.where(blk < n_slc, jnp.where(rank < n_sel, 1.0, 0.0), 0.0)
        return o_cmp, sel

    def pick(_, carry):
        sel, work = carry
        v = jnp.max(work, axis=-1, keepdims=True)
        idx = jnp.min(jnp.where(work == v, blk, ns), axis=-1, keepdims=True)
        hit = blk == idx
        return jnp.where(hit, 1.0, sel), jnp.where(hit, -jnp.inf, work)

    sel, _ = lax.fori_loop(0, n_sel, pick, (jnp.zeros((tq, ns), F32), score))
    return o_cmp, sel


def _flash_tiles(qs, k_ref, v_ref, lo, hi, tk, hpk, bias_fn):
    rows = qs.shape[0]
    tq = rows // hpk

    def body(kt, carry):
        m, l, acc = carry
        start = pl.multiple_of(kt * tk, tk)
        k = k_ref[pl.ds(start, tk), :]
        v = v_ref[pl.ds(start, tk), :]
        kpos = kt * tk + lax.broadcasted_iota(I32, (tq, tk), 1)
        s = _dot_nt(qs, k) * (HEAD_DIM ** -0.5) + _rep_rows(bias_fn(kt, kpos), hpk)
        m_new = jnp.maximum(m, jnp.max(s, axis=-1, keepdims=True))
        alpha = jnp.exp(m - m_new)
        p = jnp.exp(s - m_new)
        return m_new, alpha * l + jnp.sum(p, axis=-1, keepdims=True), alpha * acc + _dot(p, v)

    init = (jnp.full((rows, 1), NEG, F32), jnp.zeros((rows, 1), F32), jnp.zeros((rows, HEAD_DIM), F32))
    m, l, acc = lax.fori_loop(lo, hi, body, init)
    return acc / jnp.maximum(l, 1e-30)


def _gate_cols(gt, branch, hpk):
    return jnp.concatenate([gt[:, h * 3 + branch:h * 3 + branch + 1] for h in range(hpk)], axis=0)


def _nsa_prompt_kernel(q_ref, gate_ref, kc_ref, vc_ref, ks_ref, vs_ref, kw_ref, vw_ref, selmap_ref, expand_ref,
                       o_ref, *, tq, tk, wspan, n_cmp, n_slc, n_sel):
    qi = pl.program_id(2)
    hpk = B_HEADS_PER_KV
    qs = _stack_heads(q_ref[...]).astype(BF16)
    pos = qi * tq + lax.broadcasted_iota(I32, (tq, 1), 0)
    o_cmp, sel = _cmp_and_select(qs, pos, kc_ref[...], vc_ref[...], selmap_ref[...],
                                 n_cmp=n_cmp, n_slc=n_slc, n_sel=n_sel)
    sel_bf = sel.astype(BF16)
    hi = (qi * tq + tq - 1) // tk + 1

    def bias_slc(kt, kpos):
        picked = jnp.dot(sel_bf, expand_ref[kt], preferred_element_type=F32)
        return jnp.where(kpos <= pos, (picked - 1.0) * (-NEG), NEG)

    o_slc = _flash_tiles(qs, ks_ref, vs_ref, 0, hi, tk, hpk, bias_slc)

    start = pl.multiple_of(jnp.maximum(qi * tq + tq - wspan, 0), 8)
    kpos = start + lax.broadcasted_iota(I32, (tq, wspan), 1)
    rel = pos - kpos
    bias_w = _rep_rows(jnp.where(rel >= 0, jnp.where(rel < B_WINDOW, 0.0, NEG), NEG), hpk)
    s = _dot_nt(qs, kw_ref[pl.ds(start, wspan), :]) * (HEAD_DIM ** -0.5) + bias_w
    p = jnp.exp(s - jnp.max(s, axis=-1, keepdims=True))
    o_win = _dot(p, vw_ref[pl.ds(start, wspan), :]) / jnp.sum(p, axis=-1, keepdims=True)

    gt = gate_ref[...]
    o = _gate_cols(gt, 0, hpk) * o_cmp + _gate_cols(gt, 1, hpk) * o_slc + _gate_cols(gt, 2, hpk) * o_win
    for h in range(hpk):
        o_ref[:, h * HEAD_DIM:(h + 1) * HEAD_DIM] = o[h * tq:(h + 1) * tq]


def nsa_prompt(st, q, gates, kvs, cmp_kv):
    t = st.t
    tq = _pick_tile(t, 128, 8)
    tk = _pick_tile(t, 512, 8)
    wspan = min(t, -(-(tq + B_WINDOW - 1) // LANES) * LANES)
    nq = t // tq
    ncp = cmp_kv.shape[3]
    n_cmp = t // CMP_STRIDE - CMP_LEN // CMP_STRIDE + 1
    n_slc = -(-t // SLC_LEN)
    ns = -(-n_slc // LANES) * LANES
    selmap = jnp.asarray(_selection_map(ncp, n_cmp, n_slc, ns), BF16)
    expand = np.zeros((t // tk, ns, tk), np.float32)
    kk = np.arange(t)
    expand[kk // tk, kk // SLC_LEN, kk % tk] = 1.0
    gw = B_HEADS_PER_KV * HEAD_DIM
    kv_blk = lambda col: pl.BlockSpec((t, HEAD_DIM), lambda b, g, i: (b, col + g))
    cmp_blk = lambda kind: pl.BlockSpec((None, None, None, ncp, HEAD_DIM), lambda b, g, i: (b, kind, g, 0, 0))
    return pl.pallas_call(
        functools.partial(_nsa_prompt_kernel, tq=tq, tk=tk, wspan=wspan, n_cmp=n_cmp, n_slc=n_slc, n_sel=min(N_SELECT, n_slc)),
        out_shape=jax.ShapeDtypeStruct((st.r, B_HEADS * HEAD_DIM), F32),
        grid=(st.b, B_KV_HEADS, nq),
        in_specs=[pl.BlockSpec((tq, gw), lambda b, g, i: (b * nq + i, g)),
                  pl.BlockSpec((tq, LANES), lambda b, g, i: (b * nq + i, g)),
                  cmp_blk(0), cmp_blk(1),
                  kv_blk(2 * B_KV_HEADS), kv_blk(3 * B_KV_HEADS), kv_blk(4 * B_KV_HEADS), kv_blk(5 * B_KV_HEADS),
                  pl.BlockSpec(selmap.shape, lambda b, g, i: (0, 0)),
                  pl.BlockSpec(expand.shape, lambda b, g, i: (0, 0, 0))],
        out_specs=pl.BlockSpec((tq, gw), lambda b, g, i: (b * nq + i, g)),
        compiler_params=_params(("arbitrary", "arbitrary", "arbitrary")),
        name="nsa_prompt",
    )(q, gates, cmp_kv, cmp_kv, kvs, kvs, kvs, kvs, selmap, jnp.asarray(expand, BF16))


def _nsa_s1_kernel(q_ref, kc_ref, vc_ref, selmap_ref, ocmp_ref, sel_ref, *, tq, past_len, n_cmp, n_slc, n_sel):
    qs = _stack_heads(q_ref[...]).astype(BF16)
    pos = past_len + lax.broadcasted_iota(I32, (tq, 1), 0)
    o_cmp, sel = _cmp_and_select(qs, pos, kc_ref[...], vc_ref[...], selmap_ref[...],
                                 n_cmp=n_cmp, n_slc=n_slc, n_sel=n_sel)
    ocmp_ref[...] = o_cmp
    sel_ref[...] = sel


def _nsa_s2_kernel(*refs, tq, past_len, pps):
    q_ref = refs[1]
    page_refs = refs[2:2 + pps]
    sel_ref, new_ref, o_ref, m_scr, l_scr, acc_scr = refs[2 + pps:]
    p = pl.program_id(1)
    n_steps = pl.num_programs(1)
    hpk = B_HEADS_PER_KV
    gw = B_KV_HEADS * HEAD_DIM
    scale = HEAD_DIM ** -0.5
    ns = sel_ref.shape[-1]

    @pl.when(p == 0)
    def _():
        m_scr[...] = jnp.full(m_scr.shape, NEG, F32)
        l_scr[...] = jnp.zeros(l_scr.shape, F32)
        acc_scr[...] = jnp.zeros(acc_scr.shape, F32)

    q = q_ref[...]

    def update(g, s, ok, v):
        s = jnp.where(ok, s, NEG)
        m_old = m_scr[g]
        m_new = jnp.maximum(m_old, jnp.max(s, axis=-1, keepdims=True))
        alpha = jnp.exp(m_old - m_new)
        pr = jnp.where(ok, jnp.exp(s - m_new), 0.0)
        l_scr[g] = alpha * l_scr[g] + jnp.sum(pr, axis=-1, keepdims=True)
        acc_scr[g] = alpha * acc_scr[g] + _dot(pr, v)
        m_scr[g] = m_new

    nk = pps * PAGE_SIZE
    nn = lax.broadcasted_iota(I32, (ns, nk), 0)
    jj = lax.broadcasted_iota(I32, (ns, nk), 1)
    expand = jnp.where(nn == p * (nk // SLC_LEN) + jj // SLC_LEN, 1.0, 0.0).astype(BF16)
    for g in range(B_KV_HEADS):
        qs = _stack_heads(q[:, g * hpk * HEAD_DIM:(g + 1) * hpk * HEAD_DIM]).astype(BF16)
        k = jnp.concatenate([pr[:, g * HEAD_DIM:(g + 1) * HEAD_DIM] for pr in page_refs], axis=0)
        v = jnp.concatenate([pr[:, gw + g * HEAD_DIM:gw + (g + 1) * HEAD_DIM] for pr in page_refs], axis=0)
        picked = jnp.dot(sel_ref[g].astype(BF16), expand, preferred_element_type=F32)
        update(g, _dot_nt(qs, k) * scale, _rep_rows(picked, hpk) > 0.5, v)

    @pl.when(p == n_steps - 1)
    def _():
        nn2 = lax.broadcasted_iota(I32, (ns, tq), 0)
        tt2 = lax.broadcasted_iota(I32, (ns, tq), 1)
        expand2 = jnp.where(nn2 == (past_len + tt2) // SLC_LEN, 1.0, 0.0).astype(BF16)
        t_q = _rep_rows(lax.broadcasted_iota(I32, (tq, tq), 0), hpk)
        t_k = _rep_rows(lax.broadcasted_iota(I32, (tq, tq), 1), hpk)
        for g in range(B_KV_HEADS):
            qs = _stack_heads(q[:, g * hpk * HEAD_DIM:(g + 1) * hpk * HEAD_DIM]).astype(BF16)
            k = new_ref[:, g * HEAD_DIM:(g + 1) * HEAD_DIM]
            v = new_ref[:, gw + g * HEAD_DIM:gw + (g + 1) * HEAD_DIM]
            picked = jnp.dot(sel_ref[g].astype(BF16), expand2, preferred_element_type=F32)
            update(g, _dot_nt(qs, k) * scale, (t_k <= t_q) & (_rep_rows(picked, hpk) > 0.5), v)
            o_ref[g] = acc_scr[g] / jnp.maximum(l_scr[g], 1e-30)


def _nsa_s3_kernel(q_ref, gate_ref, win_ref, new_ref, ocmp_ref, oslc_ref, o_ref, *, tq):
    hpk = B_HEADS_PER_KV
    gw = B_KV_HEADS * HEAD_DIM
    scale = HEAD_DIM ** -0.5
    lb = win_ref.shape[0]
    rows = hpk * tq
    q = q_ref[...]
    gt_all = gate_ref[...]
    t_q1 = _rep_rows(lax.broadcasted_iota(I32, (tq, lb), 0), hpk)
    i_k1 = _rep_rows(lax.broadcasted_iota(I32, (tq, lb), 1), hpk)
    ok1 = (t_q1 + lb - i_k1) < B_WINDOW
    t_q2 = _rep_rows(lax.broadcasted_iota(I32, (tq, tq), 0), hpk)
    t_k2 = _rep_rows(lax.broadcasted_iota(I32, (tq, tq), 1), hpk)
    ok2 = t_k2 <= t_q2
    for g in range(B_KV_HEADS):
        qs = _stack_heads(q[:, g * hpk * HEAD_DIM:(g + 1) * hpk * HEAD_DIM]).astype(BF16)
        s1 = jnp.where(ok1, _dot_nt(qs, win_ref[:, g * HEAD_DIM:(g + 1) * HEAD_DIM]) * scale, NEG)
        s2 = jnp.where(ok2, _dot_nt(qs, new_ref[:, g * HEAD_DIM:(g + 1) * HEAD_DIM]) * scale, NEG)
        m = jnp.maximum(jnp.max(s1, axis=-1, keepdims=True), jnp.max(s2, axis=-1, keepdims=True))
        p1 = jnp.where(ok1, jnp.exp(s1 - m), 0.0)
        p2 = jnp.where(ok2, jnp.exp(s2 - m), 0.0)
        den = jnp.sum(p1, axis=-1, keepdims=True) + jnp.sum(p2, axis=-1, keepdims=True)
        o_win = (_dot(p1, win_ref[:, gw + g * HEAD_DIM:gw + (g + 1) * HEAD_DIM])
                 + _dot(p2, new_ref[:, gw + g * HEAD_DIM:gw + (g + 1) * HEAD_DIM])) / jnp.maximum(den, 1e-30)
        gt = gt_all[:, g * LANES:(g + 1) * LANES]
        o = (_gate_cols(gt, 0, hpk) * ocmp_ref[g] + _gate_cols(gt, 1, hpk) * oslc_ref[g]
             + _gate_cols(gt, 2, hpk) * o_win)
        for h in range(hpk):
            col = (g * hpk + h) * HEAD_DIM
            o_ref[:, col:col + HEAD_DIM] = o[h * tq:(h + 1) * tq]


def nsa_sample(st, q, gates, kvs, cmp_kv, cache, page_table, state_win, past_len):
    nb, tq = st.b, st.t
    hpk = B_HEADS_PER_KV
    rows = hpk * tq
    ncp = cmp_kv.shape[3]
    l_all = past_len + tq
    n_cmp = l_all // CMP_STRIDE - CMP_LEN // CMP_STRIDE + 1
    n_slc = -(-l_all // SLC_LEN)
    ns = -(-n_slc // LANES) * LANES
    assert n_cmp <= ncp
    selmap = jnp.asarray(_selection_map(ncp, n_cmp, n_slc, ns), BF16)
    gw = hpk * HEAD_DIM
    cmp_blk = lambda kind: pl.BlockSpec((None, None, None, ncp, HEAD_DIM), lambda b, g: (b, kind, g, 0, 0))
    o_cmp, sel = pl.pallas_call(
        functools.partial(_nsa_s1_kernel, tq=tq, past_len=past_len, n_cmp=n_cmp, n_slc=n_slc,
                          n_sel=min(N_SELECT, n_slc)),
        out_shape=(jax.ShapeDtypeStruct((nb, B_KV_HEADS, rows, HEAD_DIM), F32),
                   jax.ShapeDtypeStruct((nb, B_KV_HEADS, tq, ns), F32)),
        grid=(nb, B_KV_HEADS),
        in_specs=[pl.BlockSpec((tq, gw), lambda b, g: (b, g)), cmp_blk(0), cmp_blk(1),
                  pl.BlockSpec(selmap.shape, lambda b, g: (0, 0))],
        out_specs=(pl.BlockSpec((None, None, rows, HEAD_DIM), lambda b, g: (b, g, 0, 0)),
                   pl.BlockSpec((None, None, tq, ns), lambda b, g: (b, g, 0, 0))),
        compiler_params=_params(("arbitrary", "arbitrary")),
        name="nsa_sample_cmp",
    )(q, cmp_kv, cmp_kv, selmap)

    n_pages = page_table.shape[1]
    page_w = cache.shape[2] * cache.shape[3] * cache.shape[4]
    cache3 = cache.reshape(cache.shape[0], cache.shape[1], page_w)
    half = page_w // 2
    grp = pl.BlockSpec((None, B_KV_HEADS, rows, HEAD_DIM), lambda b, *_: (b, 0, 0, 0))
    pps = _pick_tile(n_pages, 8, 1)
    page_spec = lambda j: pl.BlockSpec((None, PAGE_SIZE, half),
                                       lambda b, p, pt: (pt[b * n_pages + p * pps + j], 0, 1))
    o_slc = pl.pallas_call(
        functools.partial(_nsa_s2_kernel, tq=tq, past_len=past_len, pps=pps),
        out_shape=jax.ShapeDtypeStruct((nb, B_KV_HEADS, rows, HEAD_DIM), F32),
        grid_spec=pltpu.PrefetchScalarGridSpec(
            num_scalar_prefetch=1,
            grid=(nb, n_pages // pps),
            in_specs=[pl.BlockSpec((tq, B_HEADS * HEAD_DIM), lambda b, p, pt: (b, 0))]
            + [page_spec(j) for j in range(pps)]
            + [pl.BlockSpec((None, B_KV_HEADS, tq, ns), lambda b, p, pt: (b, 0, 0, 0)),
               pl.BlockSpec((tq, half), lambda b, p, pt: (b, 1))],
            out_specs=grp,
            scratch_shapes=[pltpu.VMEM((B_KV_HEADS, rows, 1), F32), pltpu.VMEM((B_KV_HEADS, rows, 1), F32),
                            pltpu.VMEM((B_KV_HEADS, rows, HEAD_DIM), F32)]),
        compiler_params=_params(("arbitrary", "arbitrary")),
        name="nsa_sample_slc",
    )(page_table.reshape(-1), q, *([cache3] * pps), sel, kvs)

    lb = state_win.shape[1]
    win2 = state_win.reshape(nb, lb, 2 * B_KV_HEADS * HEAD_DIM)
    return pl.pallas_call(
        functools.partial(_nsa_s3_kernel, tq=tq),
        out_shape=jax.ShapeDtypeStruct((st.r, B_HEADS * HEAD_DIM), F32),
        grid=(nb,),
        in_specs=[pl.BlockSpec((tq, B_HEADS * HEAD_DIM), lambda b: (b, 0)),
                  pl.BlockSpec((tq, B_KV_HEADS * LANES), lambda b: (b, 0)),
                  pl.BlockSpec((None, lb, half), lambda b: (b, 0, 0)),
                  pl.BlockSpec((tq, half), lambda b: (b, 2)),
                  grp, grp],
        out_specs=pl.BlockSpec((tq, B_HEADS * HEAD_DIM), lambda b: (b, 0)),
        compiler_params=_params(("arbitrary",)),
        name="nsa_sample_win",
    )(q, gates, win2, kvs, o_cmp, o_slc)


def kernel(x_prompt, x_sample, state_dil0, state_dil1, state_dil2, cache_nsa, state_win, page_table, c_prompt, c_sample, w_mod, b_mod, g_norm, w_qkv_a, w_o_a, qn_a, kn_a, w_q_b, qn_b, w_gate_b, b_gate_b, w_o_b, w_mod_kv, b_mod_kv, g_kv, w_kv, kn_kv, cmp_pe, cmp_w1, cmp_b1, cmp_w2, cmp_b2, router_w, router_b, w_gu, b_gu, w_down, b_down):
    bp, tp, d = x_prompt.shape
    bs, ts, _ = x_sample.shape
    past_len = page_table.shape[1] * PAGE_SIZE
    stp = Stream(bp, tp, jnp.arange(tp), 512)
    sts = Stream(bs, ts, past_len + jnp.arange(ts), 512)
    streams = [stp, sts]
    n_pages = page_table.shape[1]
    depth = w_mod.shape[0]
    n_a = w_qkv_a.shape[0]
    ng = len(DIL_RATES)
    dil_bufs = (state_dil0, state_dil1, state_dil2)

    c_all = jnp.concatenate([c_prompt, c_sample], axis=0)
    m_all = mod_vectors(c_all, w_mod, b_mod)
    m_kv = mod_vectors(c_all, w_mod_kv[None], b_mod_kv[None])[0]

    def split(m):
        return [stp.mods(m[:bp]), sts.mods(m[bp:])]

    xs = [x_prompt.reshape(-1, d), x_sample.reshape(-1, d)]
    a_qkv = []
    kvs = cmp_ctx = None
    for l in range(depth):
        mods = split(m_all[l])
        g_attn = g_norm[l, 0]
        if l < n_a:
            wq, wo = w_qkv_a[l].astype(BF16), w_o_a[l].astype(BF16)
            qkvs = [qkv_a(st, x, g_attn, md, wq, qn_a[l], kn_a[l]) for st, x, md in zip(streams, xs, mods)]
            a_qkv.append(qkvs)
            outs = [dilated_prompt(stp, qkvs[0]), dilated_sample(sts, qkvs[1], dil_bufs, l)]
        else:
            lb = l - n_a
            wq, wo = w_q_b[lb].astype(BF16), w_o_b[lb].astype(BF16)
            qg = [q_proj_b(st, x, g_attn, md, wq, qn_b[lb], w_gate_b[lb], b_gate_b[lb])
                  for st, x, md in zip(streams, xs, mods)]
            outs = [nsa_prompt(stp, qg[0][0], qg[0][1], kvs[0], cmp_ctx[0]),
                    nsa_sample(sts, qg[1][0], qg[1][1], kvs[1], cmp_ctx[1], cache_nsa, page_table, state_win,
                               past_len)]
        xs = [out_proj_residual(st, o, wo, x, md, 2) for st, o, x, md in zip(streams, outs, xs, mods)]
        xs = moe_layer(streams, xs, g_norm[l, 1], mods, router_w[l], router_b[l], w_gu, b_gu, w_down, b_down, l)
        if l == n_a - 1:
            wkv = w_kv.astype(BF16)
            kvs = [kv_proj(st, x, g_kv, md, wkv, kn_kv) for st, x, md in zip(streams, xs, split(m_kv))]
            w1, w2 = cmp_w1.astype(BF16), cmp_w2.astype(BF16)
            cmp_w = 2 * B_KV_HEADS * HEAD_DIM
            rows_p = _pick_tile(tp, 1024, CMP_STRIDE)
            steps_p = tp // rows_p
            pq = compress_partials(
                kvs[0], lambda j: pl.BlockSpec((rows_p, cmp_w), lambda b, i: (b * steps_p + i, 0)),
                1, rows_p, (bp, steps_p), cmp_pe, w1, tp // CMP_STRIDE)
            cmp_p = compress_finish(pq[0], pq[1], cmp_b1, w2, cmp_b2)
            assert (past_len + ts) // CMP_STRIDE == past_len // CMP_STRIDE
            pps = _pick_tile(n_pages, 8, 1)
            cache3 = cache_nsa.reshape(cache_nsa.shape[0], PAGE_SIZE, -1)
            pq = compress_partials(
                cache3,
                lambda j: pl.BlockSpec((None, PAGE_SIZE, cmp_w),
                                       lambda b, i, pt: (pt[b * n_pages + i * pps + j], 0, 0)),
                pps, PAGE_SIZE, (bs, n_pages // pps), cmp_pe, w1, past_len // CMP_STRIDE,
                prefetch=page_table.reshape(-1))
            cmp_s = compress_finish(pq[0], pq[1], cmp_b1, w2, cmp_b2)
            cmp_ctx = [cmp_p, cmp_s]

    def a_rows(qkv, nb, t, kind, g):
        return qkv.reshape(nb, t, 3, ng, A_HEADS, HEAD_DIM)[:, :, kind, g]

    p_dil, s_new = [], []
    for g in range(ng):
        keep = min(DIL_WINDOWS[g], tp)
        p_dil.append(jnp.stack([
            jnp.stack([a_rows(a_qkv[l][0], bp, tp, 1, g)[:, tp - keep:],
                       a_rows(a_qkv[l][0], bp, tp, 2, g)[:, tp - keep:]], axis=2) for l in range(n_a)]))
        s_new.append(jnp.stack([
            jnp.stack([a_rows(a_qkv[l][1], bs, ts, 1, g), a_rows(a_qkv[l][1], bs, ts, 2, g)], axis=2)
            for l in range(n_a)]))
    s_dil = [roll_state(buf, new, win) for buf, new, win in zip(dil_bufs, s_new, DIL_WINDOWS)]
    rows_w = 4 * B_KV_HEADS * HEAD_DIM
    p_rows = kvs[0][:, :rows_w].reshape(bp, tp, 4, B_KV_HEADS, HEAD_DIM)
    p_winr = kvs[0][:, rows_w:].reshape(bp, tp, 2, B_KV_HEADS, HEAD_DIM)
    p_win = p_winr[:, tp - min(B_WINDOW, tp):]
    s_rows = kvs[1][:, :rows_w].reshape(bs, ts, 4, B_KV_HEADS, HEAD_DIM)
    s_win = roll_state(state_win[None], kvs[1][:, rows_w:].reshape(1, bs, ts, 2, B_KV_HEADS, HEAD_DIM), B_WINDOW)[0]
    return (xs[0].reshape(bp, tp, d), xs[1].reshape(bs, ts, d), p_dil[0], p_dil[1], p_dil[2], p_rows, p_win,
            s_dil[0], s_dil[1], s_dil[2], s_rows, s_win)
```

```python
import functools

import numpy as np
import jax
import jax.numpy as jnp
from jax import lax
from jax.experimental import pallas as pl
from jax.experimental.pallas import tpu as pltpu

F32 = jnp.float32
BF16 = jnp.bfloat16
I32 = jnp.int32

HEAD_DIM = 128
LANES = 128
DIL_WINDOWS = (128, 512, 2048)
DIL_RATES = (1, 4, 16)
A_HEADS = 8
B_HEADS = 16
B_KV_HEADS = 4
B_HEADS_PER_KV = B_HEADS // B_KV_HEADS
CMP_LEN = 32
CMP_STRIDE = 16
SLC_LEN = 64
N_SELECT = 16
B_WINDOW = 512
FORCE_BONUS = 1.0e4
N_EXPERTS = 32
TOP_K = 4
SWIGLU_ALPHA = 1.702
SWIGLU_LIMIT = 7.0
MOE_BLOCK = 128
MOE_TB = 1152
PAGE_SIZE = 128
ROPE_THETA = 10000.0
NORM_EPS = 1e-6
N_MODS = 6
NEG = -1e30
VMEM_LIMIT = 56 * 1024 * 1024

_NT = (((1,), (1,)), ((), ()))


def _params(sem, **kw):
    return pltpu.CompilerParams(dimension_semantics=sem, vmem_limit_bytes=VMEM_LIMIT, **kw)


def _dot(a, b):
    return jnp.dot(a.astype(BF16), b.astype(BF16), preferred_element_type=F32)


def _dot_nt(a, b):
    return lax.dot_general(a.astype(BF16), b.astype(BF16), _NT, preferred_element_type=F32)


def _pick_tile(n, cap, mult=LANES):
    if n <= cap:
        return n
    t = (cap // mult) * mult
    while t > mult and n % t:
        t -= mult
    assert n % t == 0, (n, cap)
    return t


def _rope_tables(pos):
    half = HEAD_DIM // 2
    inv_freq = 1.0 / (ROPE_THETA ** (jnp.arange(half, dtype=F32) * 2.0 / HEAD_DIM))
    ang = pos.astype(F32)[:, None] * inv_freq[None, :]
    c, s = jnp.cos(ang), jnp.sin(ang)
    return jnp.concatenate([c, c], axis=-1), jnp.concatenate([-s, s], axis=-1)


def _col_tiles(w, tn):
    k, n = w.shape
    return w.reshape(k, n // tn, tn).transpose(1, 0, 2)


def _modulate(x, g, shift, scale):
    y = x * lax.rsqrt(jnp.mean(x * x, axis=-1, keepdims=True) + NORM_EPS) * g
    return y * (1.0 + scale) + shift


def _head_norm_rope(blk, gain, cos2, sin2):
    y = blk * lax.rsqrt(jnp.mean(blk * blk, axis=-1, keepdims=True) + NORM_EPS) * gain
    return y * cos2 + pltpu.roll(y, HEAD_DIM // 2, 1) * sin2


def _mod_kernel(c_ref, w_ref, b_ref, o_ref):
    c = c_ref[...]
    o_ref[...] = _dot(c * jax.nn.sigmoid(c), w_ref[...]) + b_ref[...]


def mod_vectors(c, w, b):
    n_layers, d, n = w.shape
    nc = c.shape[0]
    tn = _pick_tile(n, max(LANES, (8 << 20) // (4 * d)))
    return pl.pallas_call(
        _mod_kernel,
        out_shape=jax.ShapeDtypeStruct((n_layers, nc, n), F32),
        grid=(n_layers, n // tn),
        in_specs=[pl.BlockSpec((nc, d), lambda l, j: (0, 0)),
                  pl.BlockSpec((None, d, tn), lambda l, j: (l, 0, j)),
                  pl.BlockSpec((None, 1, tn), lambda l, j: (l, 0, j))],
        out_specs=pl.BlockSpec((None, nc, tn), lambda l, j: (l, 0, j)),
        compiler_params=_params(("arbitrary", "arbitrary")),
        name="mod_vectors",
    )(c, w, b.reshape(n_layers, 1, n))


class Stream:
    def __init__(self, b, t, pos, tm_cap):
        self.b, self.t, self.r = b, t, b * t
        self.per_row_mods = t < LANES
        self.tm = self.r if self.per_row_mods else _pick_tile(t, tm_cap, 8)
        self.tiles_per_seq = 1 if self.per_row_mods else t // self.tm
        cos2, sin2 = _rope_tables(pos)
        if self.per_row_mods:
            cos2, sin2 = jnp.tile(cos2, (b, 1)), jnp.tile(sin2, (b, 1))
        self.cos2, self.sin2 = cos2, sin2

    def mods(self, m):
        if self.per_row_mods:
            return jnp.repeat(m, self.t, axis=0)[None]
        return m[:, None, :]

    def mod_spec(self, d, k, n_grid_axes=2):
        rows = self.r if self.per_row_mods else 1
        tps = self.tiles_per_seq
        return pl.BlockSpec((None, rows, d), lambda i, *_: (i // tps, 0, k))

    def rope_spec(self):
        tps = self.tiles_per_seq
        return pl.BlockSpec((self.tm, HEAD_DIM), lambda i, *_: (i % tps, 0))


def _lin_kernel(flag_ref, gidx_ref, x_ref, g_ref, sh_ref, sc_ref, w_ref, gain_ref, cos_ref, sin_ref, b_ref,
                o_ref, h_scr, *, act):
    del gidx_ref
    j = pl.program_id(1)

    @pl.when(j == 0)
    def _():
        h_scr[...] = _modulate(x_ref[...], g_ref[...], sh_ref[...], sc_ref[...]).astype(BF16)

    acc = jnp.dot(h_scr[...], w_ref[...], preferred_element_type=F32) + b_ref[...]
    tn = acc.shape[1]

    @pl.when(flag_ref[j] == 0)
    def _():
        o_ref[...] = (jax.nn.sigmoid(acc) if act == "sigmoid" else acc).astype(o_ref.dtype)

    @pl.when(flag_ref[j] != 0)
    def _():
        for hh in range(tn // HEAD_DIM):
            sl = slice(hh * HEAD_DIM, (hh + 1) * HEAD_DIM)
            o_ref[:, sl] = _head_norm_rope(acc[:, sl], gain_ref[...], cos_ref[...], sin_ref[...]).astype(o_ref.dtype)


def mod_linear(st, x, g, mods, ksh, ksc, w_bf, *, tn, name, flags=None, gidx=None, gains=None, bias=None,
               act=None):
    r, d = x.shape
    n = w_bf.shape[1]
    nj = n // tn
    if flags is None:
        flags = np.zeros((nj,), np.int32)
        gidx = np.zeros((nj,), np.int32)
        gains = jnp.ones((1, HEAD_DIM), F32)
    if bias is None:
        bias = jnp.zeros((n,), F32)
    gains = gains.reshape(-1, 1, HEAD_DIM)
    tm = st.tm
    grid_spec = pltpu.PrefetchScalarGridSpec(
        num_scalar_prefetch=2,
        grid=(r // tm, nj),
        in_specs=[pl.BlockSpec((tm, d), lambda i, j, *_: (i, 0)),
                  pl.BlockSpec((1, d), lambda i, j, *_: (0, 0)),
                  st.mod_spec(d, ksh), st.mod_spec(d, ksc),
                  pl.BlockSpec((None, d, tn), lambda i, j, *_: (j, 0, 0)),
                  pl.BlockSpec((None, 1, HEAD_DIM), lambda i, j, fl, gi: (gi[j], 0, 0)),
                  st.rope_spec(), st.rope_spec(),
                  pl.BlockSpec((1, tn), lambda i, j, *_: (0, j))],
        out_specs=pl.BlockSpec((tm, tn), lambda i, j, *_: (i, j)),
        scratch_shapes=[pltpu.VMEM((tm, d), BF16)])
    return pl.pallas_call(
        functools.partial(_lin_kernel, act=act),
        out_shape=jax.ShapeDtypeStruct((r, n), F32),
        grid_spec=grid_spec,
        compiler_params=_params(("arbitrary", "arbitrary")),
        name=name,
    )(jnp.asarray(flags, I32), jnp.asarray(gidx, I32), x, g.reshape(1, d), mods, mods, _col_tiles(w_bf, tn), gains,
      st.cos2, st.sin2, bias.reshape(1, n))


def _out_kernel(o_ref, w_ref, x_ref, gate_ref, y_ref):
    y_ref[...] = x_ref[...] + gate_ref[...] * _dot(o_ref[...], w_ref[...])


def out_proj_residual(st, o, w_bf, x, mods, kgate):
    r, kd = o.shape
    d = x.shape[1]
    tn = _pick_tile(d, 512)
    tm = st.tm
    nj = d // tn
    rows = st.r if st.per_row_mods else 1
    tps = st.tiles_per_seq
    return pl.pallas_call(
        _out_kernel,
        out_shape=jax.ShapeDtypeStruct((r, d), F32),
        grid=(r // tm, nj),
        in_specs=[pl.BlockSpec((tm, kd), lambda i, j: (i, 0)),
                  pl.BlockSpec((None, kd, tn), lambda i, j: (j, 0, 0)),
                  pl.BlockSpec((tm, tn), lambda i, j: (i, j)),
                  pl.BlockSpec((None, rows, tn), lambda i, j: (i // tps, 0, kgate * nj + j))],
        out_specs=pl.BlockSpec((tm, tn), lambda i, j: (i, j)),
        compiler_params=_params(("arbitrary", "arbitrary")),
        name="out_proj",
    )(o, _col_tiles(w_bf, tn), x, mods)


def _dil_prompt_kernel(*refs, wr, rates, sup):
    ng = len(rates)
    o_ref = refs[5 * ng]
    kf, vf, m_scr, l_scr, acc_scr = refs[5 * ng + 1:]
    i = pl.program_id(1)
    scale = HEAD_DIM ** -0.5
    qi = lax.broadcasted_iota(I32, (wr, 2 * wr), 0)
    kk = lax.broadcasted_iota(I32, (wr, 2 * wr), 1)
    band = (kk >= qi) & (kk <= qi + wr)
    for g, rate in enumerate(rates):
        q_ref, kp_ref, kc_ref, vp_ref, vc_ref = refs[5 * g:5 * g + 5]
        prev = wr * rate
        kf[0:prev, :] = kp_ref[...]
        kf[prev:prev + sup, :] = kc_ref[...]
        vf[0:prev, :] = vp_ref[...]
        vf[prev:prev + sup, :] = vc_ref[...]

        def body(idx, carry, q_ref=q_ref, rate=rate, g=g):
            c = idx % rate
            qb = idx // rate
            start = c + rate * qb * wr
            if rate == 1:
                q = q_ref[pl.ds(start, wr), :]
                k = kf[pl.ds(start, 2 * wr), :]
                v = vf[pl.ds(start, 2 * wr), :]
            else:
                q = q_ref[pl.ds(start, wr, stride=rate), :]
                k = kf[pl.ds(start, 2 * wr, stride=rate), :]
                v = vf[pl.ds(start, 2 * wr, stride=rate), :]
            s = _dot_nt(q, k) * scale
            ok = band & ((kk >= wr) | (qb > 0) | (i > 0))
            s = jnp.where(ok, s, NEG)
            m = jnp.max(s, axis=-1, keepdims=True)
            p = jnp.exp(s - m)
            l = jnp.sum(p, axis=-1, keepdims=True)
            acc = _dot(p, v)
            rows = pl.ds(start, wr) if rate == 1 else pl.ds(start, wr, stride=rate)
            if g == 0:
                m_scr[rows, :] = jnp.broadcast_to(m, (wr, HEAD_DIM))
                l_scr[rows, :] = jnp.broadcast_to(l, (wr, HEAD_DIM))
                acc_scr[rows, :] = acc
            else:
                m_old = m_scr[rows, :]
                m_new = jnp.maximum(m_old, m)
                a = jnp.exp(m_old - m_new)
                b = jnp.exp(m - m_new)
                l_scr[rows, :] = a * l_scr[rows, :] + b * l
                acc_scr[rows, :] = a * acc_scr[rows, :] + b * acc
                m_scr[rows, :] = m_new
            return carry

        n_blk = sup // wr
        lax.fori_loop(0, n_blk, body, 0, unroll=4 if n_blk % 4 == 0 else 1)
    o_ref[...] = acc_scr[...] / l_scr[...]


def dilated_prompt(st, qkv):
    ng = len(DIL_RATES)
    wr = DIL_WINDOWS[0] // DIL_RATES[0]
    assert all(w // r == wr for w, r in zip(DIL_WINDOWS, DIL_RATES))
    sup = wr * max(DIL_RATES)
    t = st.t
    assert t % sup == 0
    nsup = t // sup
    kind_cols = ng * A_HEADS
    in_specs, args = [], []
    for g, rate in enumerate(DIL_RATES):
        prev = wr * rate
        ratio = sup // prev
        qcol = lambda h, g=g: g * A_HEADS + h
        cur = lambda kind, g=g: pl.BlockSpec(
            (sup, HEAD_DIM), lambda b, i, h: (b * nsup + i, kind * kind_cols + g * A_HEADS + h))
        prv = lambda kind, g=g, ratio=ratio, prev=prev: pl.BlockSpec(
            (prev, HEAD_DIM),
            lambda b, i, h: (jnp.maximum((b * nsup + i) * ratio - 1, 0), kind * kind_cols + g * A_HEADS + h))
        in_specs += [cur(0), prv(1), cur(1), prv(2), cur(2)]
        args += [qkv] * 5
    max_prev = wr * max(DIL_RATES)
    return pl.pallas_call(
        functools.partial(_dil_prompt_kernel, wr=wr, rates=DIL_RATES, sup=sup),
        out_shape=jax.ShapeDtypeStruct((st.r, A_HEADS * HEAD_DIM), F32),
        grid=(st.b, nsup, A_HEADS),
        in_specs=in_specs,
        out_specs=pl.BlockSpec((sup, HEAD_DIM), lambda b, i, h: (b * nsup + i, h)),
        scratch_shapes=[pltpu.VMEM((max_prev + sup, HEAD_DIM), F32), pltpu.VMEM((max_prev + sup, HEAD_DIM), F32),
                        pltpu.VMEM((sup, HEAD_DIM), F32), pltpu.VMEM((sup, HEAD_DIM), F32),
                        pltpu.VMEM((sup, HEAD_DIM), F32)],
        compiler_params=_params(("arbitrary", "arbitrary", "arbitrary")),
        name="dilated_prompt",
    )(*args)


def _dil_sample_kernel(*refs, wr, rates, windows):
    ng = len(rates)
    o_ref = refs[5 * ng]
    scale = HEAD_DIM ** -0.5
    m_run = l_run = acc_run = None
    for g, (rate, win) in enumerate(zip(rates, windows)):
        q_ref, kn_ref, vn_ref, ks_ref, vs_ref = refs[5 * g:5 * g + 5]
        q = q_ref[...]
        tq = q.shape[0]
        lb = ks_ref.shape[0]
        s1 = _dot_nt(q, ks_ref[...]) * scale
        d1 = lb + lax.broadcasted_iota(I32, (tq, lb), 0) - lax.broadcasted_iota(I32, (tq, lb), 1)
        ok1 = (d1 % rate == 0) & (d1 <= win)
        s1 = jnp.where(ok1, s1, NEG)
        s2 = _dot_nt(q, kn_ref[...]) * scale
        d2 = lax.broadcasted_iota(I32, (tq, tq), 0) - lax.broadcasted_iota(I32, (tq, tq), 1)
        ok2 = (d2 >= 0) & (d2 % rate == 0) & (d2 <= win)
        s2 = jnp.where(ok2, s2, NEG)
        m = jnp.maximum(jnp.max(s1, axis=-1, keepdims=True), jnp.max(s2, axis=-1, keepdims=True))
        p1 = jnp.exp(s1 - m)
        p2 = jnp.exp(s2 - m)
        l = jnp.sum(p1, axis=-1, keepdims=True) + jnp.sum(p2, axis=-1, keepdims=True)
        acc = _dot(p1, vs_ref[...]) + _dot(p2, vn_ref[...])
        if g == 0:
            m_run, l_run, acc_run = m, l, acc
        else:
            m_new = jnp.maximum(m_run, m)
            a = jnp.exp(m_run - m_new)
            b = jnp.exp(m - m_new)
            l_run = a * l_run + b * l
            acc_run = a * acc_run + b * acc
            m_run = m_new
    o_ref[...] = acc_run / l_run


def dilated_sample(st, qkv, bufs, layer):
    ng = len(DIL_RATES)
    wr = DIL_WINDOWS[0] // DIL_RATES[0]
    t = st.t
    kind_cols = ng * A_HEADS
    in_specs, args = [], []
    for g in range(ng):
        buf = bufs[g]
        lb = buf.shape[2]
        assert lb == DIL_WINDOWS[g]
        buf2 = buf.reshape(buf.shape[0], buf.shape[1], lb, 2 * A_HEADS * HEAD_DIM)
        new = lambda kind, g=g: pl.BlockSpec((t, HEAD_DIM), lambda b, h: (b, kind * kind_cols + g * A_HEADS + h))
        old = lambda kv: pl.BlockSpec((None, None, lb, HEAD_DIM), lambda b, h: (layer, b, 0, kv * A_HEADS + h))
        in_specs += [new(0), new(1), new(2), old(0), old(1)]
        args += [qkv, qkv, qkv, buf2, buf2]
    return pl.pallas_call(
        functools.partial(_dil_sample_kernel, wr=wr, rates=DIL_RATES, windows=DIL_WINDOWS),
        out_shape=jax.ShapeDtypeStruct((st.r, A_HEADS * HEAD_DIM), F32),
        grid=(st.b, A_HEADS),
        in_specs=in_specs,
        out_specs=pl.BlockSpec((t, HEAD_DIM), lambda b, h: (b, h)),
        compiler_params=_params(("arbitrary", "arbitrary")),
        name="dilated_sample",
    )(*args)


def _roll_kernel(cur_ref, nxt_ref, new_ref, out_ref):
    i = pl.program_id(2)
    rb, t = cur_ref.shape[0], new_ref.shape[0]
    out_ref[0:rb - t] = cur_ref[t:rb]

    @pl.when(i < pl.num_programs(2) - 1)
    def _():
        out_ref[rb - t:rb] = nxt_ref[...]

    @pl.when(i == pl.num_programs(2) - 1)
    def _():
        out_ref[rb - t:rb] = new_ref[...]


def roll_state(state, new, win):
    lb, t = state.shape[2], new.shape[2]
    keep = min(win, lb + t)
    rb = _pick_tile(lb, 512, t) if lb % t == 0 else 0
    if keep != lb or not rb:
        return jnp.concatenate([state, new], axis=2)[:, :, lb + t - keep:]
    tail = state.shape[3:]
    zeros = (0,) * len(tail)
    per = rb // t
    blk = lambda rows, fn: pl.BlockSpec((None, None, rows) + tail, lambda l, b, i: (l, b, fn(i)) + zeros)
    return pl.pallas_call(
        _roll_kernel,
        out_shape=jax.ShapeDtypeStruct(state.shape, state.dtype),
        grid=state.shape[:2] + (lb // rb,),
        in_specs=[blk(rb, lambda i: i),
                  blk(t, lambda i: jnp.minimum((i + 1) * per, lb // t - 1)),
                  blk(t, lambda i: 0)],
        out_specs=blk(rb, lambda i: i),
        compiler_params=_params(("arbitrary", "arbitrary", "arbitrary")),
        name="roll_state",
    )(state, state, new)


def qkv_a(st, x, g, mods, w_bf, qn, kn):
    n = w_bf.shape[1]
    per_kind = n // 3
    tn = _pick_tile(per_kind, 512)
    kind = np.arange(n // tn) // (per_kind // tn)
    return mod_linear(st, x, g, mods, 0, 1, w_bf, tn=tn, name="qkv_a", flags=(kind < 2).astype(np.int32),
                      gidx=np.minimum(kind, 1).astype(np.int32), gains=jnp.stack([qn, kn]))


META_E, META_W, META_RANK = 0, TOP_K, 2 * TOP_K


def _route_kernel(x_ref, g_ref, sh_ref, sc_ref, rw_ref, rb_ref, cin_ref, h_ref, meta_ref, cnt_ref, carry):
    i = pl.program_id(0)

    @pl.when(i == 0)
    def _():
        carry[...] = cin_ref[...]

    h = _modulate(x_ref[...], g_ref[...], sh_ref[...], sc_ref[...])
    h_ref[...] = h
    tm = h.shape[0]
    logits = _dot(h, rw_ref[...]) + rb_ref[...]
    lane = lax.broadcasted_iota(I32, (tm, LANES), 1)
    work = logits
    sel = jnp.zeros((tm, LANES), F32)
    idxs, vals = [], []
    for _ in range(TOP_K):
        v = jnp.max(work, axis=-1, keepdims=True)
        idx = jnp.min(jnp.where(work == v, lane, LANES), axis=-1, keepdims=True)
        hit = lane == idx
        sel = jnp.where(hit, 1.0, sel)
        work = jnp.where(hit, -jnp.inf, work)
        idxs.append(idx)
        vals.append(v)
    es = [jnp.exp(v - vals[0]) for v in vals]
    den = es[0] + es[1] + es[2] + es[3]
    rr = lax.broadcasted_iota(I32, (tm, tm), 0)
    cc = lax.broadcasted_iota(I32, (tm, tm), 1)
    tri = jnp.where(cc < rr, 1.0, 0.0)
    rank = carry[...] + _dot(tri, sel)
    meta = jnp.zeros((tm, LANES), F32)
    for k in range(TOP_K):
        rk = jnp.sum(jnp.where(lane == idxs[k], rank, 0.0), axis=-1, keepdims=True)
        meta = jnp.where(lane == META_E + k, idxs[k].astype(F32), meta)
        meta = jnp.where(lane == META_W + k, es[k] / den, meta)
        meta = jnp.where(lane == META_RANK + k, rk, meta)
    meta_ref[...] = meta
    carry[...] = carry[...] + jnp.sum(sel, axis=0, keepdims=True)
    cnt_ref[...] = carry[...]


def moe_route(st, x, g, mods, rw_bf, rb, counts_in):
    r, d = x.shape
    tm = st.tm
    return pl.pallas_call(
        _route_kernel,
        out_shape=(jax.ShapeDtypeStruct((r, d), F32), jax.ShapeDtypeStruct((r, LANES), F32),
                   jax.ShapeDtypeStruct((1, LANES), F32)),
        grid=(r // tm,),
        in_specs=[pl.BlockSpec((tm, d), lambda i: (i, 0)),
                  pl.BlockSpec((1, d), lambda i: (0, 0)),
                  st.mod_spec(d, 3), st.mod_spec(d, 4),
                  pl.BlockSpec((d, LANES), lambda i: (0, 0)),
                  pl.BlockSpec((1, LANES), lambda i: (0, 0)),
                  pl.BlockSpec((1, LANES), lambda i: (0, 0))],
        out_specs=(pl.BlockSpec((tm, d), lambda i: (i, 0)),
                   pl.BlockSpec((tm, LANES), lambda i: (i, 0)),
                   pl.BlockSpec((1, LANES), lambda i: (0, 0))),
        scratch_shapes=[pltpu.VMEM((1, LANES), F32)],
        compiler_params=_params(("arbitrary",)),
        name="moe_route",
    )(x, g.reshape(1, d), mods, mods, rw_bf, rb, counts_in)


def _dispatch_kernel(dest_ref, pend_ref, *refs, tms, tiles, n_exp, n_rows):
    h_refs = refs[:len(tms)]
    xb_out, zero_tile, sem = refs[len(tms):]
    i = pl.program_id(0)

    @pl.when(i == 0)
    def _():
        zero_tile[...] = jnp.zeros(zero_tile.shape, zero_tile.dtype)

        def clear(start):
            return pltpu.make_async_copy(
                zero_tile, xb_out.at[pl.ds(pl.multiple_of(start, MOE_BLOCK), MOE_BLOCK)], sem)

        total = pend_ref[n_exp - 1]
        for phase in ("start", "wait"):
            for e in range(n_exp):
                prev = pend_ref[e - 1] if e else 0

                @pl.when(pend_ref[e] > prev)
                def _(e=e):
                    getattr(clear(pend_ref[e] - MOE_BLOCK), phase)()

            def tail(j, carry, phase=phase):
                getattr(clear(total + j * MOE_BLOCK), phase)()
                return carry

            lax.fori_loop(0, (n_rows - total) // MOE_BLOCK, tail, 0)

    first_tile = first_row = 0
    for h_ref, tm, nt in zip(h_refs, tms, tiles):
        @pl.when((i >= first_tile) & (i < first_tile + nt))
        def _(h_ref=h_ref, tm=tm, first_tile=first_tile, first_row=first_row):
            row0 = first_row + (i - first_tile) * tm

            def row(t, carry):
                for k in range(TOP_K):
                    dst = dest_ref[(row0 + t) * TOP_K + k]
                    pltpu.make_async_copy(h_ref.at[pl.ds(t, 1)], xb_out.at[pl.ds(dst, 1)], sem).start()
                return carry

            lax.fori_loop(0, tm, row, 0)

            def drain(t, carry):
                for k in range(TOP_K):
                    pltpu.make_async_copy(h_ref.at[pl.ds(0, 1)], xb_out.at[pl.ds(0, 1)], sem).wait()
                return carry

            lax.fori_loop(0, tm, drain, 0)

        first_tile += nt
        first_row += nt * tm


def moe_dispatch(streams, hs, dest, pend, n_rows):
    d = hs[0].shape[1]
    tms = tuple(st.tm for st in streams)
    tiles = tuple(st.r // st.tm for st in streams)
    firsts = tuple(int(v) for v in np.cumsum((0,) + tiles[:-1]))

    def h_spec(tm, first, nt):
        return pl.BlockSpec((tm, d), lambda i, *_: (jnp.clip(i - first, 0, nt - 1), 0))

    grid_spec = pltpu.PrefetchScalarGridSpec(
        num_scalar_prefetch=2, grid=(sum(tiles),),
        in_specs=[h_spec(tm, first, nt) for tm, first, nt in zip(tms, firsts, tiles)],
        out_specs=pl.BlockSpec(memory_space=pl.ANY),
        scratch_shapes=[pltpu.VMEM((MOE_BLOCK, d), F32), pltpu.SemaphoreType.DMA(())])
    return pl.pallas_call(
        functools.partial(_dispatch_kernel, tms=tms, tiles=tiles, n_exp=pend.shape[0], n_rows=n_rows),
        out_shape=jax.ShapeDtypeStruct((n_rows, d), F32), grid_spec=grid_spec,
        compiler_params=_params(("arbitrary",), has_side_effects=True), name="moe_dispatch",
    )(dest, pend, *hs)


def _ffn_kernel(e_ref, row_ref, nsub_ref, xb_hbm, *refs, tb, n_split):
    wgu_refs = refs[:n_split]
    (bgu_ref, wdn_ref, bdn_ref, perm_ref, yb_in, yb_hbm,
     xs, xsb, acc, wg_bf, wd_bf, sem_in, sem_out) = refs[n_split:]
    del e_ref, yb_in
    w = pl.program_id(0)
    f = pl.program_id(1)
    nw = pl.num_programs(0)
    nf = pl.num_programs(1)
    nsub = nsub_ref[w]
    live = nsub > 0
    row0 = pl.multiple_of(row_ref[w], MOE_BLOCK)

    def x_copy(item):
        return pltpu.make_async_copy(xb_hbm.at[pl.ds(pl.multiple_of(row_ref[item], MOE_BLOCK), tb)], xs, sem_in)

    @pl.when((w == 0) & (f == 0) & live)
    def _():
        x_copy(0).start()

    @pl.when((f == 0) & live)
    def _():
        x_copy(w).wait()
        xsb[...] = xs[...].astype(BF16)
        acc[...] = jnp.broadcast_to(bdn_ref[...], acc.shape)

    nxt = jnp.minimum(w + 1, nw - 1)

    @pl.when((f == nf - 1) & (w + 1 < nw) & (nsub_ref[nxt] > 0))
    def _():
        x_copy(nxt).start()

    @pl.when(live)
    def _():
        dq = wg_bf.shape[0] // n_split
        for q, wgu_ref in enumerate(wgu_refs):
            wg_bf[q * dq:(q + 1) * dq, :] = wgu_ref[...].astype(BF16)
        wd_bf[...] = wdn_ref[...].astype(BF16)
        gu = jnp.dot(xsb[...], wg_bf[...], preferred_element_type=F32) + bgu_ref[...]
        lane = lax.broadcasted_iota(I32, gu.shape, 1)
        gate = jnp.minimum(gu, SWIGLU_LIMIT)
        glu = gate * jax.nn.sigmoid(SWIGLU_ALPHA * gate)
        up = jnp.clip(gu, -SWIGLU_LIMIT, SWIGLU_LIMIT) + 1.0
        prod = jnp.where((lane % 2) == 1, up * pltpu.roll(glu, 1, 1), 0.0).astype(BF16)
        act = jnp.dot(prod, perm_ref[...], preferred_element_type=F32).astype(BF16)
        acc[...] += jnp.dot(act, wd_bf[...], preferred_element_type=F32)

    @pl.when((f == nf - 1) & live)
    def _():
        def copy(s):
            return pltpu.make_async_copy(acc.at[pl.ds(s * MOE_BLOCK, MOE_BLOCK)],
                                         yb_hbm.at[pl.ds(row0 + s * MOE_BLOCK, MOE_BLOCK)], sem_out)
        for s in range(tb // MOE_BLOCK):
            @pl.when(s < nsub)
            def _(s=s):
                copy(s).start()
        for s in range(tb // MOE_BLOCK):
            @pl.when(s < nsub)
            def _(s=s):
                copy(s).wait()


def moe_ffn(xb, yb, item_e, item_row, item_nsub, w_gu, b_gu, w_dn, b_dn, layer, *, tb, tf):
    n_layers, n_exp, d, f2 = w_gu.shape
    dff = f2 // 2
    nf = dff // tf
    n_items = item_e.shape[0]
    n_split = 4 if d % 32 == 0 else 1
    perm = np.zeros((2 * tf, tf), np.float32)
    perm[2 * np.arange(tf) + 1, np.arange(tf)] = 1.0

    def fsel(w, f, ns):
        return jnp.where(ns[w] > 0, f, nf - 1)

    grid_spec = pltpu.PrefetchScalarGridSpec(
        num_scalar_prefetch=3,
        grid=(n_items, nf),
        in_specs=[pl.BlockSpec(memory_space=pl.ANY)] + [
            pl.BlockSpec((None, None, d // n_split, 2 * tf),
                         lambda w, f, e, r_, ns, q=q: (layer, e[w], q, fsel(w, f, ns))) for q in range(n_split)] + [
                  pl.BlockSpec((None, None, 1, 2 * tf), lambda w, f, e, r_, ns: (layer, e[w], 0, fsel(w, f, ns))),
                  pl.BlockSpec((None, None, tf, d), lambda w, f, e, r_, ns: (layer, e[w], fsel(w, f, ns), 0)),
                  pl.BlockSpec((None, None, 1, d), lambda w, f, e, *_: (layer, e[w], 0, 0)),
                  pl.BlockSpec((2 * tf, tf), lambda w, f, *_: (0, 0)),
                  pl.BlockSpec(memory_space=pl.ANY)],
        out_specs=pl.BlockSpec(memory_space=pl.ANY),
        scratch_shapes=[pltpu.VMEM((tb, d), F32), pltpu.VMEM((tb, d), BF16), pltpu.VMEM((tb, d), F32),
                        pltpu.VMEM((d, 2 * tf), BF16), pltpu.VMEM((tf, d), BF16),
                        pltpu.SemaphoreType.DMA(()), pltpu.SemaphoreType.DMA(())])
    return pl.pallas_call(
        functools.partial(_ffn_kernel, tb=tb, n_split=n_split),
        out_shape=jax.ShapeDtypeStruct(yb.shape, F32),
        grid_spec=grid_spec,
        input_output_aliases={8 + n_split: 0},
        compiler_params=_params(("arbitrary", "arbitrary"), has_side_effects=True),
        name="moe_ffn",
    )(item_e, item_row, item_nsub, xb, *([w_gu] * n_split), b_gu.reshape(n_layers, n_exp, 1, f2), w_dn,
      b_dn.reshape(n_layers, n_exp, 1, d), jnp.asarray(perm, BF16), yb)


def _combine_kernel(dest_ref, yb_hbm, meta_ref, x_ref, gate_ref, o_ref, ybuf, sem, *, tm):
    i = pl.program_id(0)

    def row(t, carry):
        for k in range(TOP_K):
            src = dest_ref[(i * tm + t) * TOP_K + k]
            pltpu.make_async_copy(yb_hbm.at[pl.ds(src, 1)], ybuf.at[k, pl.ds(t, 1)], sem).start()
        return carry

    lax.fori_loop(0, tm, row, 0)

    def drain(t, carry):
        for k in range(TOP_K):
            pltpu.make_async_copy(yb_hbm.at[pl.ds(0, 1)], ybuf.at[0, pl.ds(0, 1)], sem).wait()
        return carry

    lax.fori_loop(0, tm, drain, 0)
    meta = meta_ref[...]
    y = jnp.zeros(x_ref.shape, F32)
    for k in range(TOP_K):
        y = y + meta[:, META_W + k:META_W + k + 1] * ybuf[k]
    o_ref[...] = x_ref[...] + gate_ref[...] * y


def moe_combine(st, yb, dest, meta, x, mods):
    r, d = x.shape
    tm = min(st.tm, 256)
    tps = st.tiles_per_seq * (st.tm // tm)
    rows = st.r if st.per_row_mods else 1
    grid_spec = pltpu.PrefetchScalarGridSpec(
        num_scalar_prefetch=1,
        grid=(r // tm,),
        in_specs=[pl.BlockSpec(memory_space=pl.ANY),
                  pl.BlockSpec((tm, LANES), lambda i, *_: (i, 0)),
                  pl.BlockSpec((tm, d), lambda i, *_: (i, 0)),
                  pl.BlockSpec((None, rows, d), lambda i, *_: (i // tps, 0, 5))],
        out_specs=pl.BlockSpec((tm, d), lambda i, *_: (i, 0)),
        scratch_shapes=[pltpu.VMEM((TOP_K, tm, d), F32), pltpu.SemaphoreType.DMA(())])
    return pl.pallas_call(
        functools.partial(_combine_kernel, tm=tm),
        out_shape=jax.ShapeDtypeStruct((r, d), F32),
        grid_spec=grid_spec,
        compiler_params=_params(("arbitrary",)),
        name="moe_combine",
    )(dest, yb, meta, x, mods)


def moe_layer(streams, xs, g, mods_list, router_w, router_b, w_gu, b_gu, w_dn, b_dn, layer):
    d = xs[0].shape[1]
    n_exp = router_w.shape[1]
    rw = jnp.zeros((d, LANES), BF16).at[:, :n_exp].set(router_w.astype(BF16))
    rb = jnp.full((1, LANES), NEG, F32).at[0, :n_exp].set(router_b)
    counts = jnp.zeros((1, LANES), F32)
    hs, metas = [], []
    for st, x, mods in zip(streams, xs, mods_list):
        h, meta, counts = moe_route(st, x, g, mods, rw, rb, counts)
        hs.append(h)
        metas.append(meta)
    n_asg = sum(st.r for st in streams) * TOP_K
    tb = MOE_TB
    cnt = counts[0, :n_exp].astype(I32)
    padded = (cnt + MOE_BLOCK - 1) // MOE_BLOCK * MOE_BLOCK
    pend = jnp.cumsum(padded)
    pstart = pend - padded
    n_items = (n_asg + n_exp * (MOE_BLOCK - 1)) // tb + n_exp + 1
    per_e = (padded + tb - 1) // tb
    iend = jnp.cumsum(per_e)
    total = iend[-1]
    wi = jnp.arange(n_items, dtype=I32)
    live = wi < total
    e_of = jnp.minimum(jnp.sum(wi[:, None] >= iend[None, :], axis=1), n_exp - 1).astype(I32)
    last_e = jnp.minimum(jnp.sum(total - 1 >= iend), n_exp - 1).astype(I32)
    k_of = wi - (iend - per_e)[e_of]
    item_e = jnp.where(live, e_of, last_e).astype(I32)
    item_row = jnp.where(live, pstart[e_of] + k_of * tb, 0).astype(I32)
    item_nsub = jnp.where(live, jnp.minimum(tb, padded[e_of] - k_of * tb) // MOE_BLOCK, 0).astype(I32)
    n_rows = n_asg + n_exp * (MOE_BLOCK - 1) + tb
    n_rows = -(-n_rows // MOE_BLOCK) * MOE_BLOCK
    dests = []
    for meta in metas:
        e = meta[:, META_E:META_E + TOP_K].astype(I32)
        rank = meta[:, META_RANK:META_RANK + TOP_K].astype(I32)
        dests.append((pstart[e] + rank).reshape(-1).astype(I32))
    xb = moe_dispatch(streams, hs, jnp.concatenate(dests), pend.astype(I32), n_rows)
    tf = _pick_tile(w_dn.shape[2], 256)
    yb = moe_ffn(xb, jnp.zeros((n_rows - tb, d), F32), item_e, item_row, item_nsub, w_gu, b_gu, w_dn, b_dn, layer,
                 tb=tb, tf=tf)
    return [moe_combine(st, yb, dest, meta, x, mods)
            for st, dest, meta, x, mods in zip(streams, dests, metas, xs, mods_list)]


def kv_proj(st, x, g, mods, w_bf, kn_kv):
    n = w_bf.shape[1]
    tn = B_KV_HEADS * HEAD_DIM
    j = np.arange(n // tn)
    return mod_linear(st, x, g, mods, 0, 1, w_bf, tn=tn, name="kv_proj", flags=(j % 2 == 0).astype(np.int32),
                      gidx=(j // 2).astype(np.int32), gains=kn_kv)


def q_proj_b(st, x, g, mods, wq_bf, qn, w_gate, b_gate):
    d = x.shape[1]
    n = wq_bf.shape[1]
    nj = n // 512
    q = mod_linear(st, x, g, mods, 0, 1, wq_bf, tn=512, name="q_proj_b", flags=np.ones((nj,), np.int32),
                   gidx=np.zeros((nj,), np.int32), gains=qn)
    per_g = B_HEADS_PER_KV * 3
    wg = jnp.zeros((d, B_KV_HEADS, LANES), F32).at[:, :, :per_g].set(w_gate.reshape(d, B_KV_HEADS, per_g))
    bg = jnp.zeros((B_KV_HEADS, LANES), F32).at[:, :per_g].set(b_gate.reshape(B_KV_HEADS, per_g))
    gates = mod_linear(st, x, g, mods, 0, 1, wg.reshape(d, -1).astype(BF16), tn=B_KV_HEADS * LANES,
                       name="gates_b", bias=bg.reshape(-1), act="sigmoid")
    return q, gates


def _cmp1_kernel(*refs, n_pages, cpp):
    page_refs = refs[:n_pages]
    pe_ref, w1_ref, p_ref, q_ref, rows_scr = refs[n_pages:]
    ch = n_pages * cpp
    rpr = cpp * CMP_STRIDE
    half_len = CMP_LEN // 2
    for j, pr in enumerate(page_refs):
        for c in range(2 * B_KV_HEADS):
            if len(pr.shape) == 4:
                rows_scr[c, j * rpr:(j + 1) * rpr, :] = pr[:, c // B_KV_HEADS, c % B_KV_HEADS, :]
            else:
                rows_scr[c, j * rpr:(j + 1) * rpr, :] = pr[:, c * HEAD_DIM:(c + 1) * HEAD_DIM]
    for kind in range(2):
        for half, out in ((0, p_ref), (1, q_ref)):
            acc = jnp.zeros((B_KV_HEADS * ch, w1_ref.shape[-1]), F32)
            for l in range(half_len):
                pieces = [rows_scr[kind * B_KV_HEADS + g, pl.ds(l, ch, stride=CMP_STRIDE), :]
                          for g in range(B_KV_HEADS)]
                xl = jnp.concatenate(pieces, axis=0) + pe_ref[kind, pl.ds(half * half_len + l, 1), :]
                acc = acc + _dot(xl, w1_ref[kind, half * half_len + l])
            for g in range(B_KV_HEADS):
                out[kind, g] = acc[g * ch:(g + 1) * ch]


def compress_partials(rows_arr, spec_fn, n_refs, rows_per_ref, grid, pe, w1_bf, n_chunks, prefetch=None):
    assert CMP_LEN == 2 * CMP_STRIDE
    cpp = rows_per_ref // CMP_STRIDE
    ch = n_refs * cpp
    hid = w1_bf.shape[-1]
    nb = grid[0]
    out_shape = jax.ShapeDtypeStruct((nb, 2, B_KV_HEADS, n_chunks, hid), F32)
    out_spec = pl.BlockSpec((None, 2, B_KV_HEADS, ch, hid), lambda b, i, *_: (b, 0, 0, i, 0))
    in_specs = [spec_fn(j) for j in range(n_refs)] + [
        pl.BlockSpec(pe.shape, lambda b, i, *_: (0, 0, 0)),
        pl.BlockSpec(w1_bf.shape, lambda b, i, *_: (0, 0, 0, 0))]
    kern = functools.partial(_cmp1_kernel, n_pages=n_refs, cpp=cpp)
    args = [rows_arr] * n_refs + [pe, w1_bf]
    scratch = [pltpu.VMEM((2 * B_KV_HEADS, ch * CMP_STRIDE, HEAD_DIM), F32)]
    if prefetch is None:
        return pl.pallas_call(kern, out_shape=(out_shape, out_shape), grid=grid, in_specs=in_specs,
                              out_specs=(out_spec, out_spec), scratch_shapes=scratch,
                              compiler_params=_params(("arbitrary", "arbitrary")), name="cmp_partials")(*args)
    gs = pltpu.PrefetchScalarGridSpec(num_scalar_prefetch=1, grid=grid, in_specs=in_specs,
                                      out_specs=(out_spec, out_spec), scratch_shapes=scratch)
    return pl.pallas_call(lambda pt, *r: kern(*r), out_shape=(out_shape, out_shape), grid_spec=gs,
                          compiler_params=_params(("arbitrary", "arbitrary")), name="cmp_partials_paged")(prefetch, *args)


def _cmp2_kernel(p_ref, q_ref, b1_ref, w2_ref, b2_ref, o_ref):
    q = q_ref[...]
    qs = jnp.concatenate([q[1:], jnp.zeros((1, q.shape[1]), F32)], axis=0)
    hid = jax.nn.gelu(p_ref[...] + qs + b1_ref[...])
    o_ref[...] = _dot(hid, w2_ref[...]) + b2_ref[...]


def compress_finish(p, q, b1, w2_bf, b2):
    nb, _, ng, nc, hid = p.shape
    blk = pl.BlockSpec((None, None, None, nc, hid), lambda b, k, g: (b, k, g, 0, 0))
    return pl.pallas_call(
        _cmp2_kernel,
        out_shape=jax.ShapeDtypeStruct((nb, 2, ng, nc, HEAD_DIM), F32),
        grid=(nb, 2, ng),
        in_specs=[blk, blk,
                  pl.BlockSpec((None, 1, hid), lambda b, k, g: (k, 0, 0)),
                  pl.BlockSpec((None, hid, HEAD_DIM), lambda b, k, g: (k, 0, 0)),
                  pl.BlockSpec((None, 1, HEAD_DIM), lambda b, k, g: (k, 0, 0))],
        out_specs=pl.BlockSpec((None, None, None, nc, HEAD_DIM), lambda b, k, g: (b, k, g, 0, 0)),
        compiler_params=_params(("arbitrary", "arbitrary", "arbitrary")),
        name="cmp_finish",
    )(p, q, b1.reshape(2, 1, hid), w2_bf, b2.reshape(2, 1, HEAD_DIM))


def _selection_map(n_cmp_pad, n_cmp, n_slc, n_slc_pad):
    r_s = SLC_LEN // CMP_STRIDE
    r_c = CMP_LEN // CMP_STRIDE
    mult = np.zeros(r_s + r_c - 1, np.float32)
    for m in range(r_s):
        for n in range(r_c):
            mult[m + n] += 1.0
    off = r_s * np.arange(n_slc)[None, :] - np.arange(n_cmp)[:, None]
    ok = (off >= 0) & (off < mult.shape[0])
    out = np.zeros((n_cmp_pad, n_slc_pad), np.float32)
    out[:n_cmp, :n_slc] = np.where(ok, mult[np.clip(off, 0, mult.shape[0] - 1)], 0.0)
    return out


def _stack_heads(q):
    return jnp.concatenate([q[:, h * HEAD_DIM:(h + 1) * HEAD_DIM] for h in range(q.shape[1] // HEAD_DIM)], axis=0)


def _rep_rows(x, n):
    return jnp.concatenate([x] * n, axis=0)


def _cmp_and_select(qs, pos, kc, vc, selmap, *, n_cmp, n_slc, n_sel):
    tq = pos.shape[0]
    hpk = qs.shape[0] // tq
    pos_h = _rep_rows(pos, hpk)
    ncp = kc.shape[0]
    s = _dot_nt(qs, kc) * (HEAD_DIM ** -0.5)
    ci = lax.broadcasted_iota(I32, (hpk * tq, ncp), 1)
    vis = (ci * CMP_STRIDE + (CMP_LEN - 1) <= pos_h) & (ci < n_cmp)
    s = jnp.where(vis, s, NEG)
    e = jnp.where(vis, jnp.exp(s - jnp.max(s, axis=-1, keepdims=True)), 0.0)
    p = e / jnp.maximum(jnp.sum(e, axis=-1, keepdims=True), 1e-30)
    o_cmp = _dot(p, vc)
    pi = _dot(p, selmap)
    imp = pi[0:tq]
    for h in range(1, hpk):
        imp = imp + pi[h * tq:(h + 1) * tq]
    ns = imp.shape[1]
    blk = lax.broadcasted_iota(I32, (tq, ns), 1)
    cur = pos // SLC_LEN
    forced = (blk == 0) | (blk == cur) | (blk == cur - 1)
    valid = blk * SLC_LEN <= pos
    score = jnp.where(valid, imp + jnp.where(forced, FORCE_BONUS, 0.0), -FORCE_BONUS)
    score = jnp.where(blk < n_slc, score, -jnp.inf)

    half = LANES // 2
    if ns == LANES and n_slc <= half and tq % 8 == 0:
        both = jnp.where(blk < half, score, pltpu.roll(score, half, 1))
        j8 = lax.broadcasted_iota(I32, (8, LANES), 1) % half
        rank = jnp.zeros((tq, ns), F32)
        for dd in range(1, half):
            other = pltpu.roll(both, dd, 1)
            tie = _rep_rows(jnp.where(j8 >= dd, 1.0, 0.0), tq // 8)
            rank = rank + jnp.where(other > both, 1.0, 0.0) + jnp.where(other == both, tie, 0.0)
        sel = jnp.where(blk < n_slc, jnp.where(rank < n_sel, 1.0, 0.0), 0.0)
        return o_cmp, sel

    def pick(_, carry):
        sel, work = carry
        v = jnp.max(work, axis=-1, keepdims=True)
        idx = jnp.min(jnp.where(work == v, blk, ns), axis=-1, keepdims=True)
        hit = blk == idx
        return jnp.where(hit, 1.0, sel), jnp.where(hit, -jnp.inf, work)

    sel, _ = lax.fori_loop(0, n_sel, pick, (jnp.zeros((tq, ns), F32), score))
    return o_cmp, sel


def _flash_tiles(qs, k_ref, v_ref, lo, hi, tk, hpk, bias_fn):
    rows = qs.shape[0]
    tq = rows // hpk

    def body(kt, carry):
        m, l, acc = carry
        start = pl.multiple_of(kt * tk, tk)
        k = k_ref[pl.ds(start, tk), :]
        v = v_ref[pl.ds(start, tk), :]
        kpos = kt * tk + lax.broadcasted_iota(I32, (tq, tk), 1)
        s = _dot_nt(qs, k) * (HEAD_DIM ** -0.5) + _rep_rows(bias_fn(kt, kpos), hpk)
        m_new = jnp.maximum(m, jnp.max(s, axis=-1, keepdims=True))
        alpha = jnp.exp(m - m_new)
        p = jnp.exp(s - m_new)
        return m_new, alpha * l + jnp.sum(p, axis=-1, keepdims=True), alpha * acc + _dot(p, v)

    init = (jnp.full((rows, 1), NEG, F32), jnp.zeros((rows, 1), F32), jnp.zeros((rows, HEAD_DIM), F32))
    m, l, acc = lax.fori_loop(lo, hi, body, init)
    return acc / jnp.maximum(l, 1e-30)


def _gate_cols(gt, branch, hpk):
    return jnp.concatenate([gt[:, h * 3 + branch:h * 3 + branch + 1] for h in range(hpk)], axis=0)


def _nsa_prompt_kernel(q_ref, gate_ref, kc_ref, vc_ref, ks_ref, vs_ref, kw_ref, vw_ref, selmap_ref, expand_ref,
                       o_ref, *, tq, tk, wspan, n_cmp, n_slc, n_sel):
    qi = pl.program_id(2)
    hpk = B_HEADS_PER_KV
    qs = _stack_heads(q_ref[...]).astype(BF16)
    pos = qi * tq + lax.broadcasted_iota(I32, (tq, 1), 0)
    o_cmp, sel = _cmp_and_select(qs, pos, kc_ref[...], vc_ref[...], selmap_ref[...],
                                 n_cmp=n_cmp, n_slc=n_slc, n_sel=n_sel)
    sel_bf = sel.astype(BF16)
    hi = (qi * tq + tq - 1) // tk + 1

    def bias_slc(kt, kpos):
        picked = jnp.dot(sel_bf, expand_ref[kt], preferred_element_type=F32)
        return jnp.where(kpos <= pos, (picked - 1.0) * (-NEG), NEG)

    o_slc = _flash_tiles(qs, ks_ref, vs_ref, 0, hi, tk, hpk, bias_slc)

    start = pl.multiple_of(jnp.maximum(qi * tq + tq - wspan, 0), 8)
    kpos = start + lax.broadcasted_iota(I32, (tq, wspan), 1)
    rel = pos - kpos
    bias_w = _rep_rows(jnp.where(rel >= 0, jnp.where(rel < B_WINDOW, 0.0, NEG), NEG), hpk)
    s = _dot_nt(qs, kw_ref[pl.ds(start, wspan), :]) * (HEAD_DIM ** -0.5) + bias_w
    p = jnp.exp(s - jnp.max(s, axis=-1, keepdims=True))
    o_win = _dot(p, vw_ref[pl.ds(start, wspan), :]) / jnp.sum(p, axis=-1, keepdims=True)

    gt = gate_ref[...]
    o = _gate_cols(gt, 0, hpk) * o_cmp + _gate_cols(gt, 1, hpk) * o_slc + _gate_cols(gt, 2, hpk) * o_win
    for h in range(hpk):
        o_ref[:, h * HEAD_DIM:(h + 1) * HEAD_DIM] = o[h * tq:(h + 1) * tq]


def nsa_prompt(st, q, gates, kvs, cmp_kv):
    t = st.t
    tq = _pick_tile(t, 128, 8)
    tk = _pick_tile(t, 512, 8)
    wspan = min(t, -(-(tq + B_WINDOW - 1) // LANES) * LANES)
    nq = t // tq
    ncp = cmp_kv.shape[3]
    n_cmp = t // CMP_STRIDE - CMP_LEN // CMP_STRIDE + 1
    n_slc = -(-t // SLC_LEN)
    ns = -(-n_slc // LANES) * LANES
    selmap = jnp.asarray(_selection_map(ncp, n_cmp, n_slc, ns), BF16)
    expand = np.zeros((t // tk, ns, tk), np.float32)
    kk = np.arange(t)
    expand[kk // tk, kk // SLC_LEN, kk % tk] = 1.0
    gw = B_HEADS_PER_KV * HEAD_DIM
    kv_blk = lambda col: pl.BlockSpec((t, HEAD_DIM), lambda b, g, i: (b, col + g))
    cmp_blk = lambda kind: pl.BlockSpec((None, None, None, ncp, HEAD_DIM), lambda b, g, i: (b, kind, g, 0, 0))
    return pl.pallas_call(
        functools.partial(_nsa_prompt_kernel, tq=tq, tk=tk, wspan=wspan, n_cmp=n_cmp, n_slc=n_slc, n_sel=min(N_SELECT, n_slc)),
        out_shape=jax.ShapeDtypeStruct((st.r, B_HEADS * HEAD_DIM), F32),
        grid=(st.b, B_KV_HEADS, nq),
        in_specs=[pl.BlockSpec((tq, gw), lambda b, g, i: (b * nq + i, g)),
                  pl.BlockSpec((tq, LANES), lambda b, g, i: (b * nq + i, g)),
                  cmp_blk(0), cmp_blk(1),
                  kv_blk(2 * B_KV_HEADS), kv_blk(3 * B_KV_HEADS), kv_blk(4 * B_KV_HEADS), kv_blk(5 * B_KV_HEADS),
                  pl.BlockSpec(selmap.shape, lambda b, g, i: (0, 0)),
                  pl.BlockSpec(expand.shape, lambda b, g, i: (0, 0, 0))],
        out_specs=pl.BlockSpec((tq, gw), lambda b, g, i: (b * nq + i, g)),
        compiler_params=_params(("arbitrary", "arbitrary", "arbitrary")),
        name="nsa_prompt",
    )(q, gates, cmp_kv, cmp_kv, kvs, kvs, kvs, kvs, selmap, jnp.asarray(expand, BF16))


def _nsa_s1_kernel(q_ref, kc_ref, vc_ref, selmap_ref, ocmp_ref, sel_ref, *, tq, past_len, n_cmp, n_slc, n_sel):
    qs = _stack_heads(q_ref[...]).astype(BF16)
    pos = past_len + lax.broadcasted_iota(I32, (tq, 1), 0)
    o_cmp, sel = _cmp_and_select(qs, pos, kc_ref[...], vc_ref[...], selmap_ref[...],
                                 n_cmp=n_cmp, n_slc=n_slc, n_sel=n_sel)
    ocmp_ref[...] = o_cmp
    sel_ref[...] = sel


def _nsa_s2_kernel(*refs, tq, past_len, pps):
    q_ref = refs[1]
    page_refs = refs[2:2 + pps]
    sel_ref, new_ref, o_ref, m_scr, l_scr, acc_scr = refs[2 + pps:]
    p = pl.program_id(1)
    n_steps = pl.num_programs(1)
    hpk = B_HEADS_PER_KV
    gw = B_KV_HEADS * HEAD_DIM
    scale = HEAD_DIM ** -0.5
    ns = sel_ref.shape[-1]

    @pl.when(p == 0)
    def _():
        m_scr[...] = jnp.full(m_scr.shape, NEG, F32)
        l_scr[...] = jnp.zeros(l_scr.shape, F32)
        acc_scr[...] = jnp.zeros(acc_scr.shape, F32)

    q = q_ref[...]

    def update(g, s, ok, v):
        s = jnp.where(ok, s, NEG)
        m_old = m_scr[g]
        m_new = jnp.maximum(m_old, jnp.max(s, axis=-1, keepdims=True))
        alpha = jnp.exp(m_old - m_new)
        pr = jnp.where(ok, jnp.exp(s - m_new), 0.0)
        l_scr[g] = alpha * l_scr[g] + jnp.sum(pr, axis=-1, keepdims=True)
        acc_scr[g] = alpha * acc_scr[g] + _dot(pr, v)
        m_scr[g] = m_new

    nk = pps * PAGE_SIZE
    nn = lax.broadcasted_iota(I32, (ns, nk), 0)
    jj = lax.broadcasted_iota(I32, (ns, nk), 1)
    expand = jnp.where(nn == p * (nk // SLC_LEN) + jj // SLC_LEN, 1.0, 0.0).astype(BF16)
    for g in range(B_KV_HEADS):
        qs = _stack_heads(q[:, g * hpk * HEAD_DIM:(g + 1) * hpk * HEAD_DIM]).astype(BF16)
        k = jnp.concatenate([pr[:, 0, g, :] for pr in page_refs], axis=0)
        v = jnp.concatenate([pr[:, 1, g, :] for pr in page_refs], axis=0)
        picked = jnp.dot(sel_ref[g].astype(BF16), expand, preferred_element_type=F32)
        update(g, _dot_nt(qs, k) * scale, _rep_rows(picked, hpk) > 0.5, v)

    @pl.when(p == n_steps - 1)
    def _():
        nn2 = lax.broadcasted_iota(I32, (ns, tq), 0)
        tt2 = lax.broadcasted_iota(I32, (ns, tq), 1)
        expand2 = jnp.where(nn2 == (past_len + tt2) // SLC_LEN, 1.0, 0.0).astype(BF16)
        t_q = _rep_rows(lax.broadcasted_iota(I32, (tq, tq), 0), hpk)
        t_k = _rep_rows(lax.broadcasted_iota(I32, (tq, tq), 1), hpk)
        for g in range(B_KV_HEADS):
            qs = _stack_heads(q[:, g * hpk * HEAD_DIM:(g + 1) * hpk * HEAD_DIM]).astype(BF16)
            k = new_ref[:, g * HEAD_DIM:(g + 1) * HEAD_DIM]
            v = new_ref[:, gw + g * HEAD_DIM:gw + (g + 1) * HEAD_DIM]
            picked = jnp.dot(sel_ref[g].astype(BF16), expand2, preferred_element_type=F32)
            update(g, _dot_nt(qs, k) * scale, (t_k <= t_q) & (_rep_rows(picked, hpk) > 0.5), v)
            o_ref[g] = acc_scr[g] / jnp.maximum(l_scr[g], 1e-30)


def _nsa_s3_kernel(q_ref, gate_ref, win_ref, new_ref, ocmp_ref, oslc_ref, o_ref, *, tq):
    hpk = B_HEADS_PER_KV
    gw = B_KV_HEADS * HEAD_DIM
    scale = HEAD_DIM ** -0.5
    lb = win_ref.shape[0]
    rows = hpk * tq
    q = q_ref[...]
    gt_all = gate_ref[...]
    t_q1 = _rep_rows(lax.broadcasted_iota(I32, (tq, lb), 0), hpk)
    i_k1 = _rep_rows(lax.broadcasted_iota(I32, (tq, lb), 1), hpk)
    ok1 = (t_q1 + lb - i_k1) < B_WINDOW
    t_q2 = _rep_rows(lax.broadcasted_iota(I32, (tq, tq), 0), hpk)
    t_k2 = _rep_rows(lax.broadcasted_iota(I32, (tq, tq), 1), hpk)
    ok2 = t_k2 <= t_q2
    for g in range(B_KV_HEADS):
        qs = _stack_heads(q[:, g * hpk * HEAD_DIM:(g + 1) * hpk * HEAD_DIM]).astype(BF16)
        s1 = jnp.where(ok1, _dot_nt(qs, win_ref[:, g * HEAD_DIM:(g + 1) * HEAD_DIM]) * scale, NEG)
        s2 = jnp.where(ok2, _dot_nt(qs, new_ref[:, g * HEAD_DIM:(g + 1) * HEAD_DIM]) * scale, NEG)
        m = jnp.maximum(jnp.max(s1, axis=-1, keepdims=True), jnp.max(s2, axis=-1, keepdims=True))
        p1 = jnp.where(ok1, jnp.exp(s1 - m), 0.0)
        p2 = jnp.where(ok2, jnp.exp(s2 - m), 0.0)
        den = jnp.sum(p1, axis=-1, keepdims=True) + jnp.sum(p2, axis=-1, keepdims=True)
        o_win = (_dot(p1, win_ref[:, gw + g * HEAD_DIM:gw + (g + 1) * HEAD_DIM])
                 + _dot(p2, new_ref[:, gw + g * HEAD_DIM:gw + (g + 1) * HEAD_DIM])) / jnp.maximum(den, 1e-30)
        gt = gt_all[:, g * LANES:(g + 1) * LANES]
        o = (_gate_cols(gt, 0, hpk) * ocmp_ref[g] + _gate_cols(gt, 1, hpk) * oslc_ref[g]
             + _gate_cols(gt, 2, hpk) * o_win)
        for h in range(hpk):
            col = (g * hpk + h) * HEAD_DIM
            o_ref[:, col:col + HEAD_DIM] = o[h * tq:(h + 1) * tq]


def nsa_sample(st, q, gates, kvs, cmp_kv, cache, page_table, state_win, past_len):
    nb, tq = st.b, st.t
    hpk = B_HEADS_PER_KV
    rows = hpk * tq
    ncp = cmp_kv.shape[3]
    l_all = past_len + tq
    n_cmp = l_all // CMP_STRIDE - CMP_LEN // CMP_STRIDE + 1
    n_slc = -(-l_all // SLC_LEN)
    ns = -(-n_slc // LANES) * LANES
    assert n_cmp <= ncp
    selmap = jnp.asarray(_selection_map(ncp, n_cmp, n_slc, ns), BF16)
    gw = hpk * HEAD_DIM
    cmp_blk = lambda kind: pl.BlockSpec((None, None, None, ncp, HEAD_DIM), lambda b, g: (b, kind, g, 0, 0))
    o_cmp, sel = pl.pallas_call(
        functools.partial(_nsa_s1_kernel, tq=tq, past_len=past_len, n_cmp=n_cmp, n_slc=n_slc,
                          n_sel=min(N_SELECT, n_slc)),
        out_shape=(jax.ShapeDtypeStruct((nb, B_KV_HEADS, rows, HEAD_DIM), F32),
                   jax.ShapeDtypeStruct((nb, B_KV_HEADS, tq, ns), F32)),
        grid=(nb, B_KV_HEADS),
        in_specs=[pl.BlockSpec((tq, gw), lambda b, g: (b, g)), cmp_blk(0), cmp_blk(1),
                  pl.BlockSpec(selmap.shape, lambda b, g: (0, 0))],
        out_specs=(pl.BlockSpec((None, None, rows, HEAD_DIM), lambda b, g: (b, g, 0, 0)),
                   pl.BlockSpec((None, None, tq, ns), lambda b, g: (b, g, 0, 0))),
        compiler_params=_params(("arbitrary", "arbitrary")),
        name="nsa_sample_cmp",
    )(q, cmp_kv, cmp_kv, selmap)

    n_pages = page_table.shape[1]
    half = 2 * B_KV_HEADS * HEAD_DIM
    grp = pl.BlockSpec((None, B_KV_HEADS, rows, HEAD_DIM), lambda b, *_: (b, 0, 0, 0))
    pps = _pick_tile(n_pages, 8, 1)
    page_spec = lambda j: pl.BlockSpec((None, PAGE_SIZE, 2, B_KV_HEADS, HEAD_DIM),
                                       lambda b, p, pt: (pt[b * n_pages + p * pps + j], 0, 1, 0, 0))
    o_slc = pl.pallas_call(
        functools.partial(_nsa_s2_kernel, tq=tq, past_len=past_len, pps=pps),
        out_shape=jax.ShapeDtypeStruct((nb, B_KV_HEADS, rows, HEAD_DIM), F32),
        grid_spec=pltpu.PrefetchScalarGridSpec(
            num_scalar_prefetch=1,
            grid=(nb, n_pages // pps),
            in_specs=[pl.BlockSpec((tq, B_HEADS * HEAD_DIM), lambda b, p, pt: (b, 0))]
            + [page_spec(j) for j in range(pps)]
            + [pl.BlockSpec((None, B_KV_HEADS, tq, ns), lambda b, p, pt: (b, 0, 0, 0)),
               pl.BlockSpec((tq, half), lambda b, p, pt: (b, 1))],
            out_specs=grp,
            scratch_shapes=[pltpu.VMEM((B_KV_HEADS, rows, 1), F32), pltpu.VMEM((B_KV_HEADS, rows, 1), F32),
                            pltpu.VMEM((B_KV_HEADS, rows, HEAD_DIM), F32)]),
        compiler_params=_params(("arbitrary", "arbitrary")),
        name="nsa_sample_slc",
    )(page_table.reshape(-1), q, *([cache] * pps), sel, kvs)

    lb = state_win.shape[1]
    win2 = state_win.reshape(nb, lb, 2 * B_KV_HEADS * HEAD_DIM)
    return pl.pallas_call(
        functools.partial(_nsa_s3_kernel, tq=tq),
        out_shape=jax.ShapeDtypeStruct((st.r, B_HEADS * HEAD_DIM), F32),
        grid=(nb,),
        in_specs=[pl.BlockSpec((tq, B_HEADS * HEAD_DIM), lambda b: (b, 0)),
                  pl.BlockSpec((tq, B_KV_HEADS * LANES), lambda b: (b, 0)),
                  pl.BlockSpec((None, lb, half), lambda b: (b, 0, 0)),
                  pl.BlockSpec((tq, half), lambda b: (b, 2)),
                  grp, grp],
        out_specs=pl.BlockSpec((tq, B_HEADS * HEAD_DIM), lambda b: (b, 0)),
        compiler_params=_params(("arbitrary",)),
        name="nsa_sample_win",
    )(q, gates, win2, kvs, o_cmp, o_slc)


def kernel(x_prompt, x_sample, state_dil0, state_dil1, state_dil2, cache_nsa, state_win, page_table, c_prompt, c_sample, w_mod, b_mod, g_norm, w_qkv_a, w_o_a, qn_a, kn_a, w_q_b, qn_b, w_gate_b, b_gate_b, w_o_b, w_mod_kv, b_mod_kv, g_kv, w_kv, kn_kv, cmp_pe, cmp_w1, cmp_b1, cmp_w2, cmp_b2, router_w, router_b, w_gu, b_gu, w_down, b_down):
    bp, tp, d = x_prompt.shape
    bs, ts, _ = x_sample.shape
    past_len = page_table.shape[1] * PAGE_SIZE
    stp = Stream(bp, tp, jnp.arange(tp), 512)
    sts = Stream(bs, ts, past_len + jnp.arange(ts), 512)
    streams = [stp, sts]
    n_pages = page_table.shape[1]
    depth = w_mod.shape[0]
    n_a = w_qkv_a.shape[0]
    ng = len(DIL_RATES)
    dil_bufs = (state_dil0, state_dil1, state_dil2)

    c_all = jnp.concatenate([c_prompt, c_sample], axis=0)
    m_all = mod_vectors(c_all, w_mod, b_mod)
    m_kv = mod_vectors(c_all, w_mod_kv[None], b_mod_kv[None])[0]

    def split(m):
        return [stp.mods(m[:bp]), sts.mods(m[bp:])]

    xs = [x_prompt.reshape(-1, d), x_sample.reshape(-1, d)]
    a_qkv = []
    kvs = cmp_ctx = None
    for l in range(depth):
        mods = split(m_all[l])
        g_attn = g_norm[l, 0]
        if l < n_a:
            wq, wo = w_qkv_a[l].astype(BF16), w_o_a[l].astype(BF16)
            qkvs = [qkv_a(st, x, g_attn, md, wq, qn_a[l], kn_a[l]) for st, x, md in zip(streams, xs, mods)]
            a_qkv.append(qkvs)
            outs = [dilated_prompt(stp, qkvs[0]), dilated_sample(sts, qkvs[1], dil_bufs, l)]
        else:
            lb = l - n_a
            wq, wo = w_q_b[lb].astype(BF16), w_o_b[lb].astype(BF16)
            qg = [q_proj_b(st, x, g_attn, md, wq, qn_b[lb], w_gate_b[lb], b_gate_b[lb])
                  for st, x, md in zip(streams, xs, mods)]
            outs = [nsa_prompt(stp, qg[0][0], qg[0][1], kvs[0], cmp_ctx[0]),
                    nsa_sample(sts, qg[1][0], qg[1][1], kvs[1], cmp_ctx[1], cache_nsa, page_table, state_win,
                               past_len)]
        xs = [out_proj_residual(st, o, wo, x, md, 2) for st, o, x, md in zip(streams, outs, xs, mods)]
        xs = moe_layer(streams, xs, g_norm[l, 1], mods, router_w[l], router_b[l], w_gu, b_gu, w_down, b_down, l)
        if l == n_a - 1:
            wkv = w_kv.astype(BF16)
            kvs = [kv_proj(st, x, g_kv, md, wkv, kn_kv) for st, x, md in zip(streams, xs, split(m_kv))]
            w1, w2 = cmp_w1.astype(BF16), cmp_w2.astype(BF16)
            cmp_w = 2 * B_KV_HEADS * HEAD_DIM
            rows_p = _pick_tile(tp, 1024, CMP_STRIDE)
            steps_p = tp // rows_p
            pq = compress_partials(
                kvs[0], lambda j: pl.BlockSpec((rows_p, cmp_w), lambda b, i: (b * steps_p + i, 0)),
                1, rows_p, (bp, steps_p), cmp_pe, w1, tp // CMP_STRIDE)
            cmp_p = compress_finish(pq[0], pq[1], cmp_b1, w2, cmp_b2)
            assert (past_len + ts) // CMP_STRIDE == past_len // CMP_STRIDE
            pps = _pick_tile(n_pages, 8, 1)
            pq = compress_partials(
                cache_nsa,
                lambda j: pl.BlockSpec((None, PAGE_SIZE, 2, B_KV_HEADS, HEAD_DIM),
                                       lambda b, i, pt: (pt[b * n_pages + i * pps + j], 0, 0, 0, 0)),
                pps, PAGE_SIZE, (bs, n_pages // pps), cmp_pe, w1, past_len // CMP_STRIDE,
                prefetch=page_table.reshape(-1))
            cmp_s = compress_finish(pq[0], pq[1], cmp_b1, w2, cmp_b2)
            cmp_ctx = [cmp_p, cmp_s]

    def a_rows(qkv, nb, t, kind, g):
        return qkv.reshape(nb, t, 3, ng, A_HEADS, HEAD_DIM)[:, :, kind, g]

    p_dil, s_new = [], []
    for g in range(ng):
        keep = min(DIL_WINDOWS[g], tp)
        p_dil.append(jnp.stack([
            jnp.stack([a_rows(a_qkv[l][0], bp, tp, 1, g)[:, tp - keep:],
                       a_rows(a_qkv[l][0], bp, tp, 2, g)[:, tp - keep:]], axis=2) for l in range(n_a)]))
        s_new.append(jnp.stack([
            jnp.stack([a_rows(a_qkv[l][1], bs, ts, 1, g), a_rows(a_qkv[l][1], bs, ts, 2, g)], axis=2)
            for l in range(n_a)]))
    s_dil = [roll_state(buf, new, win) for buf, new, win in zip(dil_bufs, s_new, DIL_WINDOWS)]
    rows_w = 4 * B_KV_HEADS * HEAD_DIM
    p_rows = kvs[0][:, :rows_w].reshape(bp, tp, 4, B_KV_HEADS, HEAD_DIM)
    p_winr = kvs[0][:, rows_w:].reshape(bp, tp, 2, B_KV_HEADS, HEAD_DIM)
    p_win = p_winr[:, tp - min(B_WINDOW, tp):]
    s_rows = kvs[1][:, :rows_w].reshape(bs, ts, 4, B_KV_HEADS, HEAD_DIM)
    s_win = roll_state(state_win[None], kvs[1][:, rows_w:].reshape(1, bs, ts, 2, B_KV_HEADS, HEAD_DIM), B_WINDOW)[0]
    return (xs[0].reshape(bp, tp, d), xs[1].reshape(bs, ts, d), p_dil[0], p_dil[1], p_dil[2], p_rows, p_win,
            s_dil[0], s_dil[1], s_dil[2], s_rows, s_win)
```

```python
import functools

import numpy as np
import jax
import jax.numpy as jnp
from jax import lax
from jax.experimental import pallas as pl
from jax.experimental.pallas import tpu as pltpu

F32 = jnp.float32
BF16 = jnp.bfloat16
I32 = jnp.int32

HEAD_DIM = 128
LANES = 128
DIL_WINDOWS = (128, 512, 2048)
DIL_RATES = (1, 4, 16)
A_HEADS = 8
B_HEADS = 16
B_KV_HEADS = 4
B_HEADS_PER_KV = B_HEADS // B_KV_HEADS
CMP_LEN = 32
CMP_STRIDE = 16
SLC_LEN = 64
N_SELECT = 16
B_WINDOW = 512
FORCE_BONUS = 1.0e4
N_EXPERTS = 32
TOP_K = 4
SWIGLU_ALPHA = 1.702
SWIGLU_LIMIT = 7.0
MOE_BLOCK = 128
MOE_TB = 1152
PAGE_SIZE = 128
ROPE_THETA = 10000.0
NORM_EPS = 1e-6
N_MODS = 6
NEG = -1e30
VMEM_LIMIT = 56 * 1024 * 1024

_NT = (((1,), (1,)), ((), ()))


def _params(sem, **kw):
    return pltpu.CompilerParams(dimension_semantics=sem, vmem_limit_bytes=VMEM_LIMIT, **kw)


def _dot(a, b):
    return jnp.dot(a.astype(BF16), b.astype(BF16), preferred_element_type=F32)


def _dot_nt(a, b):
    return lax.dot_general(a.astype(BF16), b.astype(BF16), _NT, preferred_element_type=F32)


def _pick_tile(n, cap, mult=LANES):
    if n <= cap:
        return n
    t = (cap // mult) * mult
    while t > mult and n % t:
        t -= mult
    assert n % t == 0, (n, cap)
    return t


def _rope_tables(pos):
    half = HEAD_DIM // 2
    inv_freq = 1.0 / (ROPE_THETA ** (jnp.arange(half, dtype=F32) * 2.0 / HEAD_DIM))
    ang = pos.astype(F32)[:, None] * inv_freq[None, :]
    c, s = jnp.cos(ang), jnp.sin(ang)
    return jnp.concatenate([c, c], axis=-1), jnp.concatenate([-s, s], axis=-1)


def _modulate(x, g, shift, scale):
    y = x * lax.rsqrt(jnp.mean(x * x, axis=-1, keepdims=True) + NORM_EPS) * g
    return y * (1.0 + scale) + shift


def _head_norm_rope(blk, gain, cos2, sin2):
    y = blk * lax.rsqrt(jnp.mean(blk * blk, axis=-1, keepdims=True) + NORM_EPS) * gain
    return y * cos2 + pltpu.roll(y, HEAD_DIM // 2, 1) * sin2


def _mod_kernel(c_ref, w_ref, b_ref, o_ref):
    c = c_ref[...]
    o_ref[...] = _dot(c * jax.nn.sigmoid(c), w_ref[...]) + b_ref[...]


def mod_vectors(c, w, b):
    n_layers, d, n = w.shape
    nc = c.shape[0]
    tn = _pick_tile(n, max(LANES, (8 << 20) // (4 * d)))
    return pl.pallas_call(
        _mod_kernel,
        out_shape=jax.ShapeDtypeStruct((n_layers, nc, n), F32),
        grid=(n_layers, n // tn),
        in_specs=[pl.BlockSpec((nc, d), lambda l, j: (0, 0)),
                  pl.BlockSpec((None, d, tn), lambda l, j: (l, 0, j)),
                  pl.BlockSpec((None, 1, tn), lambda l, j: (l, 0, j))],
        out_specs=pl.BlockSpec((None, nc, tn), lambda l, j: (l, 0, j)),
        compiler_params=_params(("arbitrary", "arbitrary")),
        name="mod_vectors",
    )(c, w, b.reshape(n_layers, 1, n))


class Stream:
    def __init__(self, b, t, pos, tm_cap):
        self.b, self.t, self.r = b, t, b * t
        self.per_row_mods = t < LANES
        self.tm = self.r if self.per_row_mods else _pick_tile(t, tm_cap, 8)
        self.tiles_per_seq = 1 if self.per_row_mods else t // self.tm
        cos2, sin2 = _rope_tables(pos)
        if self.per_row_mods:
            cos2, sin2 = jnp.tile(cos2, (b, 1)), jnp.tile(sin2, (b, 1))
        self.cos2, self.sin2 = cos2, sin2

    def mods(self, m):
        if self.per_row_mods:
            return jnp.repeat(m, self.t, axis=0)[None]
        return m[:, None, :]

    def mod_spec(self, d, k, n_grid_axes=2):
        rows = self.r if self.per_row_mods else 1
        tps = self.tiles_per_seq
        return pl.BlockSpec((None, rows, d), lambda i, *_: (i // tps, 0, k))

    def rope_spec(self):
        tps = self.tiles_per_seq
        return pl.BlockSpec((self.tm, HEAD_DIM), lambda i, *_: (i % tps, 0))


def _lin_kernel(flag_ref, gidx_ref, x_ref, g_ref, sh_ref, sc_ref, w_ref, gain_ref, cos_ref, sin_ref, b_ref,
                o_ref, h_scr, *, act):
    del gidx_ref
    j = pl.program_id(1)

    @pl.when(j == 0)
    def _():
        h_scr[...] = _modulate(x_ref[...], g_ref[...], sh_ref[...], sc_ref[...]).astype(BF16)

    acc = jnp.dot(h_scr[...], w_ref[...], preferred_element_type=F32) + b_ref[...]
    tn = acc.shape[1]

    @pl.when(flag_ref[j] == 0)
    def _():
        o_ref[...] = (jax.nn.sigmoid(acc) if act == "sigmoid" else acc).astype(o_ref.dtype)

    @pl.when(flag_ref[j] != 0)
    def _():
        for hh in range(tn // HEAD_DIM):
            sl = slice(hh * HEAD_DIM, (hh + 1) * HEAD_DIM)
            o_ref[:, sl] = _head_norm_rope(acc[:, sl], gain_ref[...], cos_ref[...], sin_ref[...]).astype(o_ref.dtype)


def mod_linear(st, x, g, mods, ksh, ksc, w_bf, *, tn, name, flags=None, gidx=None, gains=None, bias=None,
               act=None):
    r, d = x.shape
    n = w_bf.shape[1]
    nj = n // tn
    if flags is None:
        flags = np.zeros((nj,), np.int32)
        gidx = np.zeros((nj,), np.int32)
        gains = jnp.ones((1, HEAD_DIM), F32)
    if bias is None:
        bias = jnp.zeros((n,), F32)
    gains = gains.reshape(-1, 1, HEAD_DIM)
    tm = st.tm
    grid_spec = pltpu.PrefetchScalarGridSpec(
        num_scalar_prefetch=2,
        grid=(r // tm, nj),
        in_specs=[pl.BlockSpec((tm, d), lambda i, j, *_: (i, 0)),
                  pl.BlockSpec((1, d), lambda i, j, *_: (0, 0)),
                  st.mod_spec(d, ksh), st.mod_spec(d, ksc),
                  pl.BlockSpec((d, tn), lambda i, j, *_: (0, j)),
                  pl.BlockSpec((None, 1, HEAD_DIM), lambda i, j, fl, gi: (gi[j], 0, 0)),
                  st.rope_spec(), st.rope_spec(),
                  pl.BlockSpec((1, tn), lambda i, j, *_: (0, j))],
        out_specs=pl.BlockSpec((tm, tn), lambda i, j, *_: (i, j)),
        scratch_shapes=[pltpu.VMEM((tm, d), BF16)])
    return pl.pallas_call(
        functools.partial(_lin_kernel, act=act),
        out_shape=jax.ShapeDtypeStruct((r, n), F32),
        grid_spec=grid_spec,
        compiler_params=_params(("arbitrary", "arbitrary")),
        name=name,
    )(jnp.asarray(flags, I32), jnp.asarray(gidx, I32), x, g.reshape(1, d), mods, mods, w_bf, gains,
      st.cos2, st.sin2, bias.reshape(1, n))


def _out_kernel(o_ref, w_ref, x_ref, gate_ref, y_ref):
    y_ref[...] = x_ref[...] + gate_ref[...] * _dot(o_ref[...], w_ref[...])


def out_proj_residual(st, o, w_bf, x, mods, kgate):
    r, kd = o.shape
    d = x.shape[1]
    tn = _pick_tile(d, 512)
    tm = st.tm
    nj = d // tn
    rows = st.r if st.per_row_mods else 1
    tps = st.tiles_per_seq
    return pl.pallas_call(
        _out_kernel,
        out_shape=jax.ShapeDtypeStruct((r, d), F32),
        grid=(r // tm, nj),
        in_specs=[pl.BlockSpec((tm, kd), lambda i, j: (i, 0)),
                  pl.BlockSpec((kd, tn), lambda i, j: (0, j)),
                  pl.BlockSpec((tm, tn), lambda i, j: (i, j)),
                  pl.BlockSpec((None, rows, tn), lambda i, j: (i // tps, 0, kgate * nj + j))],
        out_specs=pl.BlockSpec((tm, tn), lambda i, j: (i, j)),
        compiler_params=_params(("arbitrary", "arbitrary")),
        name="out_proj",
    )(o, w_bf, x, mods)


def _dil_prompt_kernel(*refs, wr, rates, sup):
    ng = len(rates)
    o_ref = refs[5 * ng]
    kf, vf, m_scr, l_scr, acc_scr = refs[5 * ng + 1:]
    i = pl.program_id(1)
    scale = HEAD_DIM ** -0.5
    qi = lax.broadcasted_iota(I32, (wr, 2 * wr), 0)
    kk = lax.broadcasted_iota(I32, (wr, 2 * wr), 1)
    band = (kk >= qi) & (kk <= qi + wr)
    for g, rate in enumerate(rates):
        q_ref, kp_ref, kc_ref, vp_ref, vc_ref = refs[5 * g:5 * g + 5]
        prev = wr * rate
        kf[0:prev, :] = kp_ref[...]
        kf[prev:prev + sup, :] = kc_ref[...]
        vf[0:prev, :] = vp_ref[...]
        vf[prev:prev + sup, :] = vc_ref[...]

        def body(idx, carry, q_ref=q_ref, rate=rate, g=g):
            c = idx % rate
            qb = idx // rate
            start = c + rate * qb * wr
            if rate == 1:
                q = q_ref[pl.ds(start, wr), :]
                k = kf[pl.ds(start, 2 * wr), :]
                v = vf[pl.ds(start, 2 * wr), :]
            else:
                q = q_ref[pl.ds(start, wr, stride=rate), :]
                k = kf[pl.ds(start, 2 * wr, stride=rate), :]
                v = vf[pl.ds(start, 2 * wr, stride=rate), :]
            s = _dot_nt(q, k) * scale
            ok = band & ((kk >= wr) | (qb > 0) | (i > 0))
            s = jnp.where(ok, s, NEG)
            m = jnp.max(s, axis=-1, keepdims=True)
            p = jnp.exp(s - m)
            l = jnp.sum(p, axis=-1, keepdims=True)
            acc = _dot(p, v)
            rows = pl.ds(start, wr) if rate == 1 else pl.ds(start, wr, stride=rate)
            if g == 0:
                m_scr[rows, :] = jnp.broadcast_to(m, (wr, HEAD_DIM))
                l_scr[rows, :] = jnp.broadcast_to(l, (wr, HEAD_DIM))
                acc_scr[rows, :] = acc
            else:
                m_old = m_scr[rows, :]
                m_new = jnp.maximum(m_old, m)
                a = jnp.exp(m_old - m_new)
                b = jnp.exp(m - m_new)
                l_scr[rows, :] = a * l_scr[rows, :] + b * l
                acc_scr[rows, :] = a * acc_scr[rows, :] + b * acc
                m_scr[rows, :] = m_new
            return carry

        n_blk = sup // wr
        lax.fori_loop(0, n_blk, body, 0, unroll=8 if n_blk % 8 == 0 else 1)
    o_ref[...] = acc_scr[...] / l_scr[...]


def dilated_prompt(st, qkv):
    ng = len(DIL_RATES)
    wr = DIL_WINDOWS[0] // DIL_RATES[0]
    assert all(w // r == wr for w, r in zip(DIL_WINDOWS, DIL_RATES))
    sup = wr * max(DIL_RATES)
    t = st.t
    assert t % sup == 0
    nsup = t // sup
    kind_cols = ng * A_HEADS
    in_specs, args = [], []
    for g, rate in enumerate(DIL_RATES):
        prev = wr * rate
        ratio = sup // prev
        qcol = lambda h, g=g: g * A_HEADS + h
        cur = lambda kind, g=g: pl.BlockSpec(
            (sup, HEAD_DIM), lambda b, i, h: (b * nsup + i, kind * kind_cols + g * A_HEADS + h))
        prv = lambda kind, g=g, ratio=ratio, prev=prev: pl.BlockSpec(
            (prev, HEAD_DIM),
            lambda b, i, h: (jnp.maximum((b * nsup + i) * ratio - 1, 0), kind * kind_cols + g * A_HEADS + h))
        in_specs += [cur(0), prv(1), cur(1), prv(2), cur(2)]
        args += [qkv] * 5
    max_prev = wr * max(DIL_RATES)
    return pl.pallas_call(
        functools.partial(_dil_prompt_kernel, wr=wr, rates=DIL_RATES, sup=sup),
        out_shape=jax.ShapeDtypeStruct((st.r, A_HEADS * HEAD_DIM), F32),
        grid=(st.b, nsup, A_HEADS),
        in_specs=in_specs,
        out_specs=pl.BlockSpec((sup, HEAD_DIM), lambda b, i, h: (b * nsup + i, h)),
        scratch_shapes=[pltpu.VMEM((max_prev + sup, HEAD_DIM), F32), pltpu.VMEM((max_prev + sup, HEAD_DIM), F32),
                        pltpu.VMEM((sup, HEAD_DIM), F32), pltpu.VMEM((sup, HEAD_DIM), F32),
                        pltpu.VMEM((sup, HEAD_DIM), F32)],
        compiler_params=_params(("arbitrary", "arbitrary", "arbitrary")),
        name="dilated_prompt",
    )(*args)


def _dil_sample_kernel(*refs, wr, rates, windows):
    ng = len(rates)
    o_ref = refs[5 * ng]
    scale = HEAD_DIM ** -0.5
    m_run = l_run = acc_run = None
    for g, (rate, win) in enumerate(zip(rates, windows)):
        q_ref, kn_ref, vn_ref, ks_ref, vs_ref = refs[5 * g:5 * g + 5]
        q = q_ref[...]
        tq = q.shape[0]
        lb = ks_ref.shape[0]
        s1 = _dot_nt(q, ks_ref[...]) * scale
        d1 = lb + lax.broadcasted_iota(I32, (tq, lb), 0) - lax.broadcasted_iota(I32, (tq, lb), 1)
        ok1 = (d1 % rate == 0) & (d1 <= win)
        s1 = jnp.where(ok1, s1, NEG)
        s2 = _dot_nt(q, kn_ref[...]) * scale
        d2 = lax.broadcasted_iota(I32, (tq, tq), 0) - lax.broadcasted_iota(I32, (tq, tq), 1)
        ok2 = (d2 >= 0) & (d2 % rate == 0) & (d2 <= win)
        s2 = jnp.where(ok2, s2, NEG)
        m = jnp.maximum(jnp.max(s1, axis=-1, keepdims=True), jnp.max(s2, axis=-1, keepdims=True))
        p1 = jnp.exp(s1 - m)
        p2 = jnp.exp(s2 - m)
        l = jnp.sum(p1, axis=-1, keepdims=True) + jnp.sum(p2, axis=-1, keepdims=True)
        acc = _dot(p1, vs_ref[...]) + _dot(p2, vn_ref[...])
        if g == 0:
            m_run, l_run, acc_run = m, l, acc
        else:
            m_new = jnp.maximum(m_run, m)
            a = jnp.exp(m_run - m_new)
            b = jnp.exp(m - m_new)
            l_run = a * l_run + b * l
            acc_run = a * acc_run + b * acc
            m_run = m_new
    o_ref[...] = acc_run / l_run


def dilated_sample(st, qkv, bufs, layer):
    ng = len(DIL_RATES)
    wr = DIL_WINDOWS[0] // DIL_RATES[0]
    t = st.t
    kind_cols = ng * A_HEADS
    in_specs, args = [], []
    for g in range(ng):
        buf = bufs[g]
        lb = buf.shape[2]
        assert lb == DIL_WINDOWS[g]
        buf2 = buf.reshape(buf.shape[0], buf.shape[1], lb, 2 * A_HEADS * HEAD_DIM)
        new = lambda kind, g=g: pl.BlockSpec((t, HEAD_DIM), lambda b, h: (b, kind * kind_cols + g * A_HEADS + h))
        old = lambda kv: pl.BlockSpec((None, None, lb, HEAD_DIM), lambda b, h: (layer, b, 0, kv * A_HEADS + h))
        in_specs += [new(0), new(1), new(2), old(0), old(1)]
        args += [qkv, qkv, qkv, buf2, buf2]
    return pl.pallas_call(
        functools.partial(_dil_sample_kernel, wr=wr, rates=DIL_RATES, windows=DIL_WINDOWS),
        out_shape=jax.ShapeDtypeStruct((st.r, A_HEADS * HEAD_DIM), F32),
        grid=(st.b, A_HEADS),
        in_specs=in_specs,
        out_specs=pl.BlockSpec((t, HEAD_DIM), lambda b, h: (b, h)),
        compiler_params=_params(("arbitrary", "arbitrary")),
        name="dilated_sample",
    )(*args)


def _roll_kernel(cur_ref, nxt_ref, new_ref, out_ref):
    i = pl.program_id(2)
    rb, t = cur_ref.shape[0], new_ref.shape[0]
    out_ref[0:rb - t] = cur_ref[t:rb]

    @pl.when(i < pl.num_programs(2) - 1)
    def _():
        out_ref[rb - t:rb] = nxt_ref[...]

    @pl.when(i == pl.num_programs(2) - 1)
    def _():
        out_ref[rb - t:rb] = new_ref[...]


def roll_state(state, new, win):
    lb, t = state.shape[2], new.shape[2]
    keep = min(win, lb + t)
    rb = _pick_tile(lb, 512, t) if lb % t == 0 else 0
    if keep != lb or not rb:
        return jnp.concatenate([state, new], axis=2)[:, :, lb + t - keep:]
    tail = state.shape[3:]
    zeros = (0,) * len(tail)
    per = rb // t
    blk = lambda rows, fn: pl.BlockSpec((None, None, rows) + tail, lambda l, b, i: (l, b, fn(i)) + zeros)
    return pl.pallas_call(
        _roll_kernel,
        out_shape=jax.ShapeDtypeStruct(state.shape, state.dtype),
        grid=state.shape[:2] + (lb // rb,),
        in_specs=[blk(rb, lambda i: i),
                  blk(t, lambda i: jnp.minimum((i + 1) * per, lb // t - 1)),
                  blk(t, lambda i: 0)],
        out_specs=blk(rb, lambda i: i),
        compiler_params=_params(("arbitrary", "arbitrary", "arbitrary")),
        name="roll_state",
    )(state, state, new)


def qkv_a(st, x, g, mods, w_bf, qn, kn):
    n = w_bf.shape[1]
    per_kind = n // 3
    tn = _pick_tile(per_kind, 512)
    kind = np.arange(n // tn) // (per_kind // tn)
    return mod_linear(st, x, g, mods, 0, 1, w_bf, tn=tn, name="qkv_a", flags=(kind < 2).astype(np.int32),
                      gidx=np.minimum(kind, 1).astype(np.int32), gains=jnp.stack([qn, kn]))


META_E, META_W, META_RANK = 0, TOP_K, 2 * TOP_K


def _route_kernel(x_ref, g_ref, sh_ref, sc_ref, rw_ref, rb_ref, cin_ref, h_ref, meta_ref, cnt_ref, carry):
    i = pl.program_id(0)

    @pl.when(i == 0)
    def _():
        carry[...] = cin_ref[...]

    h = _modulate(x_ref[...], g_ref[...], sh_ref[...], sc_ref[...])
    h_ref[...] = h
    tm = h.shape[0]
    logits = _dot(h, rw_ref[...]) + rb_ref[...]
    lane = lax.broadcasted_iota(I32, (tm, LANES), 1)
    work = logits
    sel = jnp.zeros((tm, LANES), F32)
    idxs, vals = [], []
    for _ in range(TOP_K):
        v = jnp.max(work, axis=-1, keepdims=True)
        idx = jnp.min(jnp.where(work == v, lane, LANES), axis=-1, keepdims=True)
        hit = lane == idx
        sel = jnp.where(hit, 1.0, sel)
        work = jnp.where(hit, -jnp.inf, work)
        idxs.append(idx)
        vals.append(v)
    es = [jnp.exp(v - vals[0]) for v in vals]
    den = es[0] + es[1] + es[2] + es[3]
    rr = lax.broadcasted_iota(I32, (tm, tm), 0)
    cc = lax.broadcasted_iota(I32, (tm, tm), 1)
    tri = jnp.where(cc < rr, 1.0, 0.0)
    rank = carry[...] + _dot(tri, sel)
    meta = jnp.zeros((tm, LANES), F32)
    for k in range(TOP_K):
        rk = jnp.sum(jnp.where(lane == idxs[k], rank, 0.0), axis=-1, keepdims=True)
        meta = jnp.where(lane == META_E + k, idxs[k].astype(F32), meta)
        meta = jnp.where(lane == META_W + k, es[k] / den, meta)
        meta = jnp.where(lane == META_RANK + k, rk, meta)
    meta_ref[...] = meta
    carry[...] = carry[...] + jnp.sum(sel, axis=0, keepdims=True)
    cnt_ref[...] = carry[...]


def moe_route(st, x, g, mods, rw_bf, rb, counts_in):
    r, d = x.shape
    tm = st.tm
    return pl.pallas_call(
        _route_kernel,
        out_shape=(jax.ShapeDtypeStruct((r, d), F32), jax.ShapeDtypeStruct((r, LANES), F32),
                   jax.ShapeDtypeStruct((1, LANES), F32)),
        grid=(r // tm,),
        in_specs=[pl.BlockSpec((tm, d), lambda i: (i, 0)),
                  pl.BlockSpec((1, d), lambda i: (0, 0)),
                  st.mod_spec(d, 3), st.mod_spec(d, 4),
                  pl.BlockSpec((d, LANES), lambda i: (0, 0)),
                  pl.BlockSpec((1, LANES), lambda i: (0, 0)),
                  pl.BlockSpec((1, LANES), lambda i: (0, 0))],
        out_specs=(pl.BlockSpec((tm, d), lambda i: (i, 0)),
                   pl.BlockSpec((tm, LANES), lambda i: (i, 0)),
                   pl.BlockSpec((1, LANES), lambda i: (0, 0))),
        scratch_shapes=[pltpu.VMEM((1, LANES), F32)],
        compiler_params=_params(("arbitrary",)),
        name="moe_route",
    )(x, g.reshape(1, d), mods, mods, rw_bf, rb, counts_in)


def _dispatch_kernel(dest_ref, pend_ref, *refs, tms, tiles, n_exp, n_rows):
    h_refs = refs[:len(tms)]
    xb_out, zero_tile, sem = refs[len(tms):]
    i = pl.program_id(0)

    @pl.when(i == 0)
    def _():
        zero_tile[...] = jnp.zeros(zero_tile.shape, zero_tile.dtype)

        def clear(start):
            return pltpu.make_async_copy(
                zero_tile, xb_out.at[pl.ds(pl.multiple_of(start, MOE_BLOCK), MOE_BLOCK)], sem)

        total = pend_ref[n_exp - 1]
        for phase in ("start", "wait"):
            for e in range(n_exp):
                prev = pend_ref[e - 1] if e else 0

                @pl.when(pend_ref[e] > prev)
                def _(e=e):
                    getattr(clear(pend_ref[e] - MOE_BLOCK), phase)()

            def tail(j, carry, phase=phase):
                getattr(clear(total + j * MOE_BLOCK), phase)()
                return carry

            lax.fori_loop(0, (n_rows - total) // MOE_BLOCK, tail, 0)

    first_tile = first_row = 0
    for h_ref, tm, nt in zip(h_refs, tms, tiles):
        @pl.when((i >= first_tile) & (i < first_tile + nt))
        def _(h_ref=h_ref, tm=tm, first_tile=first_tile, first_row=first_row):
            row0 = first_row + (i - first_tile) * tm

            def row(t, carry):
                for k in range(TOP_K):
                    dst = dest_ref[(row0 + t) * TOP_K + k]
                    pltpu.make_async_copy(h_ref.at[pl.ds(t, 1)], xb_out.at[pl.ds(dst, 1)], sem).start()
                return carry

            lax.fori_loop(0, tm, row, 0)

            def drain(t, carry):
                for k in range(TOP_K):
                    pltpu.make_async_copy(h_ref.at[pl.ds(0, 1)], xb_out.at[pl.ds(0, 1)], sem).wait()
                return carry

            lax.fori_loop(0, tm, drain, 0)

        first_tile += nt
        first_row += nt * tm


def moe_dispatch(streams, hs, dest, pend, n_rows):
    d = hs[0].shape[1]
    tms = tuple(st.tm for st in streams)
    tiles = tuple(st.r // st.tm for st in streams)
    firsts = tuple(int(v) for v in np.cumsum((0,) + tiles[:-1]))

    def h_spec(tm, first, nt):
        return pl.BlockSpec((tm, d), lambda i, *_: (jnp.clip(i - first, 0, nt - 1), 0))

    grid_spec = pltpu.PrefetchScalarGridSpec(
        num_scalar_prefetch=2, grid=(sum(tiles),),
        in_specs=[h_spec(tm, first, nt) for tm, first, nt in zip(tms, firsts, tiles)],
        out_specs=pl.BlockSpec(memory_space=pl.ANY),
        scratch_shapes=[pltpu.VMEM((MOE_BLOCK, d), F32), pltpu.SemaphoreType.DMA(())])
    return pl.pallas_call(
        functools.partial(_dispatch_kernel, tms=tms, tiles=tiles, n_exp=pend.shape[0], n_rows=n_rows),
        out_shape=jax.ShapeDtypeStruct((n_rows, d), F32), grid_spec=grid_spec,
        compiler_params=_params(("arbitrary",), has_side_effects=True), name="moe_dispatch",
    )(dest, pend, *hs)


def _ffn_kernel(e_ref, row_ref, nsub_ref, xb_hbm, *refs, tb, n_split):
    wgu_refs = refs[:n_split]
    (bgu_ref, wdn_ref, bdn_ref, perm_ref, yb_in, yb_hbm,
     xs, xsb, acc, wg_bf, wd_bf, sem_in, sem_out) = refs[n_split:]
    del e_ref, yb_in
    w = pl.program_id(0)
    f = pl.program_id(1)
    nw = pl.num_programs(0)
    nf = pl.num_programs(1)
    nsub = nsub_ref[w]
    live = nsub > 0
    row0 = pl.multiple_of(row_ref[w], MOE_BLOCK)

    def x_copy(item):
        return pltpu.make_async_copy(xb_hbm.at[pl.ds(pl.multiple_of(row_ref[item], MOE_BLOCK), tb)], xs, sem_in)

    @pl.when((w == 0) & (f == 0) & live)
    def _():
        x_copy(0).start()

    @pl.when((f == 0) & live)
    def _():
        x_copy(w).wait()
        xsb[...] = xs[...].astype(BF16)
        acc[...] = jnp.broadcast_to(bdn_ref[...], acc.shape)

    nxt = jnp.minimum(w + 1, nw - 1)

    @pl.when((f == nf - 1) & (w + 1 < nw) & (nsub_ref[nxt] > 0))
    def _():
        x_copy(nxt).start()

    @pl.when(live)
    def _():
        dq = wg_bf.shape[0] // n_split
        for q, wgu_ref in enumerate(wgu_refs):
            wg_bf[q * dq:(q + 1) * dq, :] = wgu_ref[...].astype(BF16)
        wd_bf[...] = wdn_ref[...].astype(BF16)
        gu = jnp.dot(xsb[...], wg_bf[...], preferred_element_type=F32) + bgu_ref[...]
        lane = lax.broadcasted_iota(I32, gu.shape, 1)
        gate = jnp.minimum(gu, SWIGLU_LIMIT)
        glu = gate * jax.nn.sigmoid(SWIGLU_ALPHA * gate)
        up = jnp.clip(gu, -SWIGLU_LIMIT, SWIGLU_LIMIT) + 1.0
        prod = jnp.where((lane % 2) == 1, up * pltpu.roll(glu, 1, 1), 0.0).astype(BF16)
        act = jnp.dot(prod, perm_ref[...], preferred_element_type=F32).astype(BF16)
        acc[...] += jnp.dot(act, wd_bf[...], preferred_element_type=F32)

    @pl.when((f == nf - 1) & live)
    def _():
        def copy(s):
            return pltpu.make_async_copy(acc.at[pl.ds(s * MOE_BLOCK, MOE_BLOCK)],
                                         yb_hbm.at[pl.ds(row0 + s * MOE_BLOCK, MOE_BLOCK)], sem_out)
        for s in range(tb // MOE_BLOCK):
            @pl.when(s < nsub)
            def _(s=s):
                copy(s).start()
        for s in range(tb // MOE_BLOCK):
            @pl.when(s < nsub)
            def _(s=s):
                copy(s).wait()


def moe_ffn(xb, yb, item_e, item_row, item_nsub, w_gu, b_gu, w_dn, b_dn, layer, *, tb, tf):
    n_layers, n_exp, d, f2 = w_gu.shape
    dff = f2 // 2
    nf = dff // tf
    n_items = item_e.shape[0]
    n_split = 1
    perm = np.zeros((2 * tf, tf), np.float32)
    perm[2 * np.arange(tf) + 1, np.arange(tf)] = 1.0

    def fsel(w, f, ns):
        return jnp.where(ns[w] > 0, f, nf - 1)

    grid_spec = pltpu.PrefetchScalarGridSpec(
        num_scalar_prefetch=3,
        grid=(n_items, nf),
        in_specs=[pl.BlockSpec(memory_space=pl.ANY)] + [
            pl.BlockSpec((None, None, d // n_split, 2 * tf),
                         lambda w, f, e, r_, ns, q=q: (layer, e[w], q, fsel(w, f, ns))) for q in range(n_split)] + [
                  pl.BlockSpec((None, None, 1, 2 * tf), lambda w, f, e, r_, ns: (layer, e[w], 0, fsel(w, f, ns))),
                  pl.BlockSpec((None, None, tf, d), lambda w, f, e, r_, ns: (layer, e[w], fsel(w, f, ns), 0)),
                  pl.BlockSpec((None, None, 1, d), lambda w, f, e, *_: (layer, e[w], 0, 0)),
                  pl.BlockSpec((2 * tf, tf), lambda w, f, *_: (0, 0)),
                  pl.BlockSpec(memory_space=pl.ANY)],
        out_specs=pl.BlockSpec(memory_space=pl.ANY),
        scratch_shapes=[pltpu.VMEM((tb, d), F32), pltpu.VMEM((tb, d), BF16), pltpu.VMEM((tb, d), F32),
                        pltpu.VMEM((d, 2 * tf), BF16), pltpu.VMEM((tf, d), BF16),
                        pltpu.SemaphoreType.DMA(()), pltpu.SemaphoreType.DMA(())])
    return pl.pallas_call(
        functools.partial(_ffn_kernel, tb=tb, n_split=n_split),
        out_shape=jax.ShapeDtypeStruct(yb.shape, F32),
        grid_spec=grid_spec,
        input_output_aliases={8 + n_split: 0},
        compiler_params=_params(("arbitrary", "arbitrary"), has_side_effects=True),
        name="moe_ffn",
    )(item_e, item_row, item_nsub, xb, *([w_gu] * n_split), b_gu.reshape(n_layers, n_exp, 1, f2), w_dn,
      b_dn.reshape(n_layers, n_exp, 1, d), jnp.asarray(perm, BF16), yb)


def _combine_kernel(dest_ref, yb_hbm, meta_ref, x_ref, gate_ref, o_ref, ybuf, sem, *, tm):
    i = pl.program_id(0)

    def row(t, carry):
        for k in range(TOP_K):
            src = dest_ref[(i * tm + t) * TOP_K + k]
            pltpu.make_async_copy(yb_hbm.at[pl.ds(src, 1)], ybuf.at[k, pl.ds(t, 1)], sem).start()
        return carry

    lax.fori_loop(0, tm, row, 0)

    def drain(t, carry):
        for k in range(TOP_K):
            pltpu.make_async_copy(yb_hbm.at[pl.ds(0, 1)], ybuf.at[0, pl.ds(0, 1)], sem).wait()
        return carry

    lax.fori_loop(0, tm, drain, 0)
    meta = meta_ref[...]
    y = jnp.zeros(x_ref.shape, F32)
    for k in range(TOP_K):
        y = y + meta[:, META_W + k:META_W + k + 1] * ybuf[k]
    o_ref[...] = x_ref[...] + gate_ref[...] * y


def moe_combine(st, yb, dest, meta, x, mods):
    r, d = x.shape
    tm = min(st.tm, 256)
    tps = st.tiles_per_seq * (st.tm // tm)
    rows = st.r if st.per_row_mods else 1
    grid_spec = pltpu.PrefetchScalarGridSpec(
        num_scalar_prefetch=1,
        grid=(r // tm,),
        in_specs=[pl.BlockSpec(memory_space=pl.ANY),
                  pl.BlockSpec((tm, LANES), lambda i, *_: (i, 0)),
                  pl.BlockSpec((tm, d), lambda i, *_: (i, 0)),
                  pl.BlockSpec((None, rows, d), lambda i, *_: (i // tps, 0, 5))],
        out_specs=pl.BlockSpec((tm, d), lambda i, *_: (i, 0)),
        scratch_shapes=[pltpu.VMEM((TOP_K, tm, d), F32), pltpu.SemaphoreType.DMA(())])
    return pl.pallas_call(
        functools.partial(_combine_kernel, tm=tm),
        out_shape=jax.ShapeDtypeStruct((r, d), F32),
        grid_spec=grid_spec,
        compiler_params=_params(("arbitrary",)),
        name="moe_combine",
    )(dest, yb, meta, x, mods)


def moe_layer(streams, xs, g, mods_list, router_w, router_b, w_gu, b_gu, w_dn, b_dn, layer):
    d = xs[0].shape[1]
    n_exp = router_w.shape[1]
    rw = jnp.zeros((d, LANES), BF16).at[:, :n_exp].set(router_w.astype(BF16))
    rb = jnp.full((1, LANES), NEG, F32).at[0, :n_exp].set(router_b)
    counts = jnp.zeros((1, LANES), F32)
    hs, metas = [], []
    for st, x, mods in zip(streams, xs, mods_list):
        h, meta, counts = moe_route(st, x, g, mods, rw, rb, counts)
        hs.append(h)
        metas.append(meta)
    n_asg = sum(st.r for st in streams) * TOP_K
    tb = MOE_TB
    cnt = counts[0, :n_exp].astype(I32)
    padded = (cnt + MOE_BLOCK - 1) // MOE_BLOCK * MOE_BLOCK
    pend = jnp.cumsum(padded)
    pstart = pend - padded
    n_items = (n_asg + n_exp * (MOE_BLOCK - 1)) // tb + n_exp + 1
    per_e = (padded + tb - 1) // tb
    iend = jnp.cumsum(per_e)
    total = iend[-1]
    wi = jnp.arange(n_items, dtype=I32)
    live = wi < total
    e_of = jnp.minimum(jnp.sum(wi[:, None] >= iend[None, :], axis=1), n_exp - 1).astype(I32)
    last_e = jnp.minimum(jnp.sum(total - 1 >= iend), n_exp - 1).astype(I32)
    k_of = wi - (iend - per_e)[e_of]
    item_e = jnp.where(live, e_of, last_e).astype(I32)
    item_row = jnp.where(live, pstart[e_of] + k_of * tb, 0).astype(I32)
    item_nsub = jnp.where(live, jnp.minimum(tb, padded[e_of] - k_of * tb) // MOE_BLOCK, 0).astype(I32)
    n_rows = n_asg + n_exp * (MOE_BLOCK - 1) + tb
    n_rows = -(-n_rows // MOE_BLOCK) * MOE_BLOCK
    dests = []
    for meta in metas:
        e = meta[:, META_E:META_E + TOP_K].astype(I32)
        rank = meta[:, META_RANK:META_RANK + TOP_K].astype(I32)
        dests.append((pstart[e] + rank).reshape(-1).astype(I32))
    xb = moe_dispatch(streams, hs, jnp.concatenate(dests), pend.astype(I32), n_rows)
    tf = _pick_tile(w_dn.shape[2], 256)
    yb = moe_ffn(xb, jnp.zeros((n_rows - tb, d), F32), item_e, item_row, item_nsub, w_gu, b_gu, w_dn, b_dn, layer,
                 tb=tb, tf=tf)
    return [moe_combine(st, yb, dest, meta, x, mods)
            for st, dest, meta, x, mods in zip(streams, dests, metas, xs, mods_list)]


def kv_proj(st, x, g, mods, w_bf, kn_kv):
    n = w_bf.shape[1]
    tn = B_KV_HEADS * HEAD_DIM
    j = np.arange(n // tn)
    return mod_linear(st, x, g, mods, 0, 1, w_bf, tn=tn, name="kv_proj", flags=(j % 2 == 0).astype(np.int32),
                      gidx=(j // 2).astype(np.int32), gains=kn_kv)


def q_proj_b(st, x, g, mods, wq_bf, qn, w_gate, b_gate):
    d = x.shape[1]
    n = wq_bf.shape[1]
    nj = n // 512
    q = mod_linear(st, x, g, mods, 0, 1, wq_bf, tn=512, name="q_proj_b", flags=np.ones((nj,), np.int32),
                   gidx=np.zeros((nj,), np.int32), gains=qn)
    per_g = B_HEADS_PER_KV * 3
    wg = jnp.zeros((d, B_KV_HEADS, LANES), F32).at[:, :, :per_g].set(w_gate.reshape(d, B_KV_HEADS, per_g))
    bg = jnp.zeros((B_KV_HEADS, LANES), F32).at[:, :per_g].set(b_gate.reshape(B_KV_HEADS, per_g))
    gates = mod_linear(st, x, g, mods, 0, 1, wg.reshape(d, -1).astype(BF16), tn=B_KV_HEADS * LANES,
                       name="gates_b", bias=bg.reshape(-1), act="sigmoid")
    return q, gates


def _cmp1_kernel(*refs, n_pages, cpp):
    page_refs = refs[:n_pages]
    pe_ref, w1_ref, p_ref, q_ref, rows_scr = refs[n_pages:]
    ch = n_pages * cpp
    rpr = cpp * CMP_STRIDE
    half_len = CMP_LEN // 2
    for j, pr in enumerate(page_refs):
        for c in range(2 * B_KV_HEADS):
            if len(pr.shape) == 4:
                rows_scr[c, j * rpr:(j + 1) * rpr, :] = pr[:, c // B_KV_HEADS, c % B_KV_HEADS, :]
            else:
                rows_scr[c, j * rpr:(j + 1) * rpr, :] = pr[:, c * HEAD_DIM:(c + 1) * HEAD_DIM]
    for kind in range(2):
        for half, out in ((0, p_ref), (1, q_ref)):
            acc = jnp.zeros((B_KV_HEADS * ch, w1_ref.shape[-1]), F32)
            for l in range(half_len):
                pieces = [rows_scr[kind * B_KV_HEADS + g, pl.ds(l, ch, stride=CMP_STRIDE), :]
                          for g in range(B_KV_HEADS)]
                xl = jnp.concatenate(pieces, axis=0) + pe_ref[kind, pl.ds(half * half_len + l, 1), :]
                acc = acc + _dot(xl, w1_ref[kind, half * half_len + l])
            for g in range(B_KV_HEADS):
                out[kind, g] = acc[g * ch:(g + 1) * ch]


def compress_partials(rows_arr, spec_fn, n_refs, rows_per_ref, grid, pe, w1_bf, n_chunks, prefetch=None):
    assert CMP_LEN == 2 * CMP_STRIDE
    cpp = rows_per_ref // CMP_STRIDE
    ch = n_refs * cpp
    hid = w1_bf.shape[-1]
    nb = grid[0]
    out_shape = jax.ShapeDtypeStruct((nb, 2, B_KV_HEADS, n_chunks, hid), F32)
    out_spec = pl.BlockSpec((None, 2, B_KV_HEADS, ch, hid), lambda b, i, *_: (b, 0, 0, i, 0))
    in_specs = [spec_fn(j) for j in range(n_refs)] + [
        pl.BlockSpec(pe.shape, lambda b, i, *_: (0, 0, 0)),
        pl.BlockSpec(w1_bf.shape, lambda b, i, *_: (0, 0, 0, 0))]
    kern = functools.partial(_cmp1_kernel, n_pages=n_refs, cpp=cpp)
    args = [rows_arr] * n_refs + [pe, w1_bf]
    scratch = [pltpu.VMEM((2 * B_KV_HEADS, ch * CMP_STRIDE, HEAD_DIM), F32)]
    if prefetch is None:
        return pl.pallas_call(kern, out_shape=(out_shape, out_shape), grid=grid, in_specs=in_specs,
                              out_specs=(out_spec, out_spec), scratch_shapes=scratch,
                              compiler_params=_params(("arbitrary", "arbitrary")), name="cmp_partials")(*args)
    gs = pltpu.PrefetchScalarGridSpec(num_scalar_prefetch=1, grid=grid, in_specs=in_specs,
                                      out_specs=(out_spec, out_spec), scratch_shapes=scratch)
    return pl.pallas_call(lambda pt, *r: kern(*r), out_shape=(out_shape, out_shape), grid_spec=gs,
                          compiler_params=_params(("arbitrary", "arbitrary")), name="cmp_partials_paged")(prefetch, *args)


def _cmp2_kernel(p_ref, q_ref, b1_ref, w2_ref, b2_ref, o_ref):
    q = q_ref[...]
    qs = jnp.concatenate([q[1:], jnp.zeros((1, q.shape[1]), F32)], axis=0)
    hid = jax.nn.gelu(p_ref[...] + qs + b1_ref[...])
    o_ref[...] = _dot(hid, w2_ref[...]) + b2_ref[...]


def compress_finish(p, q, b1, w2_bf, b2):
    nb, _, ng, nc, hid = p.shape
    blk = pl.BlockSpec((None, None, None, nc, hid), lambda b, k, g: (b, k, g, 0, 0))
    return pl.pallas_call(
        _cmp2_kernel,
        out_shape=jax.ShapeDtypeStruct((nb, 2, ng, nc, HEAD_DIM), F32),
        grid=(nb, 2, ng),
        in_specs=[blk, blk,
                  pl.BlockSpec((None, 1, hid), lambda b, k, g: (k, 0, 0)),
                  pl.BlockSpec((None, hid, HEAD_DIM), lambda b, k, g: (k, 0, 0)),
                  pl.BlockSpec((None, 1, HEAD_DIM), lambda b, k, g: (k, 0, 0))],
        out_specs=pl.BlockSpec((None, None, None, nc, HEAD_DIM), lambda b, k, g: (b, k, g, 0, 0)),
        compiler_params=_params(("arbitrary", "arbitrary", "arbitrary")),
        name="cmp_finish",
    )(p, q, b1.reshape(2, 1, hid), w2_bf, b2.reshape(2, 1, HEAD_DIM))


def _selection_map(n_cmp_pad, n_cmp, n_slc, n_slc_pad):
    r_s = SLC_LEN // CMP_STRIDE
    r_c = CMP_LEN // CMP_STRIDE
    mult = np.zeros(r_s + r_c - 1, np.float32)
    for m in range(r_s):
        for n in range(r_c):
            mult[m + n] += 1.0
    off = r_s * np.arange(n_slc)[None, :] - np.arange(n_cmp)[:, None]
    ok = (off >= 0) & (off < mult.shape[0])
    out = np.zeros((n_cmp_pad, n_slc_pad), np.float32)
    out[:n_cmp, :n_slc] = np.where(ok, mult[np.clip(off, 0, mult.shape[0] - 1)], 0.0)
    return out


def _stack_heads(q):
    return jnp.concatenate([q[:, h * HEAD_DIM:(h + 1) * HEAD_DIM] for h in range(q.shape[1] // HEAD_DIM)], axis=0)


def _rep_rows(x, n):
    return jnp.concatenate([x] * n, axis=0)


def _cmp_and_select(qs, pos, kc, vc, selmap, *, n_cmp, n_slc, n_sel):
    tq = pos.shape[0]
    hpk = qs.shape[0] // tq
    pos_h = _rep_rows(pos, hpk)
    ncp = kc.shape[0]
    s = _dot_nt(qs, kc) * (HEAD_DIM ** -0.5)
    ci = lax.broadcasted_iota(I32, (hpk * tq, ncp), 1)
    vis = (ci * CMP_STRIDE + (CMP_LEN - 1) <= pos_h) & (ci < n_cmp)
    s = jnp.where(vis, s, NEG)
    e = jnp.where(vis, jnp.exp(s - jnp.max(s, axis=-1, keepdims=True)), 0.0)
    p = e / jnp.maximum(jnp.sum(e, axis=-1, keepdims=True), 1e-30)
    o_cmp = _dot(p, vc)
    pi = _dot(p, selmap)
    imp = pi[0:tq]
    for h in range(1, hpk):
        imp = imp + pi[h * tq:(h + 1) * tq]
    ns = imp.shape[1]
    blk = lax.broadcasted_iota(I32, (tq, ns), 1)
    cur = pos // SLC_LEN
    forced = (blk == 0) | (blk == cur) | (blk == cur - 1)
    valid = blk * SLC_LEN <= pos
    score = jnp.where(valid, imp + jnp.where(forced, FORCE_BONUS, 0.0), -FORCE_BONUS)
    score = jnp.where(blk < n_slc, score, -jnp.inf)

    half = LANES // 2
    if ns == LANES and n_slc <= half and tq % 8 == 0:
        both = jnp.where(blk < half, score, pltpu.roll(score, half, 1))
        j8 = lax.broadcasted_iota(I32, (8, LANES), 1) % half
        rank = jnp.zeros((tq, ns), F32)
        for dd in range(1, half):
            other = pltpu.roll(both, dd, 1)
            tie = _rep_rows(jnp.where(j8 >= dd, 1.0, 0.0), tq // 8)
            rank = rank + jnp.where(other > both, 1.0, 0.0) + jnp.where(other == both, tie, 0.0)
        sel = jnp.where(blk < n_slc, jnp.where(rank < n_sel, 1.0, 0.0), 0.0)
        return o_cmp, sel

    def pick(_, carry):
        sel, work = carry
        v = jnp.max(work, axis=-1, keepdims=True)
        idx = jnp.min(jnp.where(work == v, blk, ns), axis=-1, keepdims=True)
        hit = blk == idx
        return jnp.where(hit, 1.0, sel), jnp.where(hit, -jnp.inf, work)

    sel, _ = lax.fori_loop(0, n_sel, pick, (jnp.zeros((tq, ns), F32), score))
    return o_cmp, sel


def _flash_tiles(qs, k_ref, v_ref, lo, hi, tk, hpk, bias_fn):
    rows = qs.shape[0]
    tq = rows // hpk

    def body(kt, carry):
        m, l, acc = carry
        start = pl.multiple_of(kt * tk, tk)
        k = k_ref[pl.ds(start, tk), :]
        v = v_ref[pl.ds(start, tk), :]
        kpos = kt * tk + lax.broadcasted_iota(I32, (tq, tk), 1)
        s = _dot_nt(qs, k) * (HEAD_DIM ** -0.5) + _rep_rows(bias_fn(kt, kpos), hpk)
        m_new = jnp.maximum(m, jnp.max(s, axis=-1, keepdims=True))
        alpha = jnp.exp(m - m_new)
        p = jnp.exp(s - m_new)
        return m_new, alpha * l + jnp.sum(p, axis=-1, keepdims=True), alpha * acc + _dot(p, v)

    init = (jnp.full((rows, 1), NEG, F32), jnp.zeros((rows, 1), F32), jnp.zeros((rows, HEAD_DIM), F32))
    m, l, acc = lax.fori_loop(lo, hi, body, init)
    return acc / jnp.maximum(l, 1e-30)


def _gate_cols(gt, branch, hpk):
    return jnp.concatenate([gt[:, h * 3 + branch:h * 3 + branch + 1] for h in range(hpk)], axis=0)


def _nsa_prompt_kernel(q_ref, gate_ref, kc_ref, vc_ref, ks_ref, vs_ref, kw_ref, vw_ref, selmap_ref, expand_ref,
                       o_ref, *, tq, tk, wspan, n_cmp, n_slc, n_sel):
    qi = pl.program_id(2)
    hpk = B_HEADS_PER_KV
    qs = _stack_heads(q_ref[...]).astype(BF16)
    pos = qi * tq + lax.broadcasted_iota(I32, (tq, 1), 0)
    o_cmp, sel = _cmp_and_select(qs, pos, kc_ref[...], vc_ref[...], selmap_ref[...],
                                 n_cmp=n_cmp, n_slc=n_slc, n_sel=n_sel)
    sel_bf = sel.astype(BF16)
    hi = (qi * tq + tq - 1) // tk + 1

    def bias_slc(kt, kpos):
        picked = jnp.dot(sel_bf, expand_ref[kt], preferred_element_type=F32)
        return jnp.where(kpos <= pos, (picked - 1.0) * (-NEG), NEG)

    o_slc = _flash_tiles(qs, ks_ref, vs_ref, 0, hi, tk, hpk, bias_slc)

    start = pl.multiple_of(jnp.maximum(qi * tq + tq - wspan, 0), 8)
    kpos = start + lax.broadcasted_iota(I32, (tq, wspan), 1)
    rel = pos - kpos
    bias_w = _rep_rows(jnp.where(rel >= 0, jnp.where(rel < B_WINDOW, 0.0, NEG), NEG), hpk)
    s = _dot_nt(qs, kw_ref[pl.ds(start, wspan), :]) * (HEAD_DIM ** -0.5) + bias_w
    p = jnp.exp(s - jnp.max(s, axis=-1, keepdims=True))
    o_win = _dot(p, vw_ref[pl.ds(start, wspan), :]) / jnp.sum(p, axis=-1, keepdims=True)

    gt = gate_ref[...]
    o = _gate_cols(gt, 0, hpk) * o_cmp + _gate_cols(gt, 1, hpk) * o_slc + _gate_cols(gt, 2, hpk) * o_win
    for h in range(hpk):
        o_ref[:, h * HEAD_DIM:(h + 1) * HEAD_DIM] = o[h * tq:(h + 1) * tq]


def nsa_prompt(st, q, gates, kvs, cmp_kv):
    t = st.t
    tq = _pick_tile(t, 128, 8)
    tk = _pick_tile(t, 512, 8)
    wspan = min(t, -(-(tq + B_WINDOW - 1) // LANES) * LANES)
    nq = t // tq
    ncp = cmp_kv.shape[3]
    n_cmp = t // CMP_STRIDE - CMP_LEN // CMP_STRIDE + 1
    n_slc = -(-t // SLC_LEN)
    ns = -(-n_slc // LANES) * LANES
    selmap = jnp.asarray(_selection_map(ncp, n_cmp, n_slc, ns), BF16)
    expand = np.zeros((t // tk, ns, tk), np.float32)
    kk = np.arange(t)
    expand[kk // tk, kk // SLC_LEN, kk % tk] = 1.0
    gw = B_HEADS_PER_KV * HEAD_DIM
    kv_blk = lambda col: pl.BlockSpec((t, HEAD_DIM), lambda b, g, i: (b, col + g))
    cmp_blk = lambda kind: pl.BlockSpec((None, None, None, ncp, HEAD_DIM), lambda b, g, i: (b, kind, g, 0, 0))
    return pl.pallas_call(
        functools.partial(_nsa_prompt_kernel, tq=tq, tk=tk, wspan=wspan, n_cmp=n_cmp, n_slc=n_slc, n_sel=min(N_SELECT, n_slc)),
        out_shape=jax.ShapeDtypeStruct((st.r, B_HEADS * HEAD_DIM), F32),
        grid=(st.b, B_KV_HEADS, nq),
        in_specs=[pl.BlockSpec((tq, gw), lambda b, g, i: (b * nq + i, g)),
                  pl.BlockSpec((tq, LANES), lambda b, g, i: (b * nq + i, g)),
                  cmp_blk(0), cmp_blk(1),
                  kv_blk(2 * B_KV_HEADS), kv_blk(3 * B_KV_HEADS), kv_blk(4 * B_KV_HEADS), kv_blk(5 * B_KV_HEADS),
                  pl.BlockSpec(selmap.shape, lambda b, g, i: (0, 0)),
                  pl.BlockSpec(expand.shape, lambda b, g, i: (0, 0, 0))],
        out_specs=pl.BlockSpec((tq, gw), lambda b, g, i: (b * nq + i, g)),
        compiler_params=_params(("arbitrary", "arbitrary", "arbitrary")),
        name="nsa_prompt",
    )(q, gates, cmp_kv, cmp_kv, kvs, kvs, kvs, kvs, selmap, jnp.asarray(expand, BF16))


def _nsa_s1_kernel(q_ref, kc_ref, vc_ref, selmap_ref, ocmp_ref, sel_ref, *, tq, past_len, n_cmp, n_slc, n_sel):
    qs = _stack_heads(q_ref[...]).astype(BF16)
    pos = past_len + lax.broadcasted_iota(I32, (tq, 1), 0)
    o_cmp, sel = _cmp_and_select(qs, pos, kc_ref[...], vc_ref[...], selmap_ref[...],
                                 n_cmp=n_cmp, n_slc=n_slc, n_sel=n_sel)
    ocmp_ref[...] = o_cmp
    sel_ref[...] = sel


def _nsa_s2_kernel(*refs, tq, past_len, pps):
    q_ref = refs[1]
    page_refs = refs[2:2 + pps]
    sel_ref, new_ref, o_ref, m_scr, l_scr, acc_scr = refs[2 + pps:]
    p = pl.program_id(1)
    n_steps = pl.num_programs(1)
    hpk = B_HEADS_PER_KV
    gw = B_KV_HEADS * HEAD_DIM
    scale = HEAD_DIM ** -0.5
    ns = sel_ref.shape[-1]

    @pl.when(p == 0)
    def _():
        m_scr[...] = jnp.full(m_scr.shape, NEG, F32)
        l_scr[...] = jnp.zeros(l_scr.shape, F32)
        acc_scr[...] = jnp.zeros(acc_scr.shape, F32)

    q = q_ref[...]

    def update(g, s, ok, v):
        s = jnp.where(ok, s, NEG)
        m_old = m_scr[g]
        m_new = jnp.maximum(m_old, jnp.max(s, axis=-1, keepdims=True))
        alpha = jnp.exp(m_old - m_new)
        pr = jnp.where(ok, jnp.exp(s - m_new), 0.0)
        l_scr[g] = alpha * l_scr[g] + jnp.sum(pr, axis=-1, keepdims=True)
        acc_scr[g] = alpha * acc_scr[g] + _dot(pr, v)
        m_scr[g] = m_new

    nk = pps * PAGE_SIZE
    nn = lax.broadcasted_iota(I32, (ns, nk), 0)
    jj = lax.broadcasted_iota(I32, (ns, nk), 1)
    expand = jnp.where(nn == p * (nk // SLC_LEN) + jj // SLC_LEN, 1.0, 0.0).astype(BF16)
    for g in range(B_KV_HEADS):
        qs = _stack_heads(q[:, g * hpk * HEAD_DIM:(g + 1) * hpk * HEAD_DIM]).astype(BF16)
        k = jnp.concatenate([pr[:, 0, g, :] for pr in page_refs], axis=0)
        v = jnp.concatenate([pr[:, 1, g, :] for pr in page_refs], axis=0)
        picked = jnp.dot(sel_ref[g].astype(BF16), expand, preferred_element_type=F32)
        update(g, _dot_nt(qs, k) * scale, _rep_rows(picked, hpk) > 0.5, v)

    @pl.when(p == n_steps - 1)
    def _():
        nn2 = lax.broadcasted_iota(I32, (ns, tq), 0)
        tt2 = lax.broadcasted_iota(I32, (ns, tq), 1)
        expand2 = jnp.where(nn2 == (past_len + tt2) // SLC_LEN, 1.0, 0.0).astype(BF16)
        t_q = _rep_rows(lax.broadcasted_iota(I32, (tq, tq), 0), hpk)
        t_k = _rep_rows(lax.broadcasted_iota(I32, (tq, tq), 1), hpk)
        for g in range(B_KV_HEADS):
            qs = _stack_heads(q[:, g * hpk * HEAD_DIM:(g + 1) * hpk * HEAD_DIM]).astype(BF16)
            k = new_ref[:, g * HEAD_DIM:(g + 1) * HEAD_DIM]
            v = new_ref[:, gw + g * HEAD_DIM:gw + (g + 1) * HEAD_DIM]
            picked = jnp.dot(sel_ref[g].astype(BF16), expand2, preferred_element_type=F32)
            update(g, _dot_nt(qs, k) * scale, (t_k <= t_q) & (_rep_rows(picked, hpk) > 0.5), v)
            o_ref[g] = acc_scr[g] / jnp.maximum(l_scr[g], 1e-30)


def _nsa_s3_kernel(q_ref, gate_ref, win_ref, new_ref, ocmp_ref, oslc_ref, o_ref, *, tq):
    hpk = B_HEADS_PER_KV
    gw = B_KV_HEADS * HEAD_DIM
    scale = HEAD_DIM ** -0.5
    lb = win_ref.shape[0]
    rows = hpk * tq
    q = q_ref[...]
    gt_all = gate_ref[...]
    t_q1 = _rep_rows(lax.broadcasted_iota(I32, (tq, lb), 0), hpk)
    i_k1 = _rep_rows(lax.broadcasted_iota(I32, (tq, lb), 1), hpk)
    ok1 = (t_q1 + lb - i_k1) < B_WINDOW
    t_q2 = _rep_rows(lax.broadcasted_iota(I32, (tq, tq), 0), hpk)
    t_k2 = _rep_rows(lax.broadcasted_iota(I32, (tq, tq), 1), hpk)
    ok2 = t_k2 <= t_q2
    for g in range(B_KV_HEADS):
        qs = _stack_heads(q[:, g * hpk * HEAD_DIM:(g + 1) * hpk * HEAD_DIM]).astype(BF16)
        s1 = jnp.where(ok1, _dot_nt(qs, win_ref[:, g * HEAD_DIM:(g + 1) * HEAD_DIM]) * scale, NEG)
        s2 = jnp.where(ok2, _dot_nt(qs, new_ref[:, g * HEAD_DIM:(g + 1) * HEAD_DIM]) * scale, NEG)
        m = jnp.maximum(jnp.max(s1, axis=-1, keepdims=True), jnp.max(s2, axis=-1, keepdims=True))
        p1 = jnp.where(ok1, jnp.exp(s1 - m), 0.0)
        p2 = jnp.where(ok2, jnp.exp(s2 - m), 0.0)
        den = jnp.sum(p1, axis=-1, keepdims=True) + jnp.sum(p2, axis=-1, keepdims=True)
        o_win = (_dot(p1, win_ref[:, gw + g * HEAD_DIM:gw + (g + 1) * HEAD_DIM])
                 + _dot(p2, new_ref[:, gw + g * HEAD_DIM:gw + (g + 1) * HEAD_DIM])) / jnp.maximum(den, 1e-30)
        gt = gt_all[:, g * LANES:(g + 1) * LANES]
        o = (_gate_cols(gt, 0, hpk) * ocmp_ref[g] + _gate_cols(gt, 1, hpk) * oslc_ref[g]
             + _gate_cols(gt, 2, hpk) * o_win)
        for h in range(hpk):
            col = (g * hpk + h) * HEAD_DIM
            o_ref[:, col:col + HEAD_DIM] = o[h * tq:(h + 1) * tq]


def nsa_sample(st, q, gates, kvs, cmp_kv, cache, page_table, state_win, past_len):
    nb, tq = st.b, st.t
    hpk = B_HEADS_PER_KV
    rows = hpk * tq
    ncp = cmp_kv.shape[3]
    l_all = past_len + tq
    n_cmp = l_all // CMP_STRIDE - CMP_LEN // CMP_STRIDE + 1
    n_slc = -(-l_all // SLC_LEN)
    ns = -(-n_slc // LANES) * LANES
    assert n_cmp <= ncp
    selmap = jnp.asarray(_selection_map(ncp, n_cmp, n_slc, ns), BF16)
    gw = hpk * HEAD_DIM
    cmp_blk = lambda kind: pl.BlockSpec((None, None, None, ncp, HEAD_DIM), lambda b, g: (b, kind, g, 0, 0))
    o_cmp, sel = pl.pallas_call(
        functools.partial(_nsa_s1_kernel, tq=tq, past_len=past_len, n_cmp=n_cmp, n_slc=n_slc,
                          n_sel=min(N_SELECT, n_slc)),
        out_shape=(jax.ShapeDtypeStruct((nb, B_KV_HEADS, rows, HEAD_DIM), F32),
                   jax.ShapeDtypeStruct((nb, B_KV_HEADS, tq, ns), F32)),
        grid=(nb, B_KV_HEADS),
        in_specs=[pl.BlockSpec((tq, gw), lambda b, g: (b, g)), cmp_blk(0), cmp_blk(1),
                  pl.BlockSpec(selmap.shape, lambda b, g: (0, 0))],
        out_specs=(pl.BlockSpec((None, None, rows, HEAD_DIM), lambda b, g: (b, g, 0, 0)),
                   pl.BlockSpec((None, None, tq, ns), lambda b, g: (b, g, 0, 0))),
        compiler_params=_params(("arbitrary", "arbitrary")),
        name="nsa_sample_cmp",
    )(q, cmp_kv, cmp_kv, selmap)

    n_pages = page_table.shape[1]
    half = 2 * B_KV_HEADS * HEAD_DIM
    grp = pl.BlockSpec((None, B_KV_HEADS, rows, HEAD_DIM), lambda b, *_: (b, 0, 0, 0))
    pps = _pick_tile(n_pages, 8, 1)
    page_spec = lambda j: pl.BlockSpec((None, PAGE_SIZE, 2, B_KV_HEADS, HEAD_DIM),
                                       lambda b, p, pt: (pt[b * n_pages + p * pps + j], 0, 1, 0, 0))
    o_slc = pl.pallas_call(
        functools.partial(_nsa_s2_kernel, tq=tq, past_len=past_len, pps=pps),
        out_shape=jax.ShapeDtypeStruct((nb, B_KV_HEADS, rows, HEAD_DIM), F32),
        grid_spec=pltpu.PrefetchScalarGridSpec(
            num_scalar_prefetch=1,
            grid=(nb, n_pages // pps),
            in_specs=[pl.BlockSpec((tq, B_HEADS * HEAD_DIM), lambda b, p, pt: (b, 0))]
            + [page_spec(j) for j in range(pps)]
            + [pl.BlockSpec((None, B_KV_HEADS, tq, ns), lambda b, p, pt: (b, 0, 0, 0)),
               pl.BlockSpec((tq, half), lambda b, p, pt: (b, 1))],
            out_specs=grp,
            scratch_shapes=[pltpu.VMEM((B_KV_HEADS, rows, 1), F32), pltpu.VMEM((B_KV_HEADS, rows, 1), F32),
                            pltpu.VMEM((B_KV_HEADS, rows, HEAD_DIM), F32)]),
        compiler_params=_params(("arbitrary", "arbitrary")),
        name="nsa_sample_slc",
    )(page_table.reshape(-1), q, *([cache] * pps), sel, kvs)

    lb = state_win.shape[1]
    win2 = state_win.reshape(nb, lb, 2 * B_KV_HEADS * HEAD_DIM)
    return pl.pallas_call(
        functools.partial(_nsa_s3_kernel, tq=tq),
        out_shape=jax.ShapeDtypeStruct((st.r, B_HEADS * HEAD_DIM), F32),
        grid=(nb,),
        in_specs=[pl.BlockSpec((tq, B_HEADS * HEAD_DIM), lambda b: (b, 0)),
                  pl.BlockSpec((tq, B_KV_HEADS * LANES), lambda b: (b, 0)),
                  pl.BlockSpec((None, lb, half), lambda b: (b, 0, 0)),
                  pl.BlockSpec((tq, half), lambda b: (b, 2)),
                  grp, grp],
        out_specs=pl.BlockSpec((tq, B_HEADS * HEAD_DIM), lambda b: (b, 0)),
        compiler_params=_params(("arbitrary",)),
        name="nsa_sample_win",
    )(q, gates, win2, kvs, o_cmp, o_slc)


def kernel(x_prompt, x_sample, state_dil0, state_dil1, state_dil2, cache_nsa, state_win, page_table, c_prompt, c_sample, w_mod, b_mod, g_norm, w_qkv_a, w_o_a, qn_a, kn_a, w_q_b, qn_b, w_gate_b, b_gate_b, w_o_b, w_mod_kv, b_mod_kv, g_kv, w_kv, kn_kv, cmp_pe, cmp_w1, cmp_b1, cmp_w2, cmp_b2, router_w, router_b, w_gu, b_gu, w_down, b_down):
    bp, tp, d = x_prompt.shape
    bs, ts, _ = x_sample.shape
    past_len = page_table.shape[1] * PAGE_SIZE
    stp = Stream(bp, tp, jnp.arange(tp), 512)
    sts = Stream(bs, ts, past_len + jnp.arange(ts), 512)
    streams = [stp, sts]
    n_pages = page_table.shape[1]
    depth = w_mod.shape[0]
    n_a = w_qkv_a.shape[0]
    ng = len(DIL_RATES)
    dil_bufs = (state_dil0, state_dil1, state_dil2)

    c_all = jnp.concatenate([c_prompt, c_sample], axis=0)
    m_all = mod_vectors(c_all, w_mod, b_mod)
    m_kv = mod_vectors(c_all, w_mod_kv[None], b_mod_kv[None])[0]

    def split(m):
        return [stp.mods(m[:bp]), sts.mods(m[bp:])]

    xs = [x_prompt.reshape(-1, d), x_sample.reshape(-1, d)]
    a_qkv = []
    kvs = cmp_ctx = None
    for l in range(depth):
        mods = split(m_all[l])
        g_attn = g_norm[l, 0]
        if l < n_a:
            wq, wo = w_qkv_a[l].astype(BF16), w_o_a[l].astype(BF16)
            qkvs = [qkv_a(st, x, g_attn, md, wq, qn_a[l], kn_a[l]) for st, x, md in zip(streams, xs, mods)]
            a_qkv.append(qkvs)
            outs = [dilated_prompt(stp, qkvs[0]), dilated_sample(sts, qkvs[1], dil_bufs, l)]
        else:
            lb = l - n_a
            wq, wo = w_q_b[lb].astype(BF16), w_o_b[lb].astype(BF16)
            qg = [q_proj_b(st, x, g_attn, md, wq, qn_b[lb], w_gate_b[lb], b_gate_b[lb])
                  for st, x, md in zip(streams, xs, mods)]
            outs = [nsa_prompt(stp, qg[0][0], qg[0][1], kvs[0], cmp_ctx[0]),
                    nsa_sample(sts, qg[1][0], qg[1][1], kvs[1], cmp_ctx[1], cache_nsa, page_table, state_win,
                               past_len)]
        xs = [out_proj_residual(st, o, wo, x, md, 2) for st, o, x, md in zip(streams, outs, xs, mods)]
        xs = moe_layer(streams, xs, g_norm[l, 1], mods, router_w[l], router_b[l], w_gu, b_gu, w_down, b_down, l)
        if l == n_a - 1:
            wkv = w_kv.astype(BF16)
            kvs = [kv_proj(st, x, g_kv, md, wkv, kn_kv) for st, x, md in zip(streams, xs, split(m_kv))]
            w1, w2 = cmp_w1.astype(BF16), cmp_w2.astype(BF16)
            cmp_w = 2 * B_KV_HEADS * HEAD_DIM
            rows_p = _pick_tile(tp, 1024, CMP_STRIDE)
            steps_p = tp // rows_p
            pq = compress_partials(
                kvs[0], lambda j: pl.BlockSpec((rows_p, cmp_w), lambda b, i: (b * steps_p + i, 0)),
                1, rows_p, (bp, steps_p), cmp_pe, w1, tp // CMP_STRIDE)
            cmp_p = compress_finish(pq[0], pq[1], cmp_b1, w2, cmp_b2)
            assert (past_len + ts) // CMP_STRIDE == past_len // CMP_STRIDE
            pps = _pick_tile(n_pages, 8, 1)
            pq = compress_partials(
                cache_nsa,
                lambda j: pl.BlockSpec((None, PAGE_SIZE, 2, B_KV_HEADS, HEAD_DIM),
                                       lambda b, i, pt: (pt[b * n_pages + i * pps + j], 0, 0, 0, 0)),
                pps, PAGE_SIZE, (bs, n_pages // pps), cmp_pe, w1, past_len // CMP_STRIDE,
                prefetch=page_table.reshape(-1))
            cmp_s = compress_finish(pq[0], pq[1], cmp_b1, w2, cmp_b2)
            cmp_ctx = [cmp_p, cmp_s]

    def a_rows(qkv, nb, t, kind, g):
        return qkv.reshape(nb, t, 3, ng, A_HEADS, HEAD_DIM)[:, :, kind, g]

    p_dil, s_new = [], []
    for g in range(ng):
        keep = min(DIL_WINDOWS[g], tp)
        p_dil.append(jnp.stack([
            jnp.stack([a_rows(a_qkv[l][0], bp, tp, 1, g)[:, tp - keep:],
                       a_rows(a_qkv[l][0], bp, tp, 2, g)[:, tp - keep:]], axis=2) for l in range(n_a)]))
        s_new.append(jnp.stack([
            jnp.stack([a_rows(a_qkv[l][1], bs, ts, 1, g), a_rows(a_qkv[l][1], bs, ts, 2, g)], axis=2)
            for l in range(n_a)]))
    s_dil = [roll_state(buf, new, win) for buf, new, win in zip(dil_bufs, s_new, DIL_WINDOWS)]
    rows_w = 4 * B_KV_HEADS * HEAD_DIM
    p_rows = kvs[0][:, :rows_w].reshape(bp, tp, 4, B_KV_HEADS, HEAD_DIM)
    p_winr = kvs[0][:, rows_w:].reshape(bp, tp, 2, B_KV_HEADS, HEAD_DIM)
    p_win = p_winr[:, tp - min(B_WINDOW, tp):]
    s_rows = kvs[1][:, :rows_w].reshape(bs, ts, 4, B_KV_HEADS, HEAD_DIM)
    s_win = roll_state(state_win[None], kvs[1][:, rows_w:].reshape(1, bs, ts, 2, B_KV_HEADS, HEAD_DIM), B_WINDOW)[0]
    return (xs[0].reshape(bp, tp, d), xs[1].reshape(bs, ts, d), p_dil[0], p_dil[1], p_dil[2], p_rows, p_win,
            s_dil[0], s_dil[1], s_dil[2], s_rows, s_win)
```
